```python
import jax, jax.numpy as jnp
from jax import lax
import numpy as np

D_MODEL = 1024
BATCH = 2
SEQ = 16384
DEPTH = 2
DEC_BATCH = 8
DEC_SEQ = 8192
PAST_LEN = 128

MIX_WIDTH = D_MODEL
A_WIDTH = MIX_WIDTH // 2
A_GROUPS = 4
A_GROUP_DIM = A_WIDTH // A_GROUPS
A_CHUNK = 128
B_WIDTH = MIX_WIDTH - A_WIDTH
B_HEADS = 4
B_DK = B_WIDTH // B_HEADS
B_DV = B_DK
B_CHUNK = 64
N_DIR = 2
IN_COLS = 2 * A_WIDTH + 5 * B_WIDTH
MEM_TOKENS = 256
X_HEADS = 4
X_HEAD_DIM = D_MODEL // X_HEADS
D_FF = -(-8 * D_MODEL // (3 * 256)) * 256
EPS = 1e-6

kernel_name = "hybrid_gmlp_hgrn2_memxattn_encoder"


def _rms(x, g):
    xf = x.astype(jnp.float32)
    y = xf * lax.rsqrt(jnp.mean(xf * xf, axis=-1, keepdims=True) + EPS)
    return (y * g.astype(jnp.float32)).astype(x.dtype)


def _layernorm(x, g, b):
    xf = x.astype(jnp.float32)
    mu = jnp.mean(xf, axis=-1, keepdims=True)
    xc = xf - mu
    y = xc * lax.rsqrt(jnp.mean(xc * xc, axis=-1, keepdims=True) + EPS)
    return (y * g.astype(jnp.float32) + b.astype(jnp.float32)).astype(x.dtype)


def _hgrn2_scan(q, k, v, logf):
    Bn, L, H, DK = q.shape
    DV = v.shape[-1]
    n_chunks = L // B_CHUNK

    def to_chunks(t):
        return t.reshape(Bn, n_chunks, B_CHUNK, H, t.shape[-1]).transpose(1, 0, 3, 2, 4)

    qc, kc, vc, gc = to_chunks(q), to_chunks(k), to_chunks(v), to_chunks(logf)
    lower = jnp.tril(jnp.ones((B_CHUNK, B_CHUNK), dtype=bool))[:, :, None]

    def step(S, inp):
        qb, kb, vb, gb = inp
        b = jnp.cumsum(gb, axis=2)
        o_inter = jnp.einsum('bhtd,bhdv->bhtv', qb * jnp.exp(b), S)
        diff = b[:, :, :, None, :] - b[:, :, None, :, :]
        decay = jnp.exp(jnp.where(lower, diff, -jnp.inf))
        scores = jnp.einsum('bhtsd,bhsd->bhts', qb[:, :, :, None, :] * decay, kb)
        o_intra = jnp.einsum('bhts,bhsv->bhtv', scores, vb)
        b_last = b[:, :, -1:, :]
        S_new = jnp.exp(b_last[:, :, 0, :, None]) * S + jnp.einsum(
            'bhsd,bhsv->bhdv', kb * jnp.exp(b_last - b), vb)
        return S_new, o_inter + o_intra

    S0 = jnp.zeros((Bn, H, DK, DV), jnp.float32)
    _, o = lax.scan(step, S0, (qc, kc, vc, gc))
    return o.transpose(1, 0, 3, 2, 4).reshape(Bn, L, H, DV)


def _token_mix(h, l, lb, p):
    Bn, L, _ = h.shape
    proj = h @ p['w_in'][l]
    splits = np.cumsum([A_WIDTH, A_WIDTH, B_WIDTH, B_WIDTH, B_WIDTH, B_WIDTH]).tolist()
    u, v, q_raw, ff_raw, fb_raw, i_raw, g_raw = jnp.split(proj, splits, axis=-1)

    u = jax.nn.gelu(u, approximate=False)
    v = _layernorm(jax.nn.gelu(v, approximate=False), p['v_ln_g'][l], p['v_ln_b'][l])
    vc = v.reshape(Bn, L // A_CHUNK, A_CHUNK, A_GROUPS, A_GROUP_DIM)
    sv = jnp.einsum('gts,bnsgc->bntgc', p['w_spatial'][l].astype(v.dtype), vc)
    sv = sv + p['b_spatial'][l].T[None, None, :, :, None].astype(v.dtype)
    a_out = u * sv.reshape(Bn, L, A_WIDTH)

    q = (jax.nn.silu(q_raw).astype(jnp.float32) * (B_DK ** -0.5)).reshape(Bn, L, B_HEADS, B_DK)
    val = i_raw.astype(jnp.float32).reshape(Bn, L, B_HEADS, B_DV)
    o_sum = jnp.zeros((Bn, L, B_HEADS, B_DV), jnp.float32)
    for d, f_raw in ((0, ff_raw), (1, fb_raw)):
        f = f_raw.astype(jnp.float32).reshape(Bn, L, B_HEADS, B_DK)
        if l == 0:
            logf = jax.nn.log_sigmoid(f)
            k = jax.nn.sigmoid(-f)
        else:
            lbd = lb[d, l].reshape(B_HEADS, B_DK)
            forget = lbd + (1.0 - lbd) * jax.nn.sigmoid(f)
            logf = jnp.log(forget)
            k = 1.0 - forget
        if d == 0:
            o_sum = o_sum + _hgrn2_scan(q, k, val, logf)
        else:
            flip = lambda t: jnp.flip(t, axis=1)
            o_sum = o_sum + flip(_hgrn2_scan(flip(q), flip(k), flip(val), flip(logf)))
    g = g_raw.reshape(Bn, L, B_HEADS, B_DV)
    b_out = (_rms(o_sum, p['onorm_g'][l]) * jax.nn.silu(g.astype(jnp.float32))).astype(h.dtype)
    b_out = b_out.reshape(Bn, L, B_WIDTH)

    return jnp.concatenate([a_out, b_out], axis=-1) @ p['w_out'][l]


def _mem_xattn(h, mem, l, p):
    Bn, L, _ = h.shape
    m = _rms(mem, p['mem_norm_g'][l])
    q = (h @ p['w_xq'][l]).reshape(Bn, L, X_HEADS, X_HEAD_DIM)
    kv = m @ p['w_xkv'][l]
    k, v = jnp.split(kv, 2, axis=-1)
    k = k.reshape(Bn, MEM_TOKENS, X_HEADS, X_HEAD_DIM)
    v = v.reshape(Bn, MEM_TOKENS, X_HEADS, X_HEAD_DIM)
    s = jnp.einsum('blhd,bmhd->bhlm', q, k).astype(jnp.float32) * (X_HEAD_DIM ** -0.5)
    a = jax.nn.softmax(s, axis=-1).astype(v.dtype)
    o = jnp.einsum('bhlm,bmhd->blhd', a, v).reshape(Bn, L, D_MODEL)
    return o @ p['w_xo'][l]


def _ffn(h, l, p):
    gate, up = jnp.split(h @ p['w_gu'][l], 2, axis=-1)
    return (jax.nn.silu(gate) * up) @ p['w_down'][l]


def _trunk(x, mem, p):
    probs = jax.nn.softmax(p['lb_raw'].astype(jnp.float32), axis=1)
    lb = jnp.cumsum(probs, axis=1) - probs[:, :1]
    for l in range(DEPTH):
        x = x + _rms(_token_mix(_rms(x, p['pre_mix_g'][l]), l, lb, p), p['post_mix_g'][l])
        x = x + _rms(_mem_xattn(_rms(x, p['pre_x_g'][l]), mem, l, p), p['post_x_g'][l])
        x = x + _rms(_ffn(_rms(x, p['pre_ffn_g'][l]), l, p), p['post_ffn_g'][l])
    return x


def setup_inputs(seed: int = 0) -> dict:
    key = jax.random.key(seed)
    ks = jax.random.split(key, 32)
    f32 = jnp.float32

    def nrm(k, shape, scale):
        return jax.random.normal(k, shape, f32) * scale

    def gain(k, shape):
        return 1.0 + 0.02 * jax.random.normal(k, shape, f32)

    return {
        'x_prompt': nrm(ks[0], (BATCH, SEQ, D_MODEL), 1.0),
        'x_sample': nrm(ks[1], (DEC_BATCH, DEC_SEQ, D_MODEL), 1.0),
        'mem_prompt': nrm(ks[2], (BATCH, MEM_TOKENS, D_MODEL), 1.0),
        'mem_sample': nrm(ks[3], (DEC_BATCH, MEM_TOKENS, D_MODEL), 1.0),
        'pre_mix_g': gain(ks[4], (DEPTH, D_MODEL)),
        'w_in': nrm(ks[5], (DEPTH, D_MODEL, IN_COLS), D_MODEL ** -0.5),
        'v_ln_g': gain(ks[6], (DEPTH, A_WIDTH)),
        'v_ln_b': nrm(ks[7], (DEPTH, A_WIDTH), 0.02),
        'w_spatial': nrm(ks[8], (DEPTH, A_GROUPS, A_CHUNK, A_CHUNK), A_CHUNK ** -0.5),
        'b_spatial': gain(ks[9], (DEPTH, A_GROUPS, A_CHUNK)),
        'lb_raw': nrm(ks[10], (N_DIR, DEPTH, B_WIDTH), 1.0),
        'onorm_g': gain(ks[11], (DEPTH, B_DV)),
        'w_out': nrm(ks[12], (DEPTH, MIX_WIDTH, D_MODEL), MIX_WIDTH ** -0.5),
        'post_mix_g': gain(ks[13], (DEPTH, D_MODEL)),
        'pre_x_g': gain(ks[14], (DEPTH, D_MODEL)),
        'mem_norm_g': gain(ks[15], (DEPTH, D_MODEL)),
        'w_xq': nrm(ks[16], (DEPTH, D_MODEL, D_MODEL), D_MODEL ** -0.5),
        'w_xkv': nrm(ks[17], (DEPTH, D_MODEL, 2 * D_MODEL), D_MODEL ** -0.5),
        'w_xo': nrm(ks[18], (DEPTH, D_MODEL, D_MODEL), D_MODEL ** -0.5),
        'post_x_g': gain(ks[19], (DEPTH, D_MODEL)),
        'pre_ffn_g': gain(ks[20], (DEPTH, D_MODEL)),
        'w_gu': nrm(ks[21], (DEPTH, D_MODEL, 2 * D_FF), D_MODEL ** -0.5),
        'w_down': nrm(ks[22], (DEPTH, D_FF, D_MODEL), D_FF ** -0.5),
        'post_ffn_g': gain(ks[23], (DEPTH, D_MODEL)),
    }


def reference(x_prompt, x_sample, mem_prompt, mem_sample, pre_mix_g, w_in, v_ln_g, v_ln_b,
              w_spatial, b_spatial, lb_raw, onorm_g, w_out, post_mix_g, pre_x_g, mem_norm_g,
              w_xq, w_xkv, w_xo, post_x_g, pre_ffn_g, w_gu, w_down, post_ffn_g):
    p = dict(pre_mix_g=pre_mix_g, w_in=w_in, v_ln_g=v_ln_g, v_ln_b=v_ln_b,
             w_spatial=w_spatial, b_spatial=b_spatial, lb_raw=lb_raw, onorm_g=onorm_g,
             w_out=w_out, post_mix_g=post_mix_g, pre_x_g=pre_x_g, mem_norm_g=mem_norm_g,
             w_xq=w_xq, w_xkv=w_xkv, w_xo=w_xo, post_x_g=post_x_g, pre_ffn_g=pre_ffn_g,
             w_gu=w_gu, w_down=w_down, post_ffn_g=post_ffn_g)
    y_prompt = _trunk(x_prompt, mem_prompt, p)
    y_sample = _trunk(x_sample, mem_sample, p)
    return (y_prompt, y_sample)
```

```python
import functools

import jax
import jax.numpy as jnp
from jax import lax
from jax.experimental import pallas as pl
from jax.experimental.pallas import tpu as pltpu

D_MODEL = 1024
A_WIDTH = 512
A_GROUPS = 4
A_GROUP_DIM = 128
A_CHUNK = 128
B_WIDTH = 512
B_HEADS = 4
B_DK = 128
MEM_TOKENS = 256
X_HEADS = 4
X_HEAD_DIM = 256
D_FF = 2816
EPS = 1e-6

SEQ_TILE = 512
SCAN_CHUNK = 64
SUB_BLOCK = 16
FF_CHUNK = 1408
VMEM_LIMIT_BYTES = 56 * 1024 * 1024

_BF = jnp.bfloat16
_F32 = jnp.float32


def _dot(a, b):
    return jnp.dot(a.astype(_BF), b.astype(_BF), preferred_element_type=_F32)


def _dot_tb(a, b):
    return lax.dot_general(a.astype(_BF), b.astype(_BF), (((1,), (1,)), ((), ())),
                           preferred_element_type=_F32)


def _dot_ta(a, b):
    return lax.dot_general(a.astype(_BF), b.astype(_BF), (((0,), (0,)), ((), ())),
                           preferred_element_type=_F32)


def _rms(x, g):
    return x * lax.rsqrt(jnp.mean(x * x, axis=-1, keepdims=True) + EPS) * g


def _gelu(x):
    return 0.5 * x * (1.0 + lax.erf(x * (2.0 ** -0.5)))


def _silu(x):
    return x * jax.nn.sigmoid(x)


def _log_sigmoid(x):
    return jnp.minimum(x, 0.0) - jnp.log1p(jnp.exp(-jnp.abs(x)))


def _lower_bound(lbraw_ref, layer):
    depth = lbraw_ref.shape[0]
    rows = [lbraw_ref[j:j + 1, :] for j in range(depth)]
    m = functools.reduce(jnp.maximum, rows)
    e = [jnp.exp(r - m) for r in rows]
    denom = functools.reduce(lambda a, b: a + b, e)
    return functools.reduce(lambda a, b: a + b, e[1:layer + 1]) / denom


def _hgrn_inputs(h, wq, wf, wi, lbraw_ref, layer, q_s, k_s, v_s, g_s):
    q_s[...] = _silu(_dot(h, wq)) * (B_DK ** -0.5)
    f = _dot(h, wf)
    if layer == 0:
        g_s[...] = _log_sigmoid(f)
        k_s[...] = jax.nn.sigmoid(-f)
    else:
        lb = _lower_bound(lbraw_ref, layer)
        forget = lb + (1.0 - lb) * jax.nn.sigmoid(f)
        g_s[...] = jnp.log(forget)
        k_s[...] = 1.0 - forget
    v_s[...] = _dot(h, wi)


def _split3(x):
    hi = x.astype(_BF)
    r1 = x - hi.astype(_F32)
    mid = r1.astype(_BF)
    lo = (r1 - mid.astype(_F32)).astype(_BF)
    return hi, mid, lo


def _hgrn_scan(q_s, k_s, v_s, g_s, b_s, kc_s, o_ref, st_s, fwd):
    tile = q_s.shape[0]
    n_chunks = tile // SCAN_CHUNK
    n_sub = SCAN_CHUNK // SUB_BLOCK
    C = SCAN_CHUNK

    row = lax.broadcasted_iota(jnp.int32, (C, C), 0)
    col = lax.broadcasted_iota(jnp.int32, (C, C), 1)
    tri = jnp.where(col <= row if fwd else col >= row, 1.0, 0.0).astype(_BF)
    sub_row = lax.broadcasted_iota(jnp.int32, (SUB_BLOCK, C), 0)
    sub_col = lax.broadcasted_iota(jnp.int32, (SUB_BLOCK, C), 1)

    def chunk_body(c, carry):
        idx = c if fwd else n_chunks - 1 - c
        base = pl.multiple_of(idx * C, C)
        rows = pl.ds(base, C)
        g_hi, g_mid, g_lo = _split3(g_s[rows, :])
        b_s[...] = (jnp.dot(tri, g_hi, preferred_element_type=_F32)
                    + jnp.dot(tri, g_mid, preferred_element_type=_F32)
                    + jnp.dot(tri, g_lo, preferred_element_type=_F32))
        kc_s[...] = k_s[rows, :]
        for hd in range(B_HEADS):
            lanes = slice(hd * B_DK, (hd + 1) * B_DK)
            q = q_s[rows, lanes]
            k = kc_s[:, lanes]
            v = v_s[rows, lanes]
            b = b_s[:, lanes]
            st = st_s[hd]
            b_tot = b[C - 1:C] if fwd else b[0:1]
            o = _dot_tb(q * jnp.exp(b), st)
            kd = k * jnp.exp(b_tot - b)
            st_s[hd] = st * jnp.exp(b_tot) + _dot_ta(v, kd)

            a_rows = []
            for i in range(n_sub):
                lo, hi = i * SUB_BLOCK, (i + 1) * SUB_BLOCK
                qi, bi = q[lo:hi], b[lo:hi]
                if fwd and i > 0:
                    r = b[lo - 1:lo]
                    ke = k[:lo] * jnp.exp(r - b[:lo])
                    ke = jnp.concatenate([ke, jnp.zeros((C - lo, B_DK), _F32)], axis=0)
                    a_i = _dot_tb(qi * jnp.exp(bi - r), ke)
                elif (not fwd) and i < n_sub - 1:
                    r = b[hi:hi + 1]
                    ke = k[hi:] * jnp.exp(r - b[hi:])
                    ke = jnp.concatenate([jnp.zeros((hi, B_DK), _F32), ke], axis=0)
                    a_i = _dot_tb(qi * jnp.exp(bi - r), ke)
                else:
                    a_i = jnp.zeros((SUB_BLOCK, C), _F32)
                for s in range(SUB_BLOCK):
                    j = lo + s
                    k_row = kc_s[j:j + 1, lanes]
                    b_row = b_s[j:j + 1, lanes]
                    p = qi * k_row * jnp.exp(jnp.minimum(bi - b_row, 0.0))
                    a_i = jnp.where(sub_col == j, jnp.sum(p, axis=-1, keepdims=True), a_i)
                local = sub_col - lo
                if fwd:
                    keep = (sub_col < lo) | ((sub_col < hi) & (sub_row >= local))
                else:
                    keep = (sub_col >= hi) | ((sub_col >= lo) & (sub_row <= local))
                a_rows.append(jnp.where(keep, a_i, 0.0))
            a = jnp.concatenate(a_rows, axis=0)
            o_ref[rows, lanes] = o + _dot(a, v)
        return carry

    lax.fori_loop(0, n_chunks, chunk_body, 0)


def _bwd_kernel(layer, x_ref, gpre_ref, w_ref, lbraw_ref, o_ref, q_s, k_s, v_s, g_s, b_s, kc_s,
                st_s):
    @pl.when(pl.program_id(1) == 0)
    def _():
        st_s[...] = jnp.zeros_like(st_s)

    h = _rms(x_ref[0], gpre_ref[...]).astype(_BF)
    _hgrn_inputs(h, w_ref[:, 0:512], w_ref[:, 512:1024], w_ref[:, 1024:1536], lbraw_ref, layer,
                 q_s, k_s, v_s, g_s)
    _hgrn_scan(q_s, k_s, v_s, g_s, b_s, kc_s, o_ref.at[0], st_s, fwd=False)


def _mix_kernel(layer, x_ref, obwd_ref, gpre_ref, w_ref, vlng_ref, vlnb_ref, wsp_ref, bsp_ref,
                lbraw_ref, onorm_ref, wout_ref, gpost_ref, out_ref,
                q_s, k_s, v_s, g_s, b_s, kc_s, o_s, st_s, cat_s):
    @pl.when(pl.program_id(1) == 0)
    def _():
        st_s[...] = jnp.zeros_like(st_s)

    tile = x_ref.shape[1]
    x = x_ref[0]
    h = _rms(x, gpre_ref[...]).astype(_BF)

    u = _gelu(_dot(h, w_ref[:, 0:512]))
    v = _gelu(_dot(h, w_ref[:, 512:1024]))
    mu = jnp.mean(v, axis=-1, keepdims=True)
    vc = v - mu
    v = vc * lax.rsqrt(jnp.mean(vc * vc, axis=-1, keepdims=True) + EPS) * vlng_ref[...] + vlnb_ref[...]
    v = v.astype(_BF)
    for n in range(tile // A_CHUNK):
        rows = slice(n * A_CHUNK, (n + 1) * A_CHUNK)
        for gi in range(A_GROUPS):
            lanes = slice(gi * A_GROUP_DIM, (gi + 1) * A_GROUP_DIM)
            sv = jnp.dot(wsp_ref[gi], v[rows, lanes], preferred_element_type=_F32) + bsp_ref[gi]
            cat_s[rows, lanes] = (u[rows, lanes] * sv).astype(_BF)

    _hgrn_inputs(h, w_ref[:, 1024:1536], w_ref[:, 1536:2048], w_ref[:, 2048:2560], lbraw_ref, layer,
                 q_s, k_s, v_s, g_s)
    _hgrn_scan(q_s, k_s, v_s, g_s, b_s, kc_s, o_s, st_s, fwd=True)
    gate = _silu(_dot(h, w_ref[:, 2560:3072]))
    o_sum = o_s[...] + obwd_ref[0]
    for hd in range(B_HEADS):
        lanes = slice(hd * B_DK, (hd + 1) * B_DK)
        cat_s[:, A_WIDTH + hd * B_DK:A_WIDTH + (hd + 1) * B_DK] = (
            _rms(o_sum[:, lanes], onorm_ref[...]) * gate[:, lanes]).astype(_BF)

    y = jnp.dot(cat_s[...], wout_ref[...], preferred_element_type=_F32)
    out_ref[0] = x + _rms(y, gpost_ref[...])


def _kv_kernel(mem_ref, g_ref, w_ref, k_ref, v_ref):
    m = _rms(mem_ref[0], g_ref[...])
    kv = _dot(m, w_ref[...])
    k_ref[0] = kv[:, :D_MODEL].astype(_BF)
    v_ref[0] = kv[:, D_MODEL:].astype(_BF)


def _xattn_kernel(x_ref, k_ref, v_ref, gpre_ref, wq_ref, wo_ref, gpost_ref, out_ref, cat_s):
    x = x_ref[0]
    h = _rms(x, gpre_ref[...])
    q = _dot(h, wq_ref[...])
    for hd in range(X_HEADS):
        lanes = slice(hd * X_HEAD_DIM, (hd + 1) * X_HEAD_DIM)
        s = _dot_tb(q[:, lanes], k_ref[0, :, lanes]) * (X_HEAD_DIM ** -0.5)
        e = jnp.exp(s - jnp.max(s, axis=-1, keepdims=True))
        p = e / jnp.sum(e, axis=-1, keepdims=True)
        cat_s[:, lanes] = _dot(p, v_ref[0, :, lanes]).astype(_BF)
    y = jnp.dot(cat_s[...], wo_ref[...], preferred_element_type=_F32)
    out_ref[0] = x + _rms(y, gpost_ref[...])


def _ffn_kernel(x_ref, gpre_ref, wgu_ref, wdown_ref, gpost_ref, out_ref):
    x = x_ref[0]
    h = _rms(x, gpre_ref[...]).astype(_BF)
    y = jnp.zeros((x.shape[0], D_MODEL), _F32)
    for c in range(D_FF // FF_CHUNK):
        cols = slice(c * FF_CHUNK, (c + 1) * FF_CHUNK)
        gate = jnp.dot(h, wgu_ref[:, cols], preferred_element_type=_F32)
        up = jnp.dot(h, wgu_ref[:, D_FF + c * FF_CHUNK:D_FF + (c + 1) * FF_CHUNK],
                     preferred_element_type=_F32)
        y = y + _dot(_silu(gate) * up, wdown_ref[cols, :])
    out_ref[0] = x + _rms(y, gpost_ref[...])


def _params():
    return pltpu.CompilerParams(dimension_semantics=("arbitrary", "arbitrary"),
                                vmem_limit_bytes=VMEM_LIMIT_BYTES)


def _const_spec(shape):
    return pl.BlockSpec(shape, lambda b, t: (0,) * len(shape), pipeline_mode=pl.Buffered(1))


def _tile_spec(width, rev_tiles=None):
    if rev_tiles is None:
        return pl.BlockSpec((1, SEQ_TILE, width), lambda b, t: (b, t, 0))
    return pl.BlockSpec((1, SEQ_TILE, width), lambda b, t: (b, rev_tiles - 1 - t, 0))


def _scan_scratch():
    return [pltpu.VMEM((SEQ_TILE, B_WIDTH), _F32) for _ in range(4)] + [
        pltpu.VMEM((SCAN_CHUNK, B_WIDTH), _F32) for _ in range(2)]


def _state_scratch():
    return pltpu.VMEM((B_HEADS, B_DK, B_DK), _F32)


def _bwd_call(x, layer, gpre, w_bwd, lbraw):
    bn, ln, _ = x.shape
    nt = ln // SEQ_TILE
    return pl.pallas_call(
        functools.partial(_bwd_kernel, layer),
        grid=(bn, nt),
        in_specs=[_tile_spec(D_MODEL, nt), _const_spec(gpre.shape), _const_spec(w_bwd.shape),
                  _const_spec(lbraw.shape)],
        out_specs=_tile_spec(B_WIDTH, nt),
        out_shape=jax.ShapeDtypeStruct((bn, ln, B_WIDTH), _F32),
        scratch_shapes=_scan_scratch() + [_state_scratch()],
        compiler_params=_params(),
        name=f"hgrn_bwd_l{layer}",
    )(x, gpre, w_bwd, lbraw)


def _mix_call(x, obwd, layer, gpre, w_mix, vlng, vlnb, wsp, bsp, lbraw, onorm, wout, gpost):
    bn, ln, _ = x.shape
    nt = ln // SEQ_TILE
    consts = (gpre, w_mix, vlng, vlnb, wsp, bsp, lbraw, onorm, wout, gpost)
    return pl.pallas_call(
        functools.partial(_mix_kernel, layer),
        grid=(bn, nt),
        in_specs=[_tile_spec(D_MODEL), _tile_spec(B_WIDTH)] + [_const_spec(c.shape) for c in consts],
        out_specs=_tile_spec(D_MODEL),
        out_shape=jax.ShapeDtypeStruct(x.shape, _F32),
        scratch_shapes=_scan_scratch() + [pltpu.VMEM((SEQ_TILE, B_WIDTH), _F32), _state_scratch(),
                                          pltpu.VMEM((SEQ_TILE, D_MODEL), _BF)],
        compiler_params=_params(),
        name=f"token_mix_l{layer}",
    )(x, obwd, *consts)


def _kv_call(mem, layer, g, w):
    bn = mem.shape[0]
    spec = pl.BlockSpec((1, MEM_TOKENS, D_MODEL), lambda b, t: (b, 0, 0))
    return pl.pallas_call(
        _kv_kernel,
        grid=(bn, 1),
        in_specs=[spec, _const_spec(g.shape), _const_spec(w.shape)],
        out_specs=[spec, spec],
        out_shape=[jax.ShapeDtypeStruct(mem.shape, _BF)] * 2,
        compiler_params=_params(),
        name=f"mem_kv_l{layer}",
    )(mem, g, w)


def _xattn_call(x, k, v, layer, gpre, wq, wo, gpost):
    bn, ln, _ = x.shape
    nt = ln // SEQ_TILE
    mem_spec = pl.BlockSpec((1, MEM_TOKENS, D_MODEL), lambda b, t: (b, 0, 0))
    consts = (gpre, wq, wo, gpost)
    return pl.pallas_call(
        _xattn_kernel,
        grid=(bn, nt),
        in_specs=[_tile_spec(D_MODEL), mem_spec, mem_spec] + [_const_spec(c.shape) for c in consts],
        out_specs=_tile_spec(D_MODEL),
        out_shape=jax.ShapeDtypeStruct(x.shape, _F32),
        scratch_shapes=[pltpu.VMEM((SEQ_TILE, D_MODEL), _BF)],
        compiler_params=_params(),
        name=f"mem_xattn_l{layer}",
    )(x, k, v, *consts)


def _ffn_call(x, layer, gpre, wgu, wdown, gpost):
    bn, ln, _ = x.shape
    nt = ln // SEQ_TILE
    consts = (gpre, wgu, wdown, gpost)
    return pl.pallas_call(
        _ffn_kernel,
        grid=(bn, nt),
        in_specs=[_tile_spec(D_MODEL)] + [_const_spec(c.shape) for c in consts],
        out_specs=_tile_spec(D_MODEL),
        out_shape=jax.ShapeDtypeStruct(x.shape, _F32),
        compiler_params=_params(),
        name=f"swiglu_ffn_l{layer}",
    )(x, *consts)


def _layer_weights(l, p):
    w_in = p['w_in'][l].astype(_BF)
    cols = lambda a, b: w_in[:, a:b]
    row = lambda a: a[l][None, :]
    return dict(
        w_mix=jnp.concatenate([cols(0, 2048), cols(2560, 3584)], axis=1),
        w_bwd=jnp.concatenate([cols(1024, 1536), cols(2048, 3072)], axis=1),
        pre_mix_g=row(p['pre_mix_g']), v_ln_g=row(p['v_ln_g']), v_ln_b=row(p['v_ln_b']),
        w_spatial=p['w_spatial'][l].astype(_BF),
        b_spatial=jnp.broadcast_to(p['b_spatial'][l][:, :, None], (A_GROUPS, A_CHUNK, A_GROUP_DIM)),
        lb_fwd=p['lb_raw'][0], lb_bwd=p['lb_raw'][1],
        onorm_g=row(p['onorm_g']), w_out=p['w_out'][l].astype(_BF), post_mix_g=row(p['post_mix_g']),
        pre_x_g=row(p['pre_x_g']), mem_norm_g=row(p['mem_norm_g']),
        w_xq=p['w_xq'][l].astype(_BF), w_xkv=p['w_xkv'][l].astype(_BF), w_xo=p['w_xo'][l].astype(_BF),
        post_x_g=row(p['post_x_g']), pre_ffn_g=row(p['pre_ffn_g']),
        w_gu=p['w_gu'][l].astype(_BF), w_down=p['w_down'][l].astype(_BF), post_ffn_g=row(p['post_ffn_g']),
    )


def _trunk(x, mem, layers):
    assert x.shape[1] % SEQ_TILE == 0 and x.shape[2] == D_MODEL
    for l, w in enumerate(layers):
        obwd = _bwd_call(x, l, w['pre_mix_g'], w['w_bwd'], w['lb_bwd'])
        x = _mix_call(x, obwd, l, w['pre_mix_g'], w['w_mix'], w['v_ln_g'], w['v_ln_b'], w['w_spatial'],
                      w['b_spatial'], w['lb_fwd'], w['onorm_g'], w['w_out'], w['post_mix_g'])
        k, v = _kv_call(mem, l, w['mem_norm_g'], w['w_xkv'])
        x = _xattn_call(x, k, v, l, w['pre_x_g'], w['w_xq'], w['w_xo'], w['post_x_g'])
        x = _ffn_call(x, l, w['pre_ffn_g'], w['w_gu'], w['w_down'], w['post_ffn_g'])
    return x


def kernel(x_prompt, x_sample, mem_prompt, mem_sample, pre_mix_g, w_in, v_ln_g, v_ln_b, w_spatial, b_spatial, lb_raw, onorm_g, w_out, post_mix_g, pre_x_g, mem_norm_g, w_xq, w_xkv, w_xo, post_x_g, pre_ffn_g, w_gu, w_down, post_ffn_g):
    p = dict(pre_mix_g=pre_mix_g, w_in=w_in, v_ln_g=v_ln_g, v_ln_b=v_ln_b,
             w_spatial=w_spatial, b_spatial=b_spatial, lb_raw=lb_raw, onorm_g=onorm_g,
             w_out=w_out, post_mix_g=post_mix_g, pre_x_g=pre_x_g, mem_norm_g=mem_norm_g,
             w_xq=w_xq, w_xkv=w_xkv, w_xo=w_xo, post_x_g=post_x_g, pre_ffn_g=pre_ffn_g,
             w_gu=w_gu, w_down=w_down, post_ffn_g=post_ffn_g)
    layers = [_layer_weights(l, p) for l in range(w_in.shape[0])]
    return (_trunk(x_prompt, mem_prompt, layers), _trunk(x_sample, mem_sample, layers))
```

```python
import functools

import jax
import jax.numpy as jnp
from jax import lax
from jax.experimental import pallas as pl
from jax.experimental.pallas import tpu as pltpu

D_MODEL = 1024
A_WIDTH = 512
A_GROUPS = 4
A_GROUP_DIM = 128
A_CHUNK = 128
B_WIDTH = 512
B_HEADS = 4
B_DK = 128
MEM_TOKENS = 256
X_HEADS = 4
X_HEAD_DIM = 256
D_FF = 2816
EPS = 1e-6

SEQ_TILE = 512
SCAN_CHUNK = 64
SUB_BLOCK = 16
FF_CHUNK = 1408
FACTORED_DECAY_LIMIT = 55.0
VMEM_LIMIT_BYTES = 56 * 1024 * 1024

_BF = jnp.bfloat16
_F32 = jnp.float32


def _dot(a, b):
    return jnp.dot(a.astype(_BF), b.astype(_BF), preferred_element_type=_F32)


def _dot_tb(a, b):
    return lax.dot_general(a.astype(_BF), b.astype(_BF), (((1,), (1,)), ((), ())),
                           preferred_element_type=_F32)


def _dot_ta(a, b):
    return lax.dot_general(a.astype(_BF), b.astype(_BF), (((0,), (0,)), ((), ())),
                           preferred_element_type=_F32)


def _rms(x, g):
    return x * lax.rsqrt(jnp.mean(x * x, axis=-1, keepdims=True) + EPS) * g


def _gelu(x):
    return 0.5 * x * (1.0 + lax.erf(x * (2.0 ** -0.5)))


def _silu(x):
    return x * jax.nn.sigmoid(x)


def _log_sigmoid(x):
    return jnp.minimum(x, 0.0) - jnp.log1p(jnp.exp(-jnp.abs(x)))


def _lower_bound(lbraw_ref, layer):
    depth = lbraw_ref.shape[0]
    rows = [lbraw_ref[j:j + 1, :] for j in range(depth)]
    m = functools.reduce(jnp.maximum, rows)
    e = [jnp.exp(r - m) for r in rows]
    denom = functools.reduce(lambda a, b: a + b, e)
    return functools.reduce(lambda a, b: a + b, e[1:layer + 1]) / denom


def _hgrn_inputs(h, wq, wf, wi, lbraw_ref, layer, scan_scratch):
    q_s, k_s, v_s, g_s = scan_scratch[:4]
    q_s[...] = _silu(_dot(h, wq)) * (B_DK ** -0.5)
    f = _dot(h, wf)
    if layer == 0:
        g_s[...] = _log_sigmoid(f)
        k_s[...] = jax.nn.sigmoid(-f)
    else:
        lb = _lower_bound(lbraw_ref, layer)
        forget = lb + (1.0 - lb) * jax.nn.sigmoid(f)
        g_s[...] = jnp.log(forget)
        k_s[...] = 1.0 - forget
    v_s[...] = _dot(h, wi).astype(_BF)


def _split3(x):
    hi = x.astype(_BF)
    r1 = x - hi.astype(_F32)
    mid = r1.astype(_BF)
    lo = (r1 - mid.astype(_F32)).astype(_BF)
    return hi, mid, lo


def _hgrn_scan(scan_scratch, o_ref, st_s, fwd):
    q_s, k_s, v_s, g_s, b_s, kc_s, x_s, y_s, a_s, u_s = scan_scratch
    tile = q_s.shape[0]
    n_chunks = tile // SCAN_CHUNK
    n_sub = SCAN_CHUNK // SUB_BLOCK
    C = SCAN_CHUNK

    row = lax.broadcasted_iota(jnp.int32, (C, C), 0)
    col = lax.broadcasted_iota(jnp.int32, (C, C), 1)
    tri = jnp.where(col <= row if fwd else col >= row, 1.0, 0.0).astype(_BF)
    sub_row = lax.broadcasted_iota(jnp.int32, (SUB_BLOCK, C), 0)
    sub_col = lax.broadcasted_iota(jnp.int32, (SUB_BLOCK, C), 1)

    def cumulative(g):
        g_hi, g_mid, g_lo = _split3(g)
        return (jnp.dot(tri, g_hi, preferred_element_type=_F32)
                + jnp.dot(tri, g_mid, preferred_element_type=_F32)
                + jnp.dot(tri, g_lo, preferred_element_type=_F32))

    def chunk_body(c, carry):
        idx = c if fwd else n_chunks - 1 - c
        base = pl.multiple_of(idx * C, C)
        rows = pl.ds(base, C)
        b_s[0:C, :] = cumulative(g_s[rows, :])
        kc_s[...] = k_s[rows, :]
        for hd in range(B_HEADS):
            lanes = slice(hd * B_DK, (hd + 1) * B_DK)
            q = q_s[rows, lanes]
            k = kc_s[:, lanes]
            v = v_s[rows, lanes]
            b = b_s[0:C, lanes]
            st = st_s[hd]
            b_tot = b[C - 1:C] if fwd else b[0:1]
            o = _dot_tb(q * jnp.exp(b), st)
            kd = k * jnp.exp(b_tot - b)
            st_s[hd] = st * jnp.exp(b_tot) + _dot_ta(v, kd)

            a_rows = []
            for i in range(n_sub):
                lo, hi = i * SUB_BLOCK, (i + 1) * SUB_BLOCK
                qi, bi = q[lo:hi], b[lo:hi]
                if fwd and i > 0:
                    r = b[lo - 1:lo]
                    ke = k[:lo] * jnp.exp(r - b[:lo])
                    ke = jnp.concatenate([ke, jnp.zeros((C - lo, B_DK), _F32)], axis=0)
                    a_i = _dot_tb(qi * jnp.exp(bi - r), ke)
                elif (not fwd) and i < n_sub - 1:
                    r = b[hi:hi + 1]
                    ke = k[hi:] * jnp.exp(r - b[hi:])
                    ke = jnp.concatenate([jnp.zeros((hi, B_DK), _F32), ke], axis=0)
                    a_i = _dot_tb(qi * jnp.exp(bi - r), ke)
                else:
                    a_i = jnp.zeros((SUB_BLOCK, C), _F32)
                for s in range(SUB_BLOCK):
                    j = lo + s
                    k_row = kc_s[j:j + 1, lanes]
                    b_row = b_s[j:j + 1, lanes]
                    p = qi * k_row * jnp.exp(jnp.minimum(bi - b_row, 0.0))
                    a_i = jnp.where(sub_col == j, jnp.sum(p, axis=-1, keepdims=True), a_i)
                local = sub_col - lo
                if fwd:
                    keep = (sub_col < lo) | ((sub_col < hi) & (sub_row >= local))
                else:
                    keep = (sub_col >= hi) | ((sub_col >= lo) & (sub_row <= local))
                a_rows.append(jnp.where(keep, a_i, 0.0))
            a = jnp.concatenate(a_rows, axis=0)
            o_ref[rows, lanes] = o + _dot(a, v)
        return carry

    def factored_tile():
        causal = (col <= row) if fwd else (col >= row)
        mid = C // 2 - 1 if fwd else C // 2
        order = list(range(n_chunks)) if fwd else list(reversed(range(n_chunks)))
        chunk_rows = lambda c: slice(c * C, (c + 1) * C)
        head_lanes = lambda hd: slice(hd * B_DK, (hd + 1) * B_DK)
        for c in order:
            b_s[chunk_rows(c), :] = cumulative(g_s[chunk_rows(c), :])
        for c in order:
            rows = chunk_rows(c)
            b = b_s[rows, :]
            e = jnp.exp(b - b[mid:mid + 1])
            x_s[rows, :] = (q_s[rows, :] * e).astype(_BF)
            y_s[rows, :] = (k_s[rows, :] / e).astype(_BF)
        for c in order:
            rows = chunk_rows(c)
            for hd in range(B_HEADS):
                yh = y_s[rows, head_lanes(hd)]
                a_s[hd, rows, :] = jnp.where(causal, _dot_tb(x_s[rows, head_lanes(hd)], yh),
                                             0.0).astype(_BF)
                u_s[c, hd] = _dot_ta(v_s[rows, head_lanes(hd)], yh)
        for hd in range(B_HEADS):
            lanes = head_lanes(hd)
            st = st_s[hd]
            for c in order:
                rho = b_s[c * C + mid:c * C + mid + 1, lanes]
                b_tot = b_s[(c + 1) * C - 1:(c + 1) * C, lanes] if fwd else b_s[c * C:c * C + 1, lanes]
                u = u_s[c, hd]
                u_s[c, hd] = st * jnp.exp(rho)
                st = st * jnp.exp(b_tot) + u * jnp.exp(b_tot - rho)
            st_s[hd] = st
        for c in order:
            rows = chunk_rows(c)
            for hd in range(B_HEADS):
                lanes = head_lanes(hd)
                o_ref[rows, lanes] = (_dot_tb(x_s[rows, lanes], u_s[c, hd])
                                      + _dot(a_s[hd, rows, :], v_s[rows, lanes]))

    half = C // 2
    n_halves = tile // half
    blk_row = lax.broadcasted_iota(jnp.int32, (n_halves, tile), 0)
    blk_col = lax.broadcasted_iota(jnp.int32, (n_halves, tile), 1)
    member = jnp.where((blk_col >= blk_row * half) & (blk_col < (blk_row + 1) * half),
                       1.0, 0.0).astype(_BF)
    half_decay = -jnp.dot(member, g_s[...].astype(_BF), preferred_element_type=_F32)
    factored_ok = jnp.max(half_decay) < FACTORED_DECAY_LIMIT

    @pl.when(factored_ok)
    def _():
        factored_tile()

    @pl.when(jnp.logical_not(factored_ok))
    def _():
        lax.fori_loop(0, n_chunks, chunk_body, 0)


def _bwd_kernel(layer, x_ref, gpre_ref, w_ref, lbraw_ref, o_ref, *scratch):
    scan_scratch, st_s = scratch[:-1], scratch[-1]
    @pl.when(pl.program_id(1) == 0)
    def _():
        st_s[...] = jnp.zeros_like(st_s)

    h = _rms(x_ref[0], gpre_ref[...]).astype(_BF)
    _hgrn_inputs(h, w_ref[:, 0:512], w_ref[:, 512:1024], w_ref[:, 1024:1536], lbraw_ref, layer,
                 scan_scratch)
    _hgrn_scan(scan_scratch, o_ref.at[0], st_s, fwd=False)


def _mix_kernel(layer, x_ref, obwd_ref, gpre_ref, w_ref, vlng_ref, vlnb_ref, wsp_ref, bsp_ref,
                lbraw_ref, onorm_ref, wout_ref, gpost_ref, out_ref, *scratch):
    scan_scratch, (o_s, st_s, cat_s) = scratch[:-3], scratch[-3:]
    @pl.when(pl.program_id(1) == 0)
    def _():
        st_s[...] = jnp.zeros_like(st_s)

    tile = x_ref.shape[1]
    x = x_ref[0]
    h = _rms(x, gpre_ref[...]).astype(_BF)

    u = _gelu(_dot(h, w_ref[:, 0:512]))
    v = _gelu(_dot(h, w_ref[:, 512:1024]))
    mu = jnp.mean(v, axis=-1, keepdims=True)
    vc = v - mu
    v = vc * lax.rsqrt(jnp.mean(vc * vc, axis=-1, keepdims=True) + EPS) * vlng_ref[...] + vlnb_ref[...]
    v = v.astype(_BF)
    for n in range(tile // A_CHUNK):
        rows = slice(n * A_CHUNK, (n + 1) * A_CHUNK)
        for gi in range(A_GROUPS):
            lanes = slice(gi * A_GROUP_DIM, (gi + 1) * A_GROUP_DIM)
            sv = jnp.dot(wsp_ref[gi], v[rows, lanes], preferred_element_type=_F32) + bsp_ref[gi]
            cat_s[rows, lanes] = (u[rows, lanes] * sv).astype(_BF)

    _hgrn_inputs(h, w_ref[:, 1024:1536], w_ref[:, 1536:2048], w_ref[:, 2048:2560], lbraw_ref, layer,
                 scan_scratch)
    _hgrn_scan(scan_scratch, o_s, st_s, fwd=True)
    gate = _silu(_dot(h, w_ref[:, 2560:3072]))
    o_sum = o_s[...] + obwd_ref[0]
    for hd in range(B_HEADS):
        lanes = slice(hd * B_DK, (hd + 1) * B_DK)
        cat_s[:, A_WIDTH + hd * B_DK:A_WIDTH + (hd + 1) * B_DK] = (
            _rms(o_sum[:, lanes], onorm_ref[...]) * gate[:, lanes]).astype(_BF)

    y = jnp.dot(cat_s[...], wout_ref[...], preferred_element_type=_F32)
    out_ref[0] = x + _rms(y, gpost_ref[...])


def _kv_kernel(mem_ref, g_ref, w_ref, k_ref, v_ref):
    m = _rms(mem_ref[0], g_ref[...])
    kv = _dot(m, w_ref[...])
    k_ref[0] = kv[:, :D_MODEL].astype(_BF)
    v_ref[0] = kv[:, D_MODEL:].astype(_BF)


def _xattn_kernel(x_ref, k_ref, v_ref, gpre_ref, wq_ref, wo_ref, gpost_ref, out_ref, cat_s):
    x = x_ref[0]
    h = _rms(x, gpre_ref[...])
    q = _dot(h, wq_ref[...])
    for hd in range(X_HEADS):
        lanes = slice(hd * X_HEAD_DIM, (hd + 1) * X_HEAD_DIM)
        s = _dot_tb(q[:, lanes], k_ref[0, :, lanes]) * (X_HEAD_DIM ** -0.5)
        e = jnp.exp(s - jnp.max(s, axis=-1, keepdims=True))
        p = e / jnp.sum(e, axis=-1, keepdims=True)
        cat_s[:, lanes] = _dot(p, v_ref[0, :, lanes]).astype(_BF)
    y = jnp.dot(cat_s[...], wo_ref[...], preferred_element_type=_F32)
    out_ref[0] = x + _rms(y, gpost_ref[...])


def _ffn_kernel(x_ref, gpre_ref, wgu_ref, wdown_ref, gpost_ref, out_ref):
    x = x_ref[0]
    h = _rms(x, gpre_ref[...]).astype(_BF)
    y = jnp.zeros((x.shape[0], D_MODEL), _F32)
    for c in range(D_FF // FF_CHUNK):
        cols = slice(c * FF_CHUNK, (c + 1) * FF_CHUNK)
        gate = jnp.dot(h, wgu_ref[:, cols], preferred_element_type=_F32)
        up = jnp.dot(h, wgu_ref[:, D_FF + c * FF_CHUNK:D_FF + (c + 1) * FF_CHUNK],
                     preferred_element_type=_F32)
        y = y + _dot(_silu(gate) * up, wdown_ref[cols, :])
    out_ref[0] = x + _rms(y, gpost_ref[...])


def _params():
    return pltpu.CompilerParams(dimension_semantics=("arbitrary", "arbitrary"),
                                vmem_limit_bytes=VMEM_LIMIT_BYTES)


def _const_spec(shape):
    return pl.BlockSpec(shape, lambda b, t: (0,) * len(shape), pipeline_mode=pl.Buffered(1))


def _tile_spec(width, rev_tiles=None):
    if rev_tiles is None:
        return pl.BlockSpec((1, SEQ_TILE, width), lambda b, t: (b, t, 0))
    return pl.BlockSpec((1, SEQ_TILE, width), lambda b, t: (b, rev_tiles - 1 - t, 0))


def _scan_scratch():
    tile = lambda dt: pltpu.VMEM((SEQ_TILE, B_WIDTH), dt)
    return [tile(_F32), tile(_F32), tile(_BF), tile(_F32), tile(_F32),
            pltpu.VMEM((SCAN_CHUNK, B_WIDTH), _F32), tile(_BF), tile(_BF),
            pltpu.VMEM((B_HEADS, SEQ_TILE, SCAN_CHUNK), _BF),
            pltpu.VMEM((SEQ_TILE // SCAN_CHUNK, B_HEADS, B_DK, B_DK), _F32)]


def _state_scratch():
    return pltpu.VMEM((B_HEADS, B_DK, B_DK), _F32)


def _bwd_call(x, layer, gpre, w_bwd, lbraw):
    bn, ln, _ = x.shape
    nt = ln // SEQ_TILE
    return pl.pallas_call(
        functools.partial(_bwd_kernel, layer),
        grid=(bn, nt),
        in_specs=[_tile_spec(D_MODEL, nt), _const_spec(gpre.shape), _const_spec(w_bwd.shape),
                  _const_spec(lbraw.shape)],
        out_specs=_tile_spec(B_WIDTH, nt),
        out_shape=jax.ShapeDtypeStruct((bn, ln, B_WIDTH), _F32),
        scratch_shapes=_scan_scratch() + [_state_scratch()],
        compiler_params=_params(),
        name=f"hgrn_bwd_l{layer}",
    )(x, gpre, w_bwd, lbraw)


def _mix_call(x, obwd, layer, gpre, w_mix, vlng, vlnb, wsp, bsp, lbraw, onorm, wout, gpost):
    bn, ln, _ = x.shape
    nt = ln // SEQ_TILE
    consts = (gpre, w_mix, vlng, vlnb, wsp, bsp, lbraw, onorm, wout, gpost)
    return pl.pallas_call(
        functools.partial(_mix_kernel, layer),
        grid=(bn, nt),
        in_specs=[_tile_spec(D_MODEL), _tile_spec(B_WIDTH)] + [_const_spec(c.shape) for c in consts],
        out_specs=_tile_spec(D_MODEL),
        out_shape=jax.ShapeDtypeStruct(x.shape, _F32),
        scratch_shapes=_scan_scratch() + [pltpu.VMEM((SEQ_TILE, B_WIDTH), _F32), _state_scratch(),
                                          pltpu.VMEM((SEQ_TILE, D_MODEL), _BF)],
        compiler_params=_params(),
        name=f"token_mix_l{layer}",
    )(x, obwd, *consts)


def _kv_call(mem, layer, g, w):
    bn = mem.shape[0]
    spec = pl.BlockSpec((1, MEM_TOKENS, D_MODEL), lambda b, t: (b, 0, 0))
    return pl.pallas_call(
        _kv_kernel,
        grid=(bn, 1),
        in_specs=[spec, _const_spec(g.shape), _const_spec(w.shape)],
        out_specs=[spec, spec],
        out_shape=[jax.ShapeDtypeStruct(mem.shape, _BF)] * 2,
        compiler_params=_params(),
        name=f"mem_kv_l{layer}",
    )(mem, g, w)


def _xattn_call(x, k, v, layer, gpre, wq, wo, gpost):
    bn, ln, _ = x.shape
    nt = ln // SEQ_TILE
    mem_spec = pl.BlockSpec((1, MEM_TOKENS, D_MODEL), lambda b, t: (b, 0, 0))
    consts = (gpre, wq, wo, gpost)
    return pl.pallas_call(
        _xattn_kernel,
        grid=(bn, nt),
        in_specs=[_tile_spec(D_MODEL), mem_spec, mem_spec] + [_const_spec(c.shape) for c in consts],
        out_specs=_tile_spec(D_MODEL),
        out_shape=jax.ShapeDtypeStruct(x.shape, _F32),
        scratch_shapes=[pltpu.VMEM((SEQ_TILE, D_MODEL), _BF)],
        compiler_params=_params(),
        name=f"mem_xattn_l{layer}",
    )(x, k, v, *consts)


def _ffn_call(x, layer, gpre, wgu, wdown, gpost):
    bn, ln, _ = x.shape
    nt = ln // SEQ_TILE
    consts = (gpre, wgu, wdown, gpost)
    return pl.pallas_call(
        _ffn_kernel,
        grid=(bn, nt),
        in_specs=[_tile_spec(D_MODEL)] + [_const_spec(c.shape) for c in consts],
        out_specs=_tile_spec(D_MODEL),
        out_shape=jax.ShapeDtypeStruct(x.shape, _F32),
        compiler_params=_params(),
        name=f"swiglu_ffn_l{layer}",
    )(x, *consts)


def _layer_weights(l, p):
    w_in = p['w_in'][l].astype(_BF)
    cols = lambda a, b: w_in[:, a:b]
    row = lambda a: a[l][None, :]
    return dict(
        w_mix=jnp.concatenate([cols(0, 2048), cols(2560, 3584)], axis=1),
        w_bwd=jnp.concatenate([cols(1024, 1536), cols(2048, 3072)], axis=1),
        pre_mix_g=row(p['pre_mix_g']), v_ln_g=row(p['v_ln_g']), v_ln_b=row(p['v_ln_b']),
        w_spatial=p['w_spatial'][l].astype(_BF),
        b_spatial=jnp.broadcast_to(p['b_spatial'][l][:, :, None], (A_GROUPS, A_CHUNK, A_GROUP_DIM)),
        lb_fwd=p['lb_raw'][0], lb_bwd=p['lb_raw'][1],
        onorm_g=row(p['onorm_g']), w_out=p['w_out'][l].astype(_BF), post_mix_g=row(p['post_mix_g']),
        pre_x_g=row(p['pre_x_g']), mem_norm_g=row(p['mem_norm_g']),
        w_xq=p['w_xq'][l].astype(_BF), w_xkv=p['w_xkv'][l].astype(_BF), w_xo=p['w_xo'][l].astype(_BF),
        post_x_g=row(p['post_x_g']), pre_ffn_g=row(p['pre_ffn_g']),
        w_gu=p['w_gu'][l].astype(_BF), w_down=p['w_down'][l].astype(_BF), post_ffn_g=row(p['post_ffn_g']),
    )


def _trunk(x, mem, layers):
    assert x.shape[1] % SEQ_TILE == 0 and x.shape[2] == D_MODEL
    for l, w in enumerate(layers):
        obwd = _bwd_call(x, l, w['pre_mix_g'], w['w_bwd'], w['lb_bwd'])
        x = _mix_call(x, obwd, l, w['pre_mix_g'], w['w_mix'], w['v_ln_g'], w['v_ln_b'], w['w_spatial'],
                      w['b_spatial'], w['lb_fwd'], w['onorm_g'], w['w_out'], w['post_mix_g'])
        k, v = _kv_call(mem, l, w['mem_norm_g'], w['w_xkv'])
        x = _xattn_call(x, k, v, l, w['pre_x_g'], w['w_xq'], w['w_xo'], w['post_x_g'])
        x = _ffn_call(x, l, w['pre_ffn_g'], w['w_gu'], w['w_down'], w['post_ffn_g'])
    return x


def kernel(x_prompt, x_sample, mem_prompt, mem_sample, pre_mix_g, w_in, v_ln_g, v_ln_b, w_spatial, b_spatial, lb_raw, onorm_g, w_out, post_mix_g, pre_x_g, mem_norm_g, w_xq, w_xkv, w_xo, post_x_g, pre_ffn_g, w_gu, w_down, post_ffn_g):
    p = dict(pre_mix_g=pre_mix_g, w_in=w_in, v_ln_g=v_ln_g, v_ln_b=v_ln_b,
             w_spatial=w_spatial, b_spatial=b_spatial, lb_raw=lb_raw, onorm_g=onorm_g,
             w_out=w_out, post_mix_g=post_mix_g, pre_x_g=pre_x_g, mem_norm_g=mem_norm_g,
             w_xq=w_xq, w_xkv=w_xkv, w_xo=w_xo, post_x_g=post_x_g, pre_ffn_g=pre_ffn_g,
             w_gu=w_gu, w_down=w_down, post_ffn_g=post_ffn_g)
    layers = [_layer_weights(l, p) for l in range(w_in.shape[0])]
    return (_trunk(x_prompt, mem_prompt, layers), _trunk(x_sample, mem_sample, layers))
```

```python
import functools

import jax
import jax.numpy as jnp
from jax import lax
from jax.experimental import pallas as pl
from jax.experimental.pallas import tpu as pltpu

D_MODEL = 1024
A_WIDTH = 512
A_GROUPS = 4
A_GROUP_DIM = 128
A_CHUNK = 128
B_WIDTH = 512
B_HEADS = 4
B_DK = 128
MEM_TOKENS = 256
X_HEADS = 4
X_HEAD_DIM = 256
D_FF = 2816
EPS = 1e-6

SEQ_TILE = 512
ROW_BLOCK = 128
SCAN_CHUNK = 64
SUB_BLOCK = 16
FF_CHUNK = 1408
FACTORED_DECAY_LIMIT = 55.0
VMEM_LIMIT_BYTES = 56 * 1024 * 1024

_BF = jnp.bfloat16
_F32 = jnp.float32


def _dot(a, b):
    return jnp.dot(a.astype(_BF), b.astype(_BF), preferred_element_type=_F32)


def _dot_tb(a, b):
    return lax.dot_general(a.astype(_BF), b.astype(_BF), (((1,), (1,)), ((), ())),
                           preferred_element_type=_F32)


def _dot_ta(a, b):
    return lax.dot_general(a.astype(_BF), b.astype(_BF), (((0,), (0,)), ((), ())),
                           preferred_element_type=_F32)


def _rms(x, g):
    return x * lax.rsqrt(jnp.mean(x * x, axis=-1, keepdims=True) + EPS) * g


def _gelu(x):
    return 0.5 * x * (1.0 + lax.erf(x * (2.0 ** -0.5)))


def _silu(x):
    return x * jax.nn.sigmoid(x)


def _log_sigmoid(x):
    return jnp.minimum(x, 0.0) - jnp.log1p(jnp.exp(-jnp.abs(x)))


def _lower_bound(lbraw_ref, layer):
    depth = lbraw_ref.shape[0]
    rows = [lbraw_ref[j:j + 1, :] for j in range(depth)]
    m = functools.reduce(jnp.maximum, rows)
    e = [jnp.exp(r - m) for r in rows]
    denom = functools.reduce(lambda a, b: a + b, e)
    return functools.reduce(lambda a, b: a + b, e[1:layer + 1]) / denom


def _skewed(n, *stages):
    vals = {}
    for i in range(n + len(stages) - 1):
        for k, stage in enumerate(stages):
            r = i - k
            if 0 <= r < n:
                vals[(k, r)] = stage(r, vals.pop((k - 1, r), None))


def _row_block(r):
    return slice(r * ROW_BLOCK, (r + 1) * ROW_BLOCK)


def _split3(x):
    hi = x.astype(_BF)
    r1 = x - hi.astype(_F32)
    mid = r1.astype(_BF)
    lo = (r1 - mid.astype(_F32)).astype(_BF)
    return hi, mid, lo


def _cumulative(g, fwd):
    n = g.shape[0]
    row = lax.broadcasted_iota(jnp.int32, (n, n), 0)
    col = lax.broadcasted_iota(jnp.int32, (n, n), 1)
    tri = jnp.where(col <= row if fwd else col >= row, 1.0, 0.0).astype(_BF)
    g_hi, g_mid, g_lo = _split3(g)
    return (jnp.dot(tri, g_hi, preferred_element_type=_F32)
            + jnp.dot(tri, g_mid, preferred_element_type=_F32)
            + jnp.dot(tri, g_lo, preferred_element_type=_F32))


def _hgrn_gates(q_raw, f, v_raw, rows, lb, scan_scratch):
    q_s, k_s, v_s, g_s = scan_scratch[:4]
    q_s[rows, :] = _silu(q_raw) * (B_DK ** -0.5)
    if lb is None:
        g = _log_sigmoid(f)
        k_s[rows, :] = jax.nn.sigmoid(-f)
    else:
        forget = lb + (1.0 - lb) * jax.nn.sigmoid(f)
        g = jnp.log(forget)
        k_s[rows, :] = 1.0 - forget
    g_s[rows, :] = g
    v_s[rows, :] = v_raw.astype(_BF)
    return g


def _hgrn_cumulative(g, rows, scan_scratch, fwd):
    b_s = scan_scratch[4]
    for c in range(ROW_BLOCK // SCAN_CHUNK):
        lo = c * SCAN_CHUNK
        b_s[rows.start + lo:rows.start + lo + SCAN_CHUNK, :] = _cumulative(g[lo:lo + SCAN_CHUNK], fwd)


def _hgrn_scan(scan_scratch, st_s, fwd, emit):
    q_s, k_s, v_s, g_s, b_s, kc_s, bc_s, x_s, y_s, a_s, u_s = scan_scratch
    tile = q_s.shape[0]
    n_chunks = tile // SCAN_CHUNK
    n_sub = SCAN_CHUNK // SUB_BLOCK
    C = SCAN_CHUNK

    row = lax.broadcasted_iota(jnp.int32, (C, C), 0)
    col = lax.broadcasted_iota(jnp.int32, (C, C), 1)
    sub_row = lax.broadcasted_iota(jnp.int32, (SUB_BLOCK, C), 0)
    sub_col = lax.broadcasted_iota(jnp.int32, (SUB_BLOCK, C), 1)
    mid = C // 2 - 1 if fwd else C // 2
    last = C - 1 if fwd else 0

    def chunk_body(c, carry):
        idx = c if fwd else n_chunks - 1 - c
        base = pl.multiple_of(idx * C, C)
        rows = pl.ds(base, C)
        bc_s[...] = b_s[rows, :]
        kc_s[...] = k_s[rows, :]
        for hd in range(B_HEADS):
            lanes = slice(hd * B_DK, (hd + 1) * B_DK)
            q = q_s[rows, lanes]
            k = kc_s[:, lanes]
            v = v_s[rows, lanes]
            b = bc_s[:, lanes]
            st = st_s[hd]
            b_tot = b[last:last + 1]
            o = _dot_tb(q * jnp.exp(b), st)
            kd = k * jnp.exp(b_tot - b)
            st_s[hd] = st * jnp.exp(b_tot) + _dot_ta(v, kd)

            a_rows = []
            for i in range(n_sub):
                lo, hi = i * SUB_BLOCK, (i + 1) * SUB_BLOCK
                qi, bi = q[lo:hi], b[lo:hi]
                if fwd and i > 0:
                    r = b[lo - 1:lo]
                    ke = k[:lo] * jnp.exp(r - b[:lo])
                    ke = jnp.concatenate([ke, jnp.zeros((C - lo, B_DK), _F32)], axis=0)
                    a_i = _dot_tb(qi * jnp.exp(bi - r), ke)
                elif (not fwd) and i < n_sub - 1:
                    r = b[hi:hi + 1]
                    ke = k[hi:] * jnp.exp(r - b[hi:])
                    ke = jnp.concatenate([jnp.zeros((hi, B_DK), _F32), ke], axis=0)
                    a_i = _dot_tb(qi * jnp.exp(bi - r), ke)
                else:
                    a_i = jnp.zeros((SUB_BLOCK, C), _F32)
                for s in range(SUB_BLOCK):
                    j = lo + s
                    k_row = kc_s[j:j + 1, lanes]
                    b_row = bc_s[j:j + 1, lanes]
                    p = qi * k_row * jnp.exp(jnp.minimum(bi - b_row, 0.0))
                    a_i = jnp.where(sub_col == j, jnp.sum(p, axis=-1, keepdims=True), a_i)
                local = sub_col - lo
                if fwd:
                    keep = (sub_col < lo) | ((sub_col < hi) & (sub_row >= local))
                else:
                    keep = (sub_col >= hi) | ((sub_col >= lo) & (sub_row <= local))
                a_rows.append(jnp.where(keep, a_i, 0.0))
            a = jnp.concatenate(a_rows, axis=0)
            emit(rows, lanes, o + _dot(a, v))
        return carry

    def factored_tile():
        causal = (col <= row) if fwd else (col >= row)
        order = list(range(n_chunks)) if fwd else list(reversed(range(n_chunks)))
        chunk_rows = lambda c: slice(c * C, (c + 1) * C)
        head_lanes = lambda hd: slice(hd * B_DK, (hd + 1) * B_DK)

        def scale(hd, _):
            lanes = head_lanes(hd)
            for c in order:
                rows = chunk_rows(c)
                b = b_s[rows, lanes]
                e = jnp.exp(b - b[mid:mid + 1])
                x_s[rows, lanes] = (q_s[rows, lanes] * e).astype(_BF)
                y_s[rows, lanes] = (k_s[rows, lanes] / e).astype(_BF)

        def products(hd, _):
            lanes = head_lanes(hd)
            for c in order:
                rows = chunk_rows(c)
                yh = y_s[rows, lanes]
                a_s[hd, rows, :] = jnp.where(causal, _dot_tb(x_s[rows, lanes], yh), 0.0).astype(_BF)
                u_s[c, hd] = _dot_ta(v_s[rows, lanes], yh)

        def states(hd, _):
            lanes = head_lanes(hd)
            st = st_s[hd]
            for c in order:
                rho = b_s[c * C + mid:c * C + mid + 1, lanes]
                b_tot = b_s[c * C + last:c * C + last + 1, lanes]
                u = u_s[c, hd]
                u_s[c, hd] = st * jnp.exp(rho)
                st = st * jnp.exp(b_tot) + u * jnp.exp(b_tot - rho)
            st_s[hd] = st

        def outputs(hd, _):
            lanes = head_lanes(hd)
            for c in order:
                rows = chunk_rows(c)
                emit(rows, lanes, _dot_tb(x_s[rows, lanes], u_s[c, hd])
                     + _dot(a_s[hd, rows, :], v_s[rows, lanes]))

        _skewed(B_HEADS, scale, products, states, outputs)

    worst = jnp.zeros((1, B_WIDTH), _F32)
    for c in range(n_chunks):
        b_mid = b_s[c * C + mid:c * C + mid + 1, :]
        b_tot = b_s[c * C + last:c * C + last + 1, :]
        worst = jnp.maximum(worst, jnp.maximum(-b_mid, b_mid - b_tot))
    factored_ok = jnp.max(worst) < FACTORED_DECAY_LIMIT

    @pl.when(factored_ok)
    def _():
        factored_tile()

    @pl.when(jnp.logical_not(factored_ok))
    def _():
        lax.fori_loop(0, n_chunks, chunk_body, 0)


def _zero_state_at_sequence_start(st_s):
    @pl.when(pl.program_id(1) == 0)
    def _():
        st_s[...] = jnp.zeros_like(st_s)


def _bwd_kernel(layer, x_ref, gpre_ref, w_ref, lbraw_ref, o_ref, *scratch):
    scan_scratch, st_s = scratch[:-1], scratch[-1]
    _zero_state_at_sequence_start(st_s)
    lb = _lower_bound(lbraw_ref, layer) if layer > 0 else None

    def project(r, _):
        h = _rms(x_ref[0, _row_block(r), :], gpre_ref[...]).astype(_BF)
        return jnp.dot(h, w_ref[...], preferred_element_type=_F32)

    def gates(r, p):
        return _hgrn_gates(p[:, 0:512], p[:, 512:1024], p[:, 1024:1536], _row_block(r), lb, scan_scratch)

    def cumulate(r, g):
        _hgrn_cumulative(g, _row_block(r), scan_scratch, fwd=False)

    def emit(rows, lanes, o):
        o_ref[0, rows, lanes] = o

    _skewed(x_ref.shape[1] // ROW_BLOCK, project, gates, cumulate)
    _hgrn_scan(scan_scratch, st_s, False, emit)


def _mix_kernel(layer, x_ref, obwd_ref, gpre_ref, w_ref, vlng_ref, vlnb_ref, wsp_ref, bsp_ref,
                lbraw_ref, onorm_ref, wout_ref, gpost_ref, out_ref, *scratch):
    scan_scratch, (gate_s, st_s, cat_s) = scratch[:-3], scratch[-3:]
    _zero_state_at_sequence_start(st_s)
    lb = _lower_bound(lbraw_ref, layer) if layer > 0 else None
    n_blocks = x_ref.shape[1] // ROW_BLOCK

    def project(r, _):
        h = _rms(x_ref[0, _row_block(r), :], gpre_ref[...]).astype(_BF)
        return jnp.dot(h, w_ref[...], preferred_element_type=_F32)

    def elementwise(r, p):
        rows = _row_block(r)
        u = _gelu(p[:, 0:512])
        v = _gelu(p[:, 512:1024])
        vc = v - jnp.mean(v, axis=-1, keepdims=True)
        v = vc * lax.rsqrt(jnp.mean(vc * vc, axis=-1, keepdims=True) + EPS) * vlng_ref[...] + vlnb_ref[...]
        g = _hgrn_gates(p[:, 1024:1536], p[:, 1536:2048], p[:, 2048:2560], rows, lb, scan_scratch)
        gate_s[rows, :] = _silu(p[:, 2560:3072])
        return u, v.astype(_BF), g

    def dependent_matmuls(r, uvg):
        rows = _row_block(r)
        u, v, g = uvg
        for gi in range(A_GROUPS):
            lanes = slice(gi * A_GROUP_DIM, (gi + 1) * A_GROUP_DIM)
            sv = jnp.dot(wsp_ref[gi], v[:, lanes], preferred_element_type=_F32) + bsp_ref[gi]
            cat_s[rows, lanes] = (u[:, lanes] * sv).astype(_BF)
        _hgrn_cumulative(g, rows, scan_scratch, fwd=True)

    def emit(rows, lanes, o):
        o_sum = o + obwd_ref[0, rows, lanes]
        cat_s[rows, A_WIDTH + lanes.start:A_WIDTH + lanes.stop] = (
            _rms(o_sum, onorm_ref[...]) * gate_s[rows, lanes]).astype(_BF)

    def out_proj(r, _):
        return jnp.dot(cat_s[_row_block(r), :], wout_ref[...], preferred_element_type=_F32)

    def residual(r, y):
        rows = _row_block(r)
        out_ref[0, rows, :] = x_ref[0, rows, :] + _rms(y, gpost_ref[...])

    _skewed(n_blocks, project, elementwise, dependent_matmuls)
    _hgrn_scan(scan_scratch, st_s, True, emit)
    _skewed(n_blocks, out_proj, residual)


def _kv_kernel(mem_ref, g_ref, w_ref, k_ref, v_ref):
    m = _rms(mem_ref[0], g_ref[...])
    kv = _dot(m, w_ref[...])
    k_ref[0] = kv[:, :D_MODEL].astype(_BF)
    v_ref[0] = kv[:, D_MODEL:].astype(_BF)


def _xattn_kernel(x_ref, k_ref, v_ref, gpre_ref, wq_ref, wo_ref, gpost_ref, out_ref, cat_s):
    head_lanes = lambda hd: slice(hd * X_HEAD_DIM, (hd + 1) * X_HEAD_DIM)

    def query(r, _):
        h = _rms(x_ref[0, _row_block(r), :], gpre_ref[...])
        return _dot(h, wq_ref[...]).astype(_BF)

    def scores(r, q):
        return [_dot_tb(q[:, head_lanes(hd)], k_ref[0, :, head_lanes(hd)]) for hd in range(X_HEADS)]

    def softmax(r, s_heads):
        p_heads = []
        for s in s_heads:
            s = s * (X_HEAD_DIM ** -0.5)
            e = jnp.exp(s - jnp.max(s, axis=-1, keepdims=True))
            p_heads.append((e / jnp.sum(e, axis=-1, keepdims=True)).astype(_BF))
        return p_heads

    def values(r, p_heads):
        for hd, p in enumerate(p_heads):
            cat_s[_row_block(r), head_lanes(hd)] = _dot(p, v_ref[0, :, head_lanes(hd)]).astype(_BF)

    def out_proj(r, _):
        return jnp.dot(cat_s[_row_block(r), :], wo_ref[...], preferred_element_type=_F32)

    def residual(r, y):
        rows = _row_block(r)
        out_ref[0, rows, :] = x_ref[0, rows, :] + _rms(y, gpost_ref[...])

    _skewed(x_ref.shape[1] // ROW_BLOCK, query, scores, softmax, values, out_proj, residual)


def _ffn_kernel(x_ref, gpre_ref, wgu_ref, wdown_ref, gpost_ref, out_ref):
    x = x_ref[0]
    h = _rms(x, gpre_ref[...]).astype(_BF)
    y = jnp.zeros((x.shape[0], D_MODEL), _F32)
    for c in range(D_FF // FF_CHUNK):
        cols = slice(c * FF_CHUNK, (c + 1) * FF_CHUNK)
        gate = jnp.dot(h, wgu_ref[:, cols], preferred_element_type=_F32)
        up = jnp.dot(h, wgu_ref[:, D_FF + c * FF_CHUNK:D_FF + (c + 1) * FF_CHUNK],
                     preferred_element_type=_F32)
        y = y + _dot(_silu(gate) * up, wdown_ref[cols, :])
    out_ref[0] = x + _rms(y, gpost_ref[...])


def _params():
    return pltpu.CompilerParams(dimension_semantics=("arbitrary", "arbitrary"),
                                vmem_limit_bytes=VMEM_LIMIT_BYTES)


def _const_spec(shape):
    return pl.BlockSpec(shape, lambda b, t: (0,) * len(shape), pipeline_mode=pl.Buffered(1))


def _tile_spec(width, rev_tiles=None):
    if rev_tiles is None:
        return pl.BlockSpec((1, SEQ_TILE, width), lambda b, t: (b, t, 0))
    return pl.BlockSpec((1, SEQ_TILE, width), lambda b, t: (b, rev_tiles - 1 - t, 0))


def _scan_scratch():
    tile = lambda dt: pltpu.VMEM((SEQ_TILE, B_WIDTH), dt)
    chunk = lambda: pltpu.VMEM((SCAN_CHUNK, B_WIDTH), _F32)
    return [tile(_F32), tile(_F32), tile(_BF), tile(_F32), tile(_F32), chunk(), chunk(),
            tile(_BF), tile(_BF),
            pltpu.VMEM((B_HEADS, SEQ_TILE, SCAN_CHUNK), _BF),
            pltpu.VMEM((SEQ_TILE // SCAN_CHUNK, B_HEADS, B_DK, B_DK), _F32)]


def _state_scratch():
    return pltpu.VMEM((B_HEADS, B_DK, B_DK), _F32)


def _bwd_call(x, layer, gpre, w_bwd, lbraw):
    bn, ln, _ = x.shape
    nt = ln // SEQ_TILE
    return pl.pallas_call(
        functools.partial(_bwd_kernel, layer),
        grid=(bn, nt),
        in_specs=[_tile_spec(D_MODEL, nt), _const_spec(gpre.shape), _const_spec(w_bwd.shape),
                  _const_spec(lbraw.shape)],
        out_specs=_tile_spec(B_WIDTH, nt),
        out_shape=jax.ShapeDtypeStruct((bn, ln, B_WIDTH), _F32),
        scratch_shapes=_scan_scratch() + [_state_scratch()],
        compiler_params=_params(),
        name=f"hgrn_bwd_l{layer}",
    )(x, gpre, w_bwd, lbraw)


def _mix_call(x, obwd, layer, gpre, w_mix, vlng, vlnb, wsp, bsp, lbraw, onorm, wout, gpost):
    bn, ln, _ = x.shape
    nt = ln // SEQ_TILE
    consts = (gpre, w_mix, vlng, vlnb, wsp, bsp, lbraw, onorm, wout, gpost)
    return pl.pallas_call(
        functools.partial(_mix_kernel, layer),
        grid=(bn, nt),
        in_specs=[_tile_spec(D_MODEL), _tile_spec(B_WIDTH)] + [_const_spec(c.shape) for c in consts],
        out_specs=_tile_spec(D_MODEL),
        out_shape=jax.ShapeDtypeStruct(x.shape, _F32),
        scratch_shapes=_scan_scratch() + [pltpu.VMEM((SEQ_TILE, B_WIDTH), _F32), _state_scratch(),
                                          pltpu.VMEM((SEQ_TILE, D_MODEL), _BF)],
        compiler_params=_params(),
        name=f"token_mix_l{layer}",
    )(x, obwd, *consts)


def _kv_call(mem, layer, g, w):
    bn = mem.shape[0]
    spec = pl.BlockSpec((1, MEM_TOKENS, D_MODEL), lambda b, t: (b, 0, 0))
    return pl.pallas_call(
        _kv_kernel,
        grid=(bn, 1),
        in_specs=[spec, _const_spec(g.shape), _const_spec(w.shape)],
        out_specs=[spec, spec],
        out_shape=[jax.ShapeDtypeStruct(mem.shape, _BF)] * 2,
        compiler_params=_params(),
        name=f"mem_kv_l{layer}",
    )(mem, g, w)


def _xattn_call(x, k, v, layer, gpre, wq, wo, gpost):
    bn, ln, _ = x.shape
    nt = ln // SEQ_TILE
    mem_spec = pl.BlockSpec((1, MEM_TOKENS, D_MODEL), lambda b, t: (b, 0, 0))
    consts = (gpre, wq, wo, gpost)
    return pl.pallas_call(
        _xattn_kernel,
        grid=(bn, nt),
        in_specs=[_tile_spec(D_MODEL), mem_spec, mem_spec] + [_const_spec(c.shape) for c in consts],
        out_specs=_tile_spec(D_MODEL),
        out_shape=jax.ShapeDtypeStruct(x.shape, _F32),
        scratch_shapes=[pltpu.VMEM((SEQ_TILE, D_MODEL), _BF)],
        compiler_params=_params(),
        name=f"mem_xattn_l{layer}",
    )(x, k, v, *consts)


def _ffn_call(x, layer, gpre, wgu, wdown, gpost):
    bn, ln, _ = x.shape
    nt = ln // SEQ_TILE
    consts = (gpre, wgu, wdown, gpost)
    return pl.pallas_call(
        _ffn_kernel,
        grid=(bn, nt),
        in_specs=[_tile_spec(D_MODEL)] + [_const_spec(c.shape) for c in consts],
        out_specs=_tile_spec(D_MODEL),
        out_shape=jax.ShapeDtypeStruct(x.shape, _F32),
        compiler_params=_params(),
        name=f"swiglu_ffn_l{layer}",
    )(x, *consts)


def _layer_weights(l, p):
    w_in = p['w_in'][l].astype(_BF)
    cols = lambda a, b: w_in[:, a:b]
    row = lambda a: a[l][None, :]
    return dict(
        w_mix=jnp.concatenate([cols(0, 2048), cols(2560, 3584)], axis=1),
        w_bwd=jnp.concatenate([cols(1024, 1536), cols(2048, 3072)], axis=1),
        pre_mix_g=row(p['pre_mix_g']), v_ln_g=row(p['v_ln_g']), v_ln_b=row(p['v_ln_b']),
        w_spatial=p['w_spatial'][l].astype(_BF),
        b_spatial=jnp.broadcast_to(p['b_spatial'][l][:, :, None], (A_GROUPS, A_CHUNK, A_GROUP_DIM)),
        lb_fwd=p['lb_raw'][0], lb_bwd=p['lb_raw'][1],
        onorm_g=row(p['onorm_g']), w_out=p['w_out'][l].astype(_BF), post_mix_g=row(p['post_mix_g']),
        pre_x_g=row(p['pre_x_g']), mem_norm_g=row(p['mem_norm_g']),
        w_xq=p['w_xq'][l].astype(_BF), w_xkv=p['w_xkv'][l].astype(_BF), w_xo=p['w_xo'][l].astype(_BF),
        post_x_g=row(p['post_x_g']), pre_ffn_g=row(p['pre_ffn_g']),
        w_gu=p['w_gu'][l].astype(_BF), w_down=p['w_down'][l].astype(_BF), post_ffn_g=row(p['post_ffn_g']),
    )


def _trunk(x, mem, layers):
    assert x.shape[1] % SEQ_TILE == 0 and x.shape[2] == D_MODEL
    assert SEQ_TILE % ROW_BLOCK == 0 and ROW_BLOCK % SCAN_CHUNK == 0 and ROW_BLOCK == A_CHUNK
    for l, w in enumerate(layers):
        obwd = _bwd_call(x, l, w['pre_mix_g'], w['w_bwd'], w['lb_bwd'])
        x = _mix_call(x, obwd, l, w['pre_mix_g'], w['w_mix'], w['v_ln_g'], w['v_ln_b'], w['w_spatial'],
                      w['b_spatial'], w['lb_fwd'], w['onorm_g'], w['w_out'], w['post_mix_g'])
        k, v = _kv_call(mem, l, w['mem_norm_g'], w['w_xkv'])
        x = _xattn_call(x, k, v, l, w['pre_x_g'], w['w_xq'], w['w_xo'], w['post_x_g'])
        x = _ffn_call(x, l, w['pre_ffn_g'], w['w_gu'], w['w_down'], w['post_ffn_g'])
    return x


def kernel(x_prompt, x_sample, mem_prompt, mem_sample, pre_mix_g, w_in, v_ln_g, v_ln_b, w_spatial, b_spatial, lb_raw, onorm_g, w_out, post_mix_g, pre_x_g, mem_norm_g, w_xq, w_xkv, w_xo, post_x_g, pre_ffn_g, w_gu, w_down, post_ffn_g):
    p = dict(pre_mix_g=pre_mix_g, w_in=w_in, v_ln_g=v_ln_g, v_ln_b=v_ln_b,
             w_spatial=w_spatial, b_spatial=b_spatial, lb_raw=lb_raw, onorm_g=onorm_g,
             w_out=w_out, post_mix_g=post_mix_g, pre_x_g=pre_x_g, mem_norm_g=mem_norm_g,
             w_xq=w_xq, w_xkv=w_xkv, w_xo=w_xo, post_x_g=post_x_g, pre_ffn_g=pre_ffn_g,
             w_gu=w_gu, w_down=w_down, post_ffn_g=post_ffn_g)
    layers = [_layer_weights(l, p) for l in range(w_in.shape[0])]
    return (_trunk(x_prompt, mem_prompt, layers), _trunk(x_sample, mem_sample, layers))
```

```python
import functools

import jax
import jax.numpy as jnp
from jax import lax
from jax.experimental import pallas as pl
from jax.experimental.pallas import tpu as pltpu

D_MODEL = 1024
A_WIDTH = 512
A_GROUPS = 4
A_GROUP_DIM = 128
A_CHUNK = 128
B_WIDTH = 512
B_HEADS = 4
B_DK = 128
MEM_TOKENS = 256
X_HEADS = 4
X_HEAD_DIM = 256
D_FF = 2816
EPS = 1e-6

SEQ_TILE = 512
ROW_BLOCK = 256
SCAN_CHUNK = 64
SUB_BLOCK = 16
FF_CHUNK = 1408
FACTORED_DECAY_LIMIT = 55.0
VMEM_LIMIT_BYTES = 56 * 1024 * 1024

_BF = jnp.bfloat16
_F32 = jnp.float32


def _dot(a, b):
    return jnp.dot(a.astype(_BF), b.astype(_BF), preferred_element_type=_F32)


def _dot_tb(a, b):
    return lax.dot_general(a.astype(_BF), b.astype(_BF), (((1,), (1,)), ((), ())),
                           preferred_element_type=_F32)


def _dot_ta(a, b):
    return lax.dot_general(a.astype(_BF), b.astype(_BF), (((0,), (0,)), ((), ())),
                           preferred_element_type=_F32)


def _rms(x, g):
    return x * lax.rsqrt(jnp.mean(x * x, axis=-1, keepdims=True) + EPS) * g


def _gelu(x):
    return 0.5 * x * (1.0 + lax.erf(x * (2.0 ** -0.5)))


def _silu(x):
    return x * jax.nn.sigmoid(x)


def _log_sigmoid(x):
    return jnp.minimum(x, 0.0) - jnp.log1p(jnp.exp(-jnp.abs(x)))


def _lower_bound(lbraw_ref, layer):
    depth = lbraw_ref.shape[0]
    rows = [lbraw_ref[j:j + 1, :] for j in range(depth)]
    m = functools.reduce(jnp.maximum, rows)
    e = [jnp.exp(r - m) for r in rows]
    denom = functools.reduce(lambda a, b: a + b, e)
    return functools.reduce(lambda a, b: a + b, e[1:layer + 1]) / denom


def _skewed(n, *stages):
    vals = {}
    for i in range(n + len(stages) - 1):
        for k, stage in enumerate(stages):
            r = i - k
            if 0 <= r < n:
                vals[(k, r)] = stage(r, vals.pop((k - 1, r), None))


def _row_block(r):
    return slice(r * ROW_BLOCK, (r + 1) * ROW_BLOCK)


def _split3(x):
    hi = x.astype(_BF)
    r1 = x - hi.astype(_F32)
    mid = r1.astype(_BF)
    lo = (r1 - mid.astype(_F32)).astype(_BF)
    return hi, mid, lo


def _cumulative(g, fwd):
    n = g.shape[0]
    row = lax.broadcasted_iota(jnp.int32, (n, n), 0)
    col = lax.broadcasted_iota(jnp.int32, (n, n), 1)
    tri = jnp.where(col <= row if fwd else col >= row, 1.0, 0.0).astype(_BF)
    g_hi, g_mid, g_lo = _split3(g)
    return (jnp.dot(tri, g_hi, preferred_element_type=_F32)
            + jnp.dot(tri, g_mid, preferred_element_type=_F32)
            + jnp.dot(tri, g_lo, preferred_element_type=_F32))


def _hgrn_gates(q_raw, f, v_raw, rows, lb, scan_scratch):
    q_s, k_s, v_s, g_s = scan_scratch[:4]
    q_s[rows, :] = _silu(q_raw) * (B_DK ** -0.5)
    if lb is None:
        g = _log_sigmoid(f)
        k_s[rows, :] = jax.nn.sigmoid(-f)
    else:
        forget = lb + (1.0 - lb) * jax.nn.sigmoid(f)
        g = jnp.log(forget)
        k_s[rows, :] = 1.0 - forget
    g_s[rows, :] = g
    v_s[rows, :] = v_raw.astype(_BF)
    return g


def _hgrn_cumulative(g, rows, scan_scratch, fwd):
    b_s = scan_scratch[4]
    for c in range(ROW_BLOCK // SCAN_CHUNK):
        lo = c * SCAN_CHUNK
        b_s[rows.start + lo:rows.start + lo + SCAN_CHUNK, :] = _cumulative(g[lo:lo + SCAN_CHUNK], fwd)


def _hgrn_scan(scan_scratch, st_s, fwd, emit):
    q_s, k_s, v_s, g_s, b_s, kc_s, bc_s, x_s, y_s, a_s, u_s = scan_scratch
    tile = q_s.shape[0]
    n_chunks = tile // SCAN_CHUNK
    n_sub = SCAN_CHUNK // SUB_BLOCK
    C = SCAN_CHUNK

    row = lax.broadcasted_iota(jnp.int32, (C, C), 0)
    col = lax.broadcasted_iota(jnp.int32, (C, C), 1)
    sub_row = lax.broadcasted_iota(jnp.int32, (SUB_BLOCK, C), 0)
    sub_col = lax.broadcasted_iota(jnp.int32, (SUB_BLOCK, C), 1)
    mid = C // 2 - 1 if fwd else C // 2
    last = C - 1 if fwd else 0

    def chunk_body(c, carry):
        idx = c if fwd else n_chunks - 1 - c
        base = pl.multiple_of(idx * C, C)
        rows = pl.ds(base, C)
        bc_s[...] = b_s[rows, :]
        kc_s[...] = k_s[rows, :]
        for hd in range(B_HEADS):
            lanes = slice(hd * B_DK, (hd + 1) * B_DK)
            q = q_s[rows, lanes]
            k = kc_s[:, lanes]
            v = v_s[rows, lanes]
            b = bc_s[:, lanes]
            st = st_s[hd]
            b_tot = b[last:last + 1]
            o = _dot_tb(q * jnp.exp(b), st)
            kd = k * jnp.exp(b_tot - b)
            st_s[hd] = st * jnp.exp(b_tot) + _dot_ta(v, kd)

            a_rows = []
            for i in range(n_sub):
                lo, hi = i * SUB_BLOCK, (i + 1) * SUB_BLOCK
                qi, bi = q[lo:hi], b[lo:hi]
                if fwd and i > 0:
                    r = b[lo - 1:lo]
                    ke = k[:lo] * jnp.exp(r - b[:lo])
                    ke = jnp.concatenate([ke, jnp.zeros((C - lo, B_DK), _F32)], axis=0)
                    a_i = _dot_tb(qi * jnp.exp(bi - r), ke)
                elif (not fwd) and i < n_sub - 1:
                    r = b[hi:hi + 1]
                    ke = k[hi:] * jnp.exp(r - b[hi:])
                    ke = jnp.concatenate([jnp.zeros((hi, B_DK), _F32), ke], axis=0)
                    a_i = _dot_tb(qi * jnp.exp(bi - r), ke)
                else:
                    a_i = jnp.zeros((SUB_BLOCK, C), _F32)
                for s in range(SUB_BLOCK):
                    j = lo + s
                    k_row = kc_s[j:j + 1, lanes]
                    b_row = bc_s[j:j + 1, lanes]
                    p = qi * k_row * jnp.exp(jnp.minimum(bi - b_row, 0.0))
                    a_i = jnp.where(sub_col == j, jnp.sum(p, axis=-1, keepdims=True), a_i)
                local = sub_col - lo
                if fwd:
                    keep = (sub_col < lo) | ((sub_col < hi) & (sub_row >= local))
                else:
                    keep = (sub_col >= hi) | ((sub_col >= lo) & (sub_row <= local))
                a_rows.append(jnp.where(keep, a_i, 0.0))
            a = jnp.concatenate(a_rows, axis=0)
            emit(rows, lanes, o + _dot(a, v))
        return carry

    def factored_tile():
        causal = (col <= row) if fwd else (col >= row)
        order = list(range(n_chunks)) if fwd else list(reversed(range(n_chunks)))
        chunk_rows = lambda c: slice(c * C, (c + 1) * C)
        head_lanes = lambda hd: slice(hd * B_DK, (hd + 1) * B_DK)

        def scale(hd, _):
            lanes = head_lanes(hd)
            for c in order:
                rows = chunk_rows(c)
                b = b_s[rows, lanes]
                e = jnp.exp(b - b[mid:mid + 1])
                x_s[rows, lanes] = (q_s[rows, lanes] * e).astype(_BF)
                y_s[rows, lanes] = (k_s[rows, lanes] / e).astype(_BF)

        def products(hd, _):
            lanes = head_lanes(hd)
            for c in order:
                rows = chunk_rows(c)
                yh = y_s[rows, lanes]
                a_s[hd, rows, :] = jnp.where(causal, _dot_tb(x_s[rows, lanes], yh), 0.0).astype(_BF)
                u_s[c, hd] = _dot_ta(v_s[rows, lanes], yh)

        def states(hd, _):
            lanes = head_lanes(hd)
            st = st_s[hd]
            for c in order:
                rho = b_s[c * C + mid:c * C + mid + 1, lanes]
                b_tot = b_s[c * C + last:c * C + last + 1, lanes]
                u = u_s[c, hd]
                u_s[c, hd] = st * jnp.exp(rho)
                st = st * jnp.exp(b_tot) + u * jnp.exp(b_tot - rho)
            st_s[hd] = st

        def outputs(hd, _):
            lanes = head_lanes(hd)
            for c in order:
                rows = chunk_rows(c)
                emit(rows, lanes, _dot_tb(x_s[rows, lanes], u_s[c, hd])
                     + _dot(a_s[hd, rows, :], v_s[rows, lanes]))

        _skewed(B_HEADS, scale, products, states, outputs)

    worst = jnp.zeros((1, B_WIDTH), _F32)
    for c in range(n_chunks):
        b_mid = b_s[c * C + mid:c * C + mid + 1, :]
        b_tot = b_s[c * C + last:c * C + last + 1, :]
        worst = jnp.maximum(worst, jnp.maximum(-b_mid, b_mid - b_tot))
    factored_ok = jnp.max(worst) < FACTORED_DECAY_LIMIT

    @pl.when(factored_ok)
    def _():
        factored_tile()

    @pl.when(jnp.logical_not(factored_ok))
    def _():
        lax.fori_loop(0, n_chunks, chunk_body, 0)


def _zero_state_at_sequence_start(st_s):
    @pl.when(pl.program_id(1) == 0)
    def _():
        st_s[...] = jnp.zeros_like(st_s)


def _bwd_kernel(layer, x_ref, gpre_ref, w_ref, lbraw_ref, o_ref, *scratch):
    scan_scratch, st_s = scratch[:-1], scratch[-1]
    _zero_state_at_sequence_start(st_s)
    lb = _lower_bound(lbraw_ref, layer) if layer > 0 else None

    def project(r, _):
        h = _rms(x_ref[0, _row_block(r), :], gpre_ref[...]).astype(_BF)
        return jnp.dot(h, w_ref[...], preferred_element_type=_F32)

    def gates(r, p):
        return _hgrn_gates(p[:, 0:512], p[:, 512:1024], p[:, 1024:1536], _row_block(r), lb, scan_scratch)

    def cumulate(r, g):
        _hgrn_cumulative(g, _row_block(r), scan_scratch, fwd=False)

    def emit(rows, lanes, o):
        o_ref[0, rows, lanes] = o

    _skewed(x_ref.shape[1] // ROW_BLOCK, project, gates, cumulate)
    _hgrn_scan(scan_scratch, st_s, False, emit)


def _mix_kernel(layer, x_ref, obwd_ref, gpre_ref, w_ref, vlng_ref, vlnb_ref, wsp_ref, bsp_ref,
                lbraw_ref, onorm_ref, wout_ref, gpost_ref, out_ref, *scratch):
    scan_scratch, (gate_s, st_s, cat_s) = scratch[:-3], scratch[-3:]
    _zero_state_at_sequence_start(st_s)
    lb = _lower_bound(lbraw_ref, layer) if layer > 0 else None
    n_blocks = x_ref.shape[1] // ROW_BLOCK

    def project(r, _):
        h = _rms(x_ref[0, _row_block(r), :], gpre_ref[...]).astype(_BF)
        return jnp.dot(h, w_ref[...], preferred_element_type=_F32)

    def elementwise(r, p):
        rows = _row_block(r)
        u = _gelu(p[:, 0:512])
        v = _gelu(p[:, 512:1024])
        vc = v - jnp.mean(v, axis=-1, keepdims=True)
        v = vc * lax.rsqrt(jnp.mean(vc * vc, axis=-1, keepdims=True) + EPS) * vlng_ref[...] + vlnb_ref[...]
        g = _hgrn_gates(p[:, 1024:1536], p[:, 1536:2048], p[:, 2048:2560], rows, lb, scan_scratch)
        gate_s[rows, :] = _silu(p[:, 2560:3072])
        return u, v.astype(_BF), g

    def dependent_matmuls(r, uvg):
        rows = _row_block(r)
        u, v, g = uvg
        for n in range(ROW_BLOCK // A_CHUNK):
            chunk = slice(n * A_CHUNK, (n + 1) * A_CHUNK)
            for gi in range(A_GROUPS):
                lanes = slice(gi * A_GROUP_DIM, (gi + 1) * A_GROUP_DIM)
                sv = jnp.dot(wsp_ref[gi], v[chunk, lanes], preferred_element_type=_F32) + bsp_ref[gi]
                cat_s[rows.start + chunk.start:rows.start + chunk.stop, lanes] = (
                    u[chunk, lanes] * sv).astype(_BF)
        _hgrn_cumulative(g, rows, scan_scratch, fwd=True)

    def emit(rows, lanes, o):
        o_sum = o + obwd_ref[0, rows, lanes]
        cat_s[rows, A_WIDTH + lanes.start:A_WIDTH + lanes.stop] = (
            _rms(o_sum, onorm_ref[...]) * gate_s[rows, lanes]).astype(_BF)

    def out_proj(r, _):
        return jnp.dot(cat_s[_row_block(r), :], wout_ref[...], preferred_element_type=_F32)

    def residual(r, y):
        rows = _row_block(r)
        out_ref[0, rows, :] = x_ref[0, rows, :] + _rms(y, gpost_ref[...])

    _skewed(n_blocks, project, elementwise, dependent_matmuls)
    _hgrn_scan(scan_scratch, st_s, True, emit)
    _skewed(n_blocks, out_proj, residual)


def _kv_kernel(mem_ref, g_ref, w_ref, k_ref, v_ref):
    m = _rms(mem_ref[0], g_ref[...])
    kv = _dot(m, w_ref[...])
    k_ref[0] = kv[:, :D_MODEL].astype(_BF)
    v_ref[0] = kv[:, D_MODEL:].astype(_BF)


def _xattn_kernel(x_ref, k_ref, v_ref, gpre_ref, wq_ref, wo_ref, gpost_ref, out_ref, cat_s):
    head_lanes = lambda hd: slice(hd * X_HEAD_DIM, (hd + 1) * X_HEAD_DIM)

    def query(r, _):
        h = _rms(x_ref[0, _row_block(r), :], gpre_ref[...])
        return _dot(h, wq_ref[...]).astype(_BF)

    def scores(r, q):
        return [_dot_tb(q[:, head_lanes(hd)], k_ref[0, :, head_lanes(hd)]) for hd in range(X_HEADS)]

    def softmax(r, s_heads):
        p_heads = []
        for s in s_heads:
            s = s * (X_HEAD_DIM ** -0.5)
            e = jnp.exp(s - jnp.max(s, axis=-1, keepdims=True))
            p_heads.append((e / jnp.sum(e, axis=-1, keepdims=True)).astype(_BF))
        return p_heads

    def values(r, p_heads):
        for hd, p in enumerate(p_heads):
            cat_s[_row_block(r), head_lanes(hd)] = _dot(p, v_ref[0, :, head_lanes(hd)]).astype(_BF)

    def out_proj(r, _):
        return jnp.dot(cat_s[_row_block(r), :], wo_ref[...], preferred_element_type=_F32)

    def residual(r, y):
        rows = _row_block(r)
        out_ref[0, rows, :] = x_ref[0, rows, :] + _rms(y, gpost_ref[...])

    _skewed(x_ref.shape[1] // ROW_BLOCK, query, scores, softmax, values, out_proj, residual)


def _ffn_kernel(x_ref, gpre_ref, wgu_ref, wdown_ref, gpost_ref, out_ref):
    x = x_ref[0]
    h = _rms(x, gpre_ref[...]).astype(_BF)
    y = jnp.zeros((x.shape[0], D_MODEL), _F32)
    for c in range(D_FF // FF_CHUNK):
        cols = slice(c * FF_CHUNK, (c + 1) * FF_CHUNK)
        gate = jnp.dot(h, wgu_ref[:, cols], preferred_element_type=_F32)
        up = jnp.dot(h, wgu_ref[:, D_FF + c * FF_CHUNK:D_FF + (c + 1) * FF_CHUNK],
                     preferred_element_type=_F32)
        y = y + _dot(_silu(gate) * up, wdown_ref[cols, :])
    out_ref[0] = x + _rms(y, gpost_ref[...])


def _params():
    return pltpu.CompilerParams(dimension_semantics=("arbitrary", "arbitrary"),
                                vmem_limit_bytes=VMEM_LIMIT_BYTES)


def _const_spec(shape):
    return pl.BlockSpec(shape, lambda b, t: (0,) * len(shape), pipeline_mode=pl.Buffered(1))


def _tile_spec(width, rev_tiles=None):
    if rev_tiles is None:
        return pl.BlockSpec((1, SEQ_TILE, width), lambda b, t: (b, t, 0))
    return pl.BlockSpec((1, SEQ_TILE, width), lambda b, t: (b, rev_tiles - 1 - t, 0))


def _scan_scratch():
    tile = lambda dt: pltpu.VMEM((SEQ_TILE, B_WIDTH), dt)
    chunk = lambda: pltpu.VMEM((SCAN_CHUNK, B_WIDTH), _F32)
    return [tile(_F32), tile(_F32), tile(_BF), tile(_F32), tile(_F32), chunk(), chunk(),
            tile(_BF), tile(_BF),
            pltpu.VMEM((B_HEADS, SEQ_TILE, SCAN_CHUNK), _BF),
            pltpu.VMEM((SEQ_TILE // SCAN_CHUNK, B_HEADS, B_DK, B_DK), _F32)]


def _state_scratch():
    return pltpu.VMEM((B_HEADS, B_DK, B_DK), _F32)


def _bwd_call(x, layer, gpre, w_bwd, lbraw):
    bn, ln, _ = x.shape
    nt = ln // SEQ_TILE
    return pl.pallas_call(
        functools.partial(_bwd_kernel, layer),
        grid=(bn, nt),
        in_specs=[_tile_spec(D_MODEL, nt), _const_spec(gpre.shape), _const_spec(w_bwd.shape),
                  _const_spec(lbraw.shape)],
        out_specs=_tile_spec(B_WIDTH, nt),
        out_shape=jax.ShapeDtypeStruct((bn, ln, B_WIDTH), _F32),
        scratch_shapes=_scan_scratch() + [_state_scratch()],
        compiler_params=_params(),
        name=f"hgrn_bwd_l{layer}",
    )(x, gpre, w_bwd, lbraw)


def _mix_call(x, obwd, layer, gpre, w_mix, vlng, vlnb, wsp, bsp, lbraw, onorm, wout, gpost):
    bn, ln, _ = x.shape
    nt = ln // SEQ_TILE
    consts = (gpre, w_mix, vlng, vlnb, wsp, bsp, lbraw, onorm, wout, gpost)
    return pl.pallas_call(
        functools.partial(_mix_kernel, layer),
        grid=(bn, nt),
        in_specs=[_tile_spec(D_MODEL), _tile_spec(B_WIDTH)] + [_const_spec(c.shape) for c in consts],
        out_specs=_tile_spec(D_MODEL),
        out_shape=jax.ShapeDtypeStruct(x.shape, _F32),
        scratch_shapes=_scan_scratch() + [pltpu.VMEM((SEQ_TILE, B_WIDTH), _F32), _state_scratch(),
                                          pltpu.VMEM((SEQ_TILE, D_MODEL), _BF)],
        compiler_params=_params(),
        name=f"token_mix_l{layer}",
    )(x, obwd, *consts)


def _kv_call(mem, layer, g, w):
    bn = mem.shape[0]
    spec = pl.BlockSpec((1, MEM_TOKENS, D_MODEL), lambda b, t: (b, 0, 0))
    return pl.pallas_call(
        _kv_kernel,
        grid=(bn, 1),
        in_specs=[spec, _const_spec(g.shape), _const_spec(w.shape)],
        out_specs=[spec, spec],
        out_shape=[jax.ShapeDtypeStruct(mem.shape, _BF)] * 2,
        compiler_params=_params(),
        name=f"mem_kv_l{layer}",
    )(mem, g, w)


def _xattn_call(x, k, v, layer, gpre, wq, wo, gpost):
    bn, ln, _ = x.shape
    nt = ln // SEQ_TILE
    mem_spec = pl.BlockSpec((1, MEM_TOKENS, D_MODEL), lambda b, t: (b, 0, 0))
    consts = (gpre, wq, wo, gpost)
    return pl.pallas_call(
        _xattn_kernel,
        grid=(bn, nt),
        in_specs=[_tile_spec(D_MODEL), mem_spec, mem_spec] + [_const_spec(c.shape) for c in consts],
        out_specs=_tile_spec(D_MODEL),
        out_shape=jax.ShapeDtypeStruct(x.shape, _F32),
        scratch_shapes=[pltpu.VMEM((SEQ_TILE, D_MODEL), _BF)],
        compiler_params=_params(),
        name=f"mem_xattn_l{layer}",
    )(x, k, v, *consts)


def _ffn_call(x, layer, gpre, wgu, wdown, gpost):
    bn, ln, _ = x.shape
    nt = ln // SEQ_TILE
    consts = (gpre, wgu, wdown, gpost)
    return pl.pallas_call(
        _ffn_kernel,
        grid=(bn, nt),
        in_specs=[_tile_spec(D_MODEL)] + [_const_spec(c.shape) for c in consts],
        out_specs=_tile_spec(D_MODEL),
        out_shape=jax.ShapeDtypeStruct(x.shape, _F32),
        compiler_params=_params(),
        name=f"swiglu_ffn_l{layer}",
    )(x, *consts)


def _layer_weights(l, p):
    w_in = p['w_in'][l].astype(_BF)
    cols = lambda a, b: w_in[:, a:b]
    row = lambda a: a[l][None, :]
    return dict(
        w_mix=jnp.concatenate([cols(0, 2048), cols(2560, 3584)], axis=1),
        w_bwd=jnp.concatenate([cols(1024, 1536), cols(2048, 3072)], axis=1),
        pre_mix_g=row(p['pre_mix_g']), v_ln_g=row(p['v_ln_g']), v_ln_b=row(p['v_ln_b']),
        w_spatial=p['w_spatial'][l].astype(_BF),
        b_spatial=jnp.broadcast_to(p['b_spatial'][l][:, :, None], (A_GROUPS, A_CHUNK, A_GROUP_DIM)),
        lb_fwd=p['lb_raw'][0], lb_bwd=p['lb_raw'][1],
        onorm_g=row(p['onorm_g']), w_out=p['w_out'][l].astype(_BF), post_mix_g=row(p['post_mix_g']),
        pre_x_g=row(p['pre_x_g']), mem_norm_g=row(p['mem_norm_g']),
        w_xq=p['w_xq'][l].astype(_BF), w_xkv=p['w_xkv'][l].astype(_BF), w_xo=p['w_xo'][l].astype(_BF),
        post_x_g=row(p['post_x_g']), pre_ffn_g=row(p['pre_ffn_g']),
        w_gu=p['w_gu'][l].astype(_BF), w_down=p['w_down'][l].astype(_BF), post_ffn_g=row(p['post_ffn_g']),
    )


def _trunk(x, mem, layers):
    assert x.shape[1] % SEQ_TILE == 0 and x.shape[2] == D_MODEL
    assert SEQ_TILE % ROW_BLOCK == 0 and ROW_BLOCK % SCAN_CHUNK == 0 and ROW_BLOCK % A_CHUNK == 0
    for l, w in enumerate(layers):
        obwd = _bwd_call(x, l, w['pre_mix_g'], w['w_bwd'], w['lb_bwd'])
        x = _mix_call(x, obwd, l, w['pre_mix_g'], w['w_mix'], w['v_ln_g'], w['v_ln_b'], w['w_spatial'],
                      w['b_spatial'], w['lb_fwd'], w['onorm_g'], w['w_out'], w['post_mix_g'])
        k, v = _kv_call(mem, l, w['mem_norm_g'], w['w_xkv'])
        x = _xattn_call(x, k, v, l, w['pre_x_g'], w['w_xq'], w['w_xo'], w['post_x_g'])
        x = _ffn_call(x, l, w['pre_ffn_g'], w['w_gu'], w['w_down'], w['post_ffn_g'])
    return x


def kernel(x_prompt, x_sample, mem_prompt, mem_sample, pre_mix_g, w_in, v_ln_g, v_ln_b, w_spatial, b_spatial, lb_raw, onorm_g, w_out, post_mix_g, pre_x_g, mem_norm_g, w_xq, w_xkv, w_xo, post_x_g, pre_ffn_g, w_gu, w_down, post_ffn_g):
    p = dict(pre_mix_g=pre_mix_g, w_in=w_in, v_ln_g=v_ln_g, v_ln_b=v_ln_b,
             w_spatial=w_spatial, b_spatial=b_spatial, lb_raw=lb_raw, onorm_g=onorm_g,
             w_out=w_out, post_mix_g=post_mix_g, pre_x_g=pre_x_g, mem_norm_g=mem_norm_g,
             w_xq=w_xq, w_xkv=w_xkv, w_xo=w_xo, post_x_g=post_x_g, pre_ffn_g=pre_ffn_g,
             w_gu=w_gu, w_down=w_down, post_ffn_g=post_ffn_g)
    layers = [_layer_weights(l, p) for l in range(w_in.shape[0])]
    return (_trunk(x_prompt, mem_prompt, layers), _trunk(x_sample, mem_sample, layers))
```

```python
import functools

import jax
import jax.numpy as jnp
from jax import lax
from jax.experimental import pallas as pl
from jax.experimental.pallas import tpu as pltpu

D_MODEL = 1024
A_WIDTH = 512
A_GROUPS = 4
A_GROUP_DIM = 128
A_CHUNK = 128
B_WIDTH = 512
B_HEADS = 4
B_DK = 128
MEM_TOKENS = 256
X_HEADS = 4
X_HEAD_DIM = 256
D_FF = 2816
EPS = 1e-6

SEQ_TILE = 1024
ROW_BLOCK = 256
SCAN_CHUNK = 64
SUB_BLOCK = 16
FF_CHUNK = 1408
FACTORED_DECAY_LIMIT = 55.0
VMEM_LIMIT_BYTES = 56 * 1024 * 1024

_BF = jnp.bfloat16
_F32 = jnp.float32


def _dot(a, b):
    return jnp.dot(a.astype(_BF), b.astype(_BF), preferred_element_type=_F32)


def _dot_tb(a, b):
    return lax.dot_general(a.astype(_BF), b.astype(_BF), (((1,), (1,)), ((), ())),
                           preferred_element_type=_F32)


def _dot_ta(a, b):
    return lax.dot_general(a.astype(_BF), b.astype(_BF), (((0,), (0,)), ((), ())),
                           preferred_element_type=_F32)


def _rms(x, g):
    return x * lax.rsqrt(jnp.mean(x * x, axis=-1, keepdims=True) + EPS) * g


def _gelu(x):
    return 0.5 * x * (1.0 + lax.erf(x * (2.0 ** -0.5)))


def _silu(x):
    return x * jax.nn.sigmoid(x)


def _log_sigmoid(x):
    return jnp.minimum(x, 0.0) - jnp.log1p(jnp.exp(-jnp.abs(x)))


def _lower_bound(lbraw_ref, layer):
    depth = lbraw_ref.shape[0]
    rows = [lbraw_ref[j:j + 1, :] for j in range(depth)]
    m = functools.reduce(jnp.maximum, rows)
    e = [jnp.exp(r - m) for r in rows]
    denom = functools.reduce(lambda a, b: a + b, e)
    return functools.reduce(lambda a, b: a + b, e[1:layer + 1]) / denom


def _skewed(n, *stages):
    vals = {}
    for i in range(n + len(stages) - 1):
        for k, stage in enumerate(stages):
            r = i - k
            if 0 <= r < n:
                vals[(k, r)] = stage(r, vals.pop((k - 1, r), None))


def _row_block(r):
    return slice(r * ROW_BLOCK, (r + 1) * ROW_BLOCK)


def _split3(x):
    hi = x.astype(_BF)
    r1 = x - hi.astype(_F32)
    mid = r1.astype(_BF)
    lo = (r1 - mid.astype(_F32)).astype(_BF)
    return hi, mid, lo


def _cumulative(g, fwd):
    n = g.shape[0]
    row = lax.broadcasted_iota(jnp.int32, (n, n), 0)
    col = lax.broadcasted_iota(jnp.int32, (n, n), 1)
    tri = jnp.where(col <= row if fwd else col >= row, 1.0, 0.0).astype(_BF)
    g_hi, g_mid, g_lo = _split3(g)
    return (jnp.dot(tri, g_hi, preferred_element_type=_F32)
            + jnp.dot(tri, g_mid, preferred_element_type=_F32)
            + jnp.dot(tri, g_lo, preferred_element_type=_F32))


def _hgrn_gates(q_raw, f, v_raw, rows, lb, scan_scratch):
    q_s, k_s, v_s, g_s = scan_scratch[:4]
    q_s[rows, :] = _silu(q_raw) * (B_DK ** -0.5)
    if lb is None:
        g = _log_sigmoid(f)
        k_s[rows, :] = jax.nn.sigmoid(-f)
    else:
        forget = lb + (1.0 - lb) * jax.nn.sigmoid(f)
        g = jnp.log(forget)
        k_s[rows, :] = 1.0 - forget
    g_s[rows, :] = g
    v_s[rows, :] = v_raw.astype(_BF)
    return g


def _hgrn_cumulative(g, rows, scan_scratch, fwd):
    b_s = scan_scratch[4]
    for c in range(ROW_BLOCK // SCAN_CHUNK):
        lo = c * SCAN_CHUNK
        b_s[rows.start + lo:rows.start + lo + SCAN_CHUNK, :] = _cumulative(g[lo:lo + SCAN_CHUNK], fwd)


def _hgrn_scan(scan_scratch, st_s, fwd, emit):
    q_s, k_s, v_s, g_s, b_s, kc_s, bc_s, x_s, y_s, a_s, u_s = scan_scratch
    tile = q_s.shape[0]
    n_chunks = tile // SCAN_CHUNK
    n_sub = SCAN_CHUNK // SUB_BLOCK
    C = SCAN_CHUNK

    row = lax.broadcasted_iota(jnp.int32, (C, C), 0)
    col = lax.broadcasted_iota(jnp.int32, (C, C), 1)
    sub_row = lax.broadcasted_iota(jnp.int32, (SUB_BLOCK, C), 0)
    sub_col = lax.broadcasted_iota(jnp.int32, (SUB_BLOCK, C), 1)
    mid = C // 2 - 1 if fwd else C // 2
    last = C - 1 if fwd else 0

    def chunk_body(c, carry):
        idx = c if fwd else n_chunks - 1 - c
        base = pl.multiple_of(idx * C, C)
        rows = pl.ds(base, C)
        bc_s[...] = b_s[rows, :]
        kc_s[...] = k_s[rows, :]
        for hd in range(B_HEADS):
            lanes = slice(hd * B_DK, (hd + 1) * B_DK)
            q = q_s[rows, lanes]
            k = kc_s[:, lanes]
            v = v_s[rows, lanes]
            b = bc_s[:, lanes]
            st = st_s[hd]
            b_tot = b[last:last + 1]
            o = _dot_tb(q * jnp.exp(b), st)
            kd = k * jnp.exp(b_tot - b)
            st_s[hd] = st * jnp.exp(b_tot) + _dot_ta(v, kd)

            a_rows = []
            for i in range(n_sub):
                lo, hi = i * SUB_BLOCK, (i + 1) * SUB_BLOCK
                qi, bi = q[lo:hi], b[lo:hi]
                if fwd and i > 0:
                    r = b[lo - 1:lo]
                    ke = k[:lo] * jnp.exp(r - b[:lo])
                    ke = jnp.concatenate([ke, jnp.zeros((C - lo, B_DK), _F32)], axis=0)
                    a_i = _dot_tb(qi * jnp.exp(bi - r), ke)
                elif (not fwd) and i < n_sub - 1:
                    r = b[hi:hi + 1]
                    ke = k[hi:] * jnp.exp(r - b[hi:])
                    ke = jnp.concatenate([jnp.zeros((hi, B_DK), _F32), ke], axis=0)
                    a_i = _dot_tb(qi * jnp.exp(bi - r), ke)
                else:
                    a_i = jnp.zeros((SUB_BLOCK, C), _F32)
                for s in range(SUB_BLOCK):
                    j = lo + s
                    k_row = kc_s[j:j + 1, lanes]
                    b_row = bc_s[j:j + 1, lanes]
                    p = qi * k_row * jnp.exp(jnp.minimum(bi - b_row, 0.0))
                    a_i = jnp.where(sub_col == j, jnp.sum(p, axis=-1, keepdims=True), a_i)
                local = sub_col - lo
                if fwd:
                    keep = (sub_col < lo) | ((sub_col < hi) & (sub_row >= local))
                else:
                    keep = (sub_col >= hi) | ((sub_col >= lo) & (sub_row <= local))
                a_rows.append(jnp.where(keep, a_i, 0.0))
            a = jnp.concatenate(a_rows, axis=0)
            emit(rows, lanes, o + _dot(a, v))
        return carry

    def factored_tile():
        causal = (col <= row) if fwd else (col >= row)
        order = list(range(n_chunks)) if fwd else list(reversed(range(n_chunks)))
        chunk_rows = lambda c: slice(c * C, (c + 1) * C)
        head_lanes = lambda hd: slice(hd * B_DK, (hd + 1) * B_DK)

        def scale(hd, _):
            lanes = head_lanes(hd)
            for c in order:
                rows = chunk_rows(c)
                b = b_s[rows, lanes]
                e = jnp.exp(b - b[mid:mid + 1])
                x_s[rows, lanes] = (q_s[rows, lanes] * e).astype(_BF)
                y_s[rows, lanes] = (k_s[rows, lanes] / e).astype(_BF)

        def products(hd, _):
            lanes = head_lanes(hd)
            for c in order:
                rows = chunk_rows(c)
                yh = y_s[rows, lanes]
                a_s[hd, rows, :] = jnp.where(causal, _dot_tb(x_s[rows, lanes], yh), 0.0).astype(_BF)
                u_s[c, hd] = _dot_ta(v_s[rows, lanes], yh)

        def states(hd, _):
            lanes = head_lanes(hd)
            st = st_s[hd]
            for c in order:
                rho = b_s[c * C + mid:c * C + mid + 1, lanes]
                b_tot = b_s[c * C + last:c * C + last + 1, lanes]
                u = u_s[c, hd]
                u_s[c, hd] = st * jnp.exp(rho)
                st = st * jnp.exp(b_tot) + u * jnp.exp(b_tot - rho)
            st_s[hd] = st

        def outputs(hd, _):
            lanes = head_lanes(hd)
            for c in order:
                rows = chunk_rows(c)
                emit(rows, lanes, _dot_tb(x_s[rows, lanes], u_s[c, hd])
                     + _dot(a_s[hd, rows, :], v_s[rows, lanes]))

        _skewed(B_HEADS, scale, products, states, outputs)

    worst = jnp.zeros((1, B_WIDTH), _F32)
    for c in range(n_chunks):
        b_mid = b_s[c * C + mid:c * C + mid + 1, :]
        b_tot = b_s[c * C + last:c * C + last + 1, :]
        worst = jnp.maximum(worst, jnp.maximum(-b_mid, b_mid - b_tot))
    factored_ok = jnp.max(worst) < FACTORED_DECAY_LIMIT

    @pl.when(factored_ok)
    def _():
        factored_tile()

    @pl.when(jnp.logical_not(factored_ok))
    def _():
        lax.fori_loop(0, n_chunks, chunk_body, 0)


def _zero_state_at_sequence_start(st_s):
    @pl.when(pl.program_id(1) == 0)
    def _():
        st_s[...] = jnp.zeros_like(st_s)


def _bwd_kernel(layer, x_ref, gpre_ref, w_ref, lbraw_ref, o_ref, *scratch):
    scan_scratch, st_s = scratch[:-1], scratch[-1]
    _zero_state_at_sequence_start(st_s)
    lb = _lower_bound(lbraw_ref, layer) if layer > 0 else None

    def project(r, _):
        h = _rms(x_ref[0, _row_block(r), :], gpre_ref[...]).astype(_BF)
        return jnp.dot(h, w_ref[...], preferred_element_type=_F32)

    def gates(r, p):
        return _hgrn_gates(p[:, 0:512], p[:, 512:1024], p[:, 1024:1536], _row_block(r), lb, scan_scratch)

    def cumulate(r, g):
        _hgrn_cumulative(g, _row_block(r), scan_scratch, fwd=False)

    def emit(rows, lanes, o):
        o_ref[0, rows, lanes] = o

    _skewed(x_ref.shape[1] // ROW_BLOCK, project, gates, cumulate)
    _hgrn_scan(scan_scratch, st_s, False, emit)


def _mix_kernel(layer, x_ref, obwd_ref, gpre_ref, w_ref, vlng_ref, vlnb_ref, wsp_ref, bsp_ref,
                lbraw_ref, onorm_ref, wout_ref, gpost_ref, out_ref, *scratch):
    scan_scratch, (gate_s, st_s, cat_s) = scratch[:-3], scratch[-3:]
    _zero_state_at_sequence_start(st_s)
    lb = _lower_bound(lbraw_ref, layer) if layer > 0 else None
    n_blocks = x_ref.shape[1] // ROW_BLOCK

    def project(r, _):
        h = _rms(x_ref[0, _row_block(r), :], gpre_ref[...]).astype(_BF)
        return jnp.dot(h, w_ref[...], preferred_element_type=_F32)

    def elementwise(r, p):
        rows = _row_block(r)
        u = _gelu(p[:, 0:512])
        v = _gelu(p[:, 512:1024])
        vc = v - jnp.mean(v, axis=-1, keepdims=True)
        v = vc * lax.rsqrt(jnp.mean(vc * vc, axis=-1, keepdims=True) + EPS) * vlng_ref[...] + vlnb_ref[...]
        g = _hgrn_gates(p[:, 1024:1536], p[:, 1536:2048], p[:, 2048:2560], rows, lb, scan_scratch)
        gate_s[rows, :] = _silu(p[:, 2560:3072])
        return u, v.astype(_BF), g

    def dependent_matmuls(r, uvg):
        rows = _row_block(r)
        u, v, g = uvg
        for n in range(ROW_BLOCK // A_CHUNK):
            chunk = slice(n * A_CHUNK, (n + 1) * A_CHUNK)
            for gi in range(A_GROUPS):
                lanes = slice(gi * A_GROUP_DIM, (gi + 1) * A_GROUP_DIM)
                sv = jnp.dot(wsp_ref[gi], v[chunk, lanes], preferred_element_type=_F32) + bsp_ref[gi]
                cat_s[rows.start + chunk.start:rows.start + chunk.stop, lanes] = (
                    u[chunk, lanes] * sv).astype(_BF)
        _hgrn_cumulative(g, rows, scan_scratch, fwd=True)

    def emit(rows, lanes, o):
        o_sum = o + obwd_ref[0, rows, lanes]
        cat_s[rows, A_WIDTH + lanes.start:A_WIDTH + lanes.stop] = (
            _rms(o_sum, onorm_ref[...]) * gate_s[rows, lanes]).astype(_BF)

    def out_proj(r, _):
        return jnp.dot(cat_s[_row_block(r), :], wout_ref[...], preferred_element_type=_F32)

    def residual(r, y):
        rows = _row_block(r)
        out_ref[0, rows, :] = x_ref[0, rows, :] + _rms(y, gpost_ref[...])

    _skewed(n_blocks, project, elementwise, dependent_matmuls)
    _hgrn_scan(scan_scratch, st_s, True, emit)
    _skewed(n_blocks, out_proj, residual)


def _kv_kernel(mem_ref, g_ref, w_ref, k_ref, v_ref):
    m = _rms(mem_ref[0], g_ref[...])
    kv = _dot(m, w_ref[...])
    k_ref[0] = kv[:, :D_MODEL].astype(_BF)
    v_ref[0] = kv[:, D_MODEL:].astype(_BF)


def _xattn_kernel(x_ref, k_ref, v_ref, gpre_ref, wq_ref, wo_ref, gpost_ref, out_ref, cat_s):
    head_lanes = lambda hd: slice(hd * X_HEAD_DIM, (hd + 1) * X_HEAD_DIM)

    def query(r, _):
        h = _rms(x_ref[0, _row_block(r), :], gpre_ref[...])
        return _dot(h, wq_ref[...]).astype(_BF)

    def scores(r, q):
        return [_dot_tb(q[:, head_lanes(hd)], k_ref[0, :, head_lanes(hd)]) for hd in range(X_HEADS)]

    def softmax(r, s_heads):
        p_heads = []
        for s in s_heads:
            s = s * (X_HEAD_DIM ** -0.5)
            e = jnp.exp(s - jnp.max(s, axis=-1, keepdims=True))
            p_heads.append((e / jnp.sum(e, axis=-1, keepdims=True)).astype(_BF))
        return p_heads

    def values(r, p_heads):
        for hd, p in enumerate(p_heads):
            cat_s[_row_block(r), head_lanes(hd)] = _dot(p, v_ref[0, :, head_lanes(hd)]).astype(_BF)

    def out_proj(r, _):
        return jnp.dot(cat_s[_row_block(r), :], wo_ref[...], preferred_element_type=_F32)

    def residual(r, y):
        rows = _row_block(r)
        out_ref[0, rows, :] = x_ref[0, rows, :] + _rms(y, gpost_ref[...])

    _skewed(x_ref.shape[1] // ROW_BLOCK, query, scores, softmax, values, out_proj, residual)


def _ffn_kernel(x_ref, gpre_ref, wgu_ref, wdown_ref, gpost_ref, out_ref):
    x = x_ref[0]
    h = _rms(x, gpre_ref[...]).astype(_BF)
    y = jnp.zeros((x.shape[0], D_MODEL), _F32)
    for c in range(D_FF // FF_CHUNK):
        cols = slice(c * FF_CHUNK, (c + 1) * FF_CHUNK)
        gate = jnp.dot(h, wgu_ref[:, cols], preferred_element_type=_F32)
        up = jnp.dot(h, wgu_ref[:, D_FF + c * FF_CHUNK:D_FF + (c + 1) * FF_CHUNK],
                     preferred_element_type=_F32)
        y = y + _dot(_silu(gate) * up, wdown_ref[cols, :])
    out_ref[0] = x + _rms(y, gpost_ref[...])


def _params():
    return pltpu.CompilerParams(dimension_semantics=("arbitrary", "arbitrary"),
                                vmem_limit_bytes=VMEM_LIMIT_BYTES)


def _const_spec(shape):
    return pl.BlockSpec(shape, lambda b, t: (0,) * len(shape), pipeline_mode=pl.Buffered(1))


def _tile_spec(width, rev_tiles=None):
    if rev_tiles is None:
        return pl.BlockSpec((1, SEQ_TILE, width), lambda b, t: (b, t, 0))
    return pl.BlockSpec((1, SEQ_TILE, width), lambda b, t: (b, rev_tiles - 1 - t, 0))


def _scan_scratch():
    tile = lambda dt: pltpu.VMEM((SEQ_TILE, B_WIDTH), dt)
    chunk = lambda: pltpu.VMEM((SCAN_CHUNK, B_WIDTH), _F32)
    return [tile(_F32), tile(_F32), tile(_BF), tile(_F32), tile(_F32), chunk(), chunk(),
            tile(_BF), tile(_BF),
            pltpu.VMEM((B_HEADS, SEQ_TILE, SCAN_CHUNK), _BF),
            pltpu.VMEM((SEQ_TILE // SCAN_CHUNK, B_HEADS, B_DK, B_DK), _F32)]


def _state_scratch():
    return pltpu.VMEM((B_HEADS, B_DK, B_DK), _F32)


def _bwd_call(x, layer, gpre, w_bwd, lbraw):
    bn, ln, _ = x.shape
    nt = ln // SEQ_TILE
    return pl.pallas_call(
        functools.partial(_bwd_kernel, layer),
        grid=(bn, nt),
        in_specs=[_tile_spec(D_MODEL, nt), _const_spec(gpre.shape), _const_spec(w_bwd.shape),
                  _const_spec(lbraw.shape)],
        out_specs=_tile_spec(B_WIDTH, nt),
        out_shape=jax.ShapeDtypeStruct((bn, ln, B_WIDTH), _F32),
        scratch_shapes=_scan_scratch() + [_state_scratch()],
        compiler_params=_params(),
        name=f"hgrn_bwd_l{layer}",
    )(x, gpre, w_bwd, lbraw)


def _mix_call(x, obwd, layer, gpre, w_mix, vlng, vlnb, wsp, bsp, lbraw, onorm, wout, gpost):
    bn, ln, _ = x.shape
    nt = ln // SEQ_TILE
    consts = (gpre, w_mix, vlng, vlnb, wsp, bsp, lbraw, onorm, wout, gpost)
    return pl.pallas_call(
        functools.partial(_mix_kernel, layer),
        grid=(bn, nt),
        in_specs=[_tile_spec(D_MODEL), _tile_spec(B_WIDTH)] + [_const_spec(c.shape) for c in consts],
        out_specs=_tile_spec(D_MODEL),
        out_shape=jax.ShapeDtypeStruct(x.shape, _F32),
        scratch_shapes=_scan_scratch() + [pltpu.VMEM((SEQ_TILE, B_WIDTH), _F32), _state_scratch(),
                                          pltpu.VMEM((SEQ_TILE, D_MODEL), _BF)],
        compiler_params=_params(),
        name=f"token_mix_l{layer}",
    )(x, obwd, *consts)


def _kv_call(mem, layer, g, w):
    bn = mem.shape[0]
    spec = pl.BlockSpec((1, MEM_TOKENS, D_MODEL), lambda b, t: (b, 0, 0))
    return pl.pallas_call(
        _kv_kernel,
        grid=(bn, 1),
        in_specs=[spec, _const_spec(g.shape), _const_spec(w.shape)],
        out_specs=[spec, spec],
        out_shape=[jax.ShapeDtypeStruct(mem.shape, _BF)] * 2,
        compiler_params=_params(),
        name=f"mem_kv_l{layer}",
    )(mem, g, w)


def _xattn_call(x, k, v, layer, gpre, wq, wo, gpost):
    bn, ln, _ = x.shape
    nt = ln // SEQ_TILE
    mem_spec = pl.BlockSpec((1, MEM_TOKENS, D_MODEL), lambda b, t: (b, 0, 0))
    consts = (gpre, wq, wo, gpost)
    return pl.pallas_call(
        _xattn_kernel,
        grid=(bn, nt),
        in_specs=[_tile_spec(D_MODEL), mem_spec, mem_spec] + [_const_spec(c.shape) for c in consts],
        out_specs=_tile_spec(D_MODEL),
        out_shape=jax.ShapeDtypeStruct(x.shape, _F32),
        scratch_shapes=[pltpu.VMEM((SEQ_TILE, D_MODEL), _BF)],
        compiler_params=_params(),
        name=f"mem_xattn_l{layer}",
    )(x, k, v, *consts)


def _ffn_call(x, layer, gpre, wgu, wdown, gpost):
    bn, ln, _ = x.shape
    nt = ln // SEQ_TILE
    consts = (gpre, wgu, wdown, gpost)
    return pl.pallas_call(
        _ffn_kernel,
        grid=(bn, nt),
        in_specs=[_tile_spec(D_MODEL)] + [_const_spec(c.shape) for c in consts],
        out_specs=_tile_spec(D_MODEL),
        out_shape=jax.ShapeDtypeStruct(x.shape, _F32),
        compiler_params=_params(),
        name=f"swiglu_ffn_l{layer}",
    )(x, *consts)


def _layer_weights(l, p):
    w_in = p['w_in'][l].astype(_BF)
    cols = lambda a, b: w_in[:, a:b]
    row = lambda a: a[l][None, :]
    return dict(
        w_mix=jnp.concatenate([cols(0, 2048), cols(2560, 3584)], axis=1),
        w_bwd=jnp.concatenate([cols(1024, 1536), cols(2048, 3072)], axis=1),
        pre_mix_g=row(p['pre_mix_g']), v_ln_g=row(p['v_ln_g']), v_ln_b=row(p['v_ln_b']),
        w_spatial=p['w_spatial'][l].astype(_BF),
        b_spatial=jnp.broadcast_to(p['b_spatial'][l][:, :, None], (A_GROUPS, A_CHUNK, A_GROUP_DIM)),
        lb_fwd=p['lb_raw'][0], lb_bwd=p['lb_raw'][1],
        onorm_g=row(p['onorm_g']), w_out=p['w_out'][l].astype(_BF), post_mix_g=row(p['post_mix_g']),
        pre_x_g=row(p['pre_x_g']), mem_norm_g=row(p['mem_norm_g']),
        w_xq=p['w_xq'][l].astype(_BF), w_xkv=p['w_xkv'][l].astype(_BF), w_xo=p['w_xo'][l].astype(_BF),
        post_x_g=row(p['post_x_g']), pre_ffn_g=row(p['pre_ffn_g']),
        w_gu=p['w_gu'][l].astype(_BF), w_down=p['w_down'][l].astype(_BF), post_ffn_g=row(p['post_ffn_g']),
    )


def _trunk(x, mem, layers):
    assert x.shape[1] % SEQ_TILE == 0 and x.shape[2] == D_MODEL
    assert SEQ_TILE % ROW_BLOCK == 0 and ROW_BLOCK % SCAN_CHUNK == 0 and ROW_BLOCK % A_CHUNK == 0
    for l, w in enumerate(layers):
        obwd = _bwd_call(x, l, w['pre_mix_g'], w['w_bwd'], w['lb_bwd'])
        x = _mix_call(x, obwd, l, w['pre_mix_g'], w['w_mix'], w['v_ln_g'], w['v_ln_b'], w['w_spatial'],
                      w['b_spatial'], w['lb_fwd'], w['onorm_g'], w['w_out'], w['post_mix_g'])
        k, v = _kv_call(mem, l, w['mem_norm_g'], w['w_xkv'])
        x = _xattn_call(x, k, v, l, w['pre_x_g'], w['w_xq'], w['w_xo'], w['post_x_g'])
        x = _ffn_call(x, l, w['pre_ffn_g'], w['w_gu'], w['w_down'], w['post_ffn_g'])
    return x


def kernel(x_prompt, x_sample, mem_prompt, mem_sample, pre_mix_g, w_in, v_ln_g, v_ln_b, w_spatial, b_spatial, lb_raw, onorm_g, w_out, post_mix_g, pre_x_g, mem_norm_g, w_xq, w_xkv, w_xo, post_x_g, pre_ffn_g, w_gu, w_down, post_ffn_g):
    p = dict(pre_mix_g=pre_mix_g, w_in=w_in, v_ln_g=v_ln_g, v_ln_b=v_ln_b,
             w_spatial=w_spatial, b_spatial=b_spatial, lb_raw=lb_raw, onorm_g=onorm_g,
             w_out=w_out, post_mix_g=post_mix_g, pre_x_g=pre_x_g, mem_norm_g=mem_norm_g,
             w_xq=w_xq, w_xkv=w_xkv, w_xo=w_xo, post_x_g=post_x_g, pre_ffn_g=pre_ffn_g,
             w_gu=w_gu, w_down=w_down, post_ffn_g=post_ffn_g)
    layers = [_layer_weights(l, p) for l in range(w_in.shape[0])]
    return (_trunk(x_prompt, mem_prompt, layers), _trunk(x_sample, mem_sample, layers))
```

```python
import functools

import jax
import jax.numpy as jnp
from jax import lax
from jax.experimental import pallas as pl
from jax.experimental.pallas import tpu as pltpu

D_MODEL = 1024
A_WIDTH = 512
A_GROUPS = 4
A_GROUP_DIM = 128
A_CHUNK = 128
B_WIDTH = 512
B_HEADS = 4
B_DK = 128
MEM_TOKENS = 256
X_HEADS = 4
X_HEAD_DIM = 256
D_FF = 2816
EPS = 1e-6

SEQ_TILE = 1024
ROW_BLOCK = 256
SCAN_CHUNK = 64
SUB_BLOCK = 16
FF_CHUNK = 1408
FACTORED_DECAY_LIMIT = 55.0
VMEM_LIMIT_BYTES = 60 * 1024 * 1024

_BF = jnp.bfloat16
_F32 = jnp.float32


def _dot(a, b):
    return jnp.dot(a.astype(_BF), b.astype(_BF), preferred_element_type=_F32)


def _dot_tb(a, b):
    return lax.dot_general(a.astype(_BF), b.astype(_BF), (((1,), (1,)), ((), ())),
                           preferred_element_type=_F32)


def _dot_ta(a, b):
    return lax.dot_general(a.astype(_BF), b.astype(_BF), (((0,), (0,)), ((), ())),
                           preferred_element_type=_F32)


def _rms(x, g):
    return x * lax.rsqrt(jnp.mean(x * x, axis=-1, keepdims=True) + EPS) * g


def _gelu(x):
    return 0.5 * x * (1.0 + lax.erf(x * (2.0 ** -0.5)))


def _silu(x):
    return x * jax.nn.sigmoid(x)


def _log_sigmoid(x):
    return jnp.minimum(x, 0.0) - jnp.log1p(jnp.exp(-jnp.abs(x)))


def _lower_bound(lbraw_ref, layer):
    depth = lbraw_ref.shape[0]
    rows = [lbraw_ref[j:j + 1, :] for j in range(depth)]
    m = functools.reduce(jnp.maximum, rows)
    e = [jnp.exp(r - m) for r in rows]
    denom = functools.reduce(lambda a, b: a + b, e)
    return functools.reduce(lambda a, b: a + b, e[1:layer + 1]) / denom


def _skewed(n, *stages):
    vals = {}
    for i in range(n + len(stages) - 1):
        for k, stage in enumerate(stages):
            r = i - k
            if 0 <= r < n:
                vals[(k, r)] = stage(r, vals.pop((k - 1, r), None))


def _row_block(r):
    return slice(r * ROW_BLOCK, (r + 1) * ROW_BLOCK)


def _split3(x):
    hi = x.astype(_BF)
    r1 = x - hi.astype(_F32)
    mid = r1.astype(_BF)
    lo = (r1 - mid.astype(_F32)).astype(_BF)
    return hi, mid, lo


def _cumulative(g, fwd):
    n = g.shape[0]
    row = lax.broadcasted_iota(jnp.int32, (n, n), 0)
    col = lax.broadcasted_iota(jnp.int32, (n, n), 1)
    tri = jnp.where(col <= row if fwd else col >= row, 1.0, 0.0).astype(_BF)
    g_hi, g_mid, g_lo = _split3(g)
    return (jnp.dot(tri, g_hi, preferred_element_type=_F32)
            + jnp.dot(tri, g_mid, preferred_element_type=_F32)
            + jnp.dot(tri, g_lo, preferred_element_type=_F32))


def _hgrn_gates(f, rows, lb, scan_scratch):
    k_s, g_s = scan_scratch[1], scan_scratch[3]
    if lb is None:
        g = _log_sigmoid(f)
        k_s[rows, :] = jax.nn.sigmoid(-f)
    else:
        forget = lb + (1.0 - lb) * jax.nn.sigmoid(f)
        g = jnp.log(forget)
        k_s[rows, :] = 1.0 - forget
    g_s[rows, :] = g
    return g


def _hgrn_cumulative(g, rows, scan_scratch, fwd):
    b_s = scan_scratch[4]
    for c in range(ROW_BLOCK // SCAN_CHUNK):
        lo = c * SCAN_CHUNK
        b_s[rows.start + lo:rows.start + lo + SCAN_CHUNK, :] = _cumulative(g[lo:lo + SCAN_CHUNK], fwd)


def _hgrn_scan(scan_scratch, st_s, fwd, emit):
    q_s, k_s, v_s, g_s, b_s, kc_s, bc_s, x_s, y_s, a_s, u_s = scan_scratch
    tile = q_s.shape[0]
    n_chunks = tile // SCAN_CHUNK
    n_sub = SCAN_CHUNK // SUB_BLOCK
    C = SCAN_CHUNK

    row = lax.broadcasted_iota(jnp.int32, (C, C), 0)
    col = lax.broadcasted_iota(jnp.int32, (C, C), 1)
    sub_row = lax.broadcasted_iota(jnp.int32, (SUB_BLOCK, C), 0)
    sub_col = lax.broadcasted_iota(jnp.int32, (SUB_BLOCK, C), 1)
    mid = C // 2 - 1 if fwd else C // 2
    last = C - 1 if fwd else 0

    def chunk_body(c, carry):
        idx = c if fwd else n_chunks - 1 - c
        base = pl.multiple_of(idx * C, C)
        rows = pl.ds(base, C)
        bc_s[...] = b_s[rows, :]
        kc_s[...] = k_s[rows, :]
        for hd in range(B_HEADS):
            lanes = slice(hd * B_DK, (hd + 1) * B_DK)
            q = q_s[rows, lanes]
            k = kc_s[:, lanes]
            v = v_s[rows, lanes]
            b = bc_s[:, lanes]
            st = st_s[hd]
            b_tot = b[last:last + 1]
            o = _dot_tb(q * jnp.exp(b), st)
            kd = k * jnp.exp(b_tot - b)
            st_s[hd] = st * jnp.exp(b_tot) + _dot_ta(v, kd)

            a_rows = []
            for i in range(n_sub):
                lo, hi = i * SUB_BLOCK, (i + 1) * SUB_BLOCK
                qi, bi = q[lo:hi], b[lo:hi]
                if fwd and i > 0:
                    r = b[lo - 1:lo]
                    ke = k[:lo] * jnp.exp(r - b[:lo])
                    ke = jnp.concatenate([ke, jnp.zeros((C - lo, B_DK), _F32)], axis=0)
                    a_i = _dot_tb(qi * jnp.exp(bi - r), ke)
                elif (not fwd) and i < n_sub - 1:
                    r = b[hi:hi + 1]
                    ke = k[hi:] * jnp.exp(r - b[hi:])
                    ke = jnp.concatenate([jnp.zeros((hi, B_DK), _F32), ke], axis=0)
                    a_i = _dot_tb(qi * jnp.exp(bi - r), ke)
                else:
                    a_i = jnp.zeros((SUB_BLOCK, C), _F32)
                for s in range(SUB_BLOCK):
                    j = lo + s
                    k_row = kc_s[j:j + 1, lanes]
                    b_row = bc_s[j:j + 1, lanes]
                    p = qi * k_row * jnp.exp(jnp.minimum(bi - b_row, 0.0))
                    a_i = jnp.where(sub_col == j, jnp.sum(p, axis=-1, keepdims=True), a_i)
                local = sub_col - lo
                if fwd:
                    keep = (sub_col < lo) | ((sub_col < hi) & (sub_row >= local))
                else:
                    keep = (sub_col >= hi) | ((sub_col >= lo) & (sub_row <= local))
                a_rows.append(jnp.where(keep, a_i, 0.0))
            a = jnp.concatenate(a_rows, axis=0)
            emit(rows, lanes, o + _dot(a, v))
        return carry

    def factored_tile():
        causal = (col <= row) if fwd else (col >= row)
        order = list(range(n_chunks)) if fwd else list(reversed(range(n_chunks)))
        chunk_rows = lambda c: slice(c * C, (c + 1) * C)
        head_lanes = lambda hd: slice(hd * B_DK, (hd + 1) * B_DK)

        def scale(hd, _):
            lanes = head_lanes(hd)
            for c in order:
                rows = chunk_rows(c)
                b = b_s[rows, lanes]
                e = jnp.exp(b - b[mid:mid + 1])
                x_s[rows, lanes] = (q_s[rows, lanes] * e).astype(_BF)
                y_s[rows, lanes] = (k_s[rows, lanes] / e).astype(_BF)

        def products(hd, _):
            lanes = head_lanes(hd)
            for c in order:
                rows = chunk_rows(c)
                yh = y_s[rows, lanes]
                a_s[hd, rows, :] = jnp.where(causal, _dot_tb(x_s[rows, lanes], yh), 0.0).astype(_BF)
                u_s[c, hd] = _dot_ta(v_s[rows, lanes], yh)

        def states(hd, _):
            lanes = head_lanes(hd)
            st = st_s[hd]
            for c in order:
                rho = b_s[c * C + mid:c * C + mid + 1, lanes]
                b_tot = b_s[c * C + last:c * C + last + 1, lanes]
                u = u_s[c, hd]
                u_s[c, hd] = st * jnp.exp(rho)
                st = st * jnp.exp(b_tot) + u * jnp.exp(b_tot - rho)
            st_s[hd] = st

        def outputs(hd, _):
            lanes = head_lanes(hd)
            for c in order:
                rows = chunk_rows(c)
                emit(rows, lanes, _dot_tb(x_s[rows, lanes], u_s[c, hd])
                     + _dot(a_s[hd, rows, :], v_s[rows, lanes]))

        _skewed(B_HEADS, scale, products, states, outputs)

    worst = jnp.zeros((1, B_WIDTH), _F32)
    for c in range(n_chunks):
        b_mid = b_s[c * C + mid:c * C + mid + 1, :]
        b_tot = b_s[c * C + last:c * C + last + 1, :]
        worst = jnp.maximum(worst, jnp.maximum(-b_mid, b_mid - b_tot))
    factored_ok = jnp.max(worst) < FACTORED_DECAY_LIMIT

    @pl.when(factored_ok)
    def _():
        factored_tile()

    @pl.when(jnp.logical_not(factored_ok))
    def _():
        lax.fori_loop(0, n_chunks, chunk_body, 0)


def _zero_state_at_sequence_start(st_s):
    @pl.when(pl.program_id(1) == 0)
    def _():
        st_s[...] = jnp.zeros_like(st_s)


def _bwd_kernel(layer, x_ref, gpre_ref, w_ref, lbraw_ref, o_ref, q_ref, v_ref, *scratch):
    (k_s, g_s), rest, st_s = scratch[:2], scratch[2:-1], scratch[-1]
    scan_scratch = (q_ref.at[0], k_s, v_ref.at[0], g_s) + tuple(rest)
    _zero_state_at_sequence_start(st_s)
    lb = _lower_bound(lbraw_ref, layer) if layer > 0 else None

    def project(r, _):
        h = _rms(x_ref[0, _row_block(r), :], gpre_ref[...]).astype(_BF)
        return jnp.dot(h, w_ref[...], preferred_element_type=_F32)

    def gates(r, p):
        rows = _row_block(r)
        q_ref[0, rows, :] = _silu(p[:, 0:512]) * (B_DK ** -0.5)
        v_ref[0, rows, :] = p[:, 1024:1536].astype(_BF)
        return _hgrn_gates(p[:, 512:1024], rows, lb, scan_scratch)

    def cumulate(r, g):
        _hgrn_cumulative(g, _row_block(r), scan_scratch, fwd=False)

    def emit(rows, lanes, o):
        o_ref[0, rows, lanes] = o

    _skewed(x_ref.shape[1] // ROW_BLOCK, project, gates, cumulate)
    _hgrn_scan(scan_scratch, st_s, False, emit)


def _mix_kernel(layer, x_ref, obwd_ref, q_ref, v_ref, gpre_ref, w_ref, vlng_ref, vlnb_ref, wsp_ref,
                bsp_ref, lbraw_ref, onorm_ref, wout_ref, gpost_ref, out_ref, *scratch):
    (k_s, g_s), rest, (gate_s, st_s, cat_s) = scratch[:2], scratch[2:-3], scratch[-3:]
    scan_scratch = (q_ref.at[0], k_s, v_ref.at[0], g_s) + tuple(rest)
    _zero_state_at_sequence_start(st_s)
    lb = _lower_bound(lbraw_ref, layer) if layer > 0 else None
    n_blocks = x_ref.shape[1] // ROW_BLOCK

    def project(r, _):
        h = _rms(x_ref[0, _row_block(r), :], gpre_ref[...]).astype(_BF)
        return jnp.dot(h, w_ref[...], preferred_element_type=_F32)

    def elementwise(r, p):
        rows = _row_block(r)
        u = _gelu(p[:, 0:512])
        v = _gelu(p[:, 512:1024])
        vc = v - jnp.mean(v, axis=-1, keepdims=True)
        v = vc * lax.rsqrt(jnp.mean(vc * vc, axis=-1, keepdims=True) + EPS) * vlng_ref[...] + vlnb_ref[...]
        g = _hgrn_gates(p[:, 1024:1536], rows, lb, scan_scratch)
        gate_s[rows, :] = _silu(p[:, 1536:2048])
        return u, v.astype(_BF), g

    def dependent_matmuls(r, uvg):
        rows = _row_block(r)
        u, v, g = uvg
        for n in range(ROW_BLOCK // A_CHUNK):
            chunk = slice(n * A_CHUNK, (n + 1) * A_CHUNK)
            for gi in range(A_GROUPS):
                lanes = slice(gi * A_GROUP_DIM, (gi + 1) * A_GROUP_DIM)
                sv = jnp.dot(wsp_ref[gi], v[chunk, lanes], preferred_element_type=_F32) + bsp_ref[gi]
                cat_s[rows.start + chunk.start:rows.start + chunk.stop, lanes] = (
                    u[chunk, lanes] * sv).astype(_BF)
        _hgrn_cumulative(g, rows, scan_scratch, fwd=True)

    def emit(rows, lanes, o):
        o_sum = o + obwd_ref[0, rows, lanes]
        cat_s[rows, A_WIDTH + lanes.start:A_WIDTH + lanes.stop] = (
            _rms(o_sum, onorm_ref[...]) * gate_s[rows, lanes]).astype(_BF)

    def out_proj(r, _):
        return jnp.dot(cat_s[_row_block(r), :], wout_ref[...], preferred_element_type=_F32)

    def residual(r, y):
        rows = _row_block(r)
        out_ref[0, rows, :] = x_ref[0, rows, :] + _rms(y, gpost_ref[...])

    _skewed(n_blocks, project, elementwise, dependent_matmuls)
    _hgrn_scan(scan_scratch, st_s, True, emit)
    _skewed(n_blocks, out_proj, residual)


def _kv_kernel(mem_ref, g_ref, w_ref, k_ref, v_ref):
    m = _rms(mem_ref[0], g_ref[...])
    kv = _dot(m, w_ref[...])
    k_ref[0] = kv[:, :D_MODEL].astype(_BF)
    v_ref[0] = kv[:, D_MODEL:].astype(_BF)


def _xattn_kernel(x_ref, k_ref, v_ref, gpre_ref, wq_ref, wo_ref, gpost_ref, out_ref, cat_s):
    head_lanes = lambda hd: slice(hd * X_HEAD_DIM, (hd + 1) * X_HEAD_DIM)

    def query(r, _):
        h = _rms(x_ref[0, _row_block(r), :], gpre_ref[...])
        return _dot(h, wq_ref[...]).astype(_BF)

    def scores(r, q):
        return [_dot_tb(q[:, head_lanes(hd)], k_ref[0, :, head_lanes(hd)]) for hd in range(X_HEADS)]

    def softmax(r, s_heads):
        p_heads = []
        for s in s_heads:
            s = s * (X_HEAD_DIM ** -0.5)
            e = jnp.exp(s - jnp.max(s, axis=-1, keepdims=True))
            p_heads.append((e / jnp.sum(e, axis=-1, keepdims=True)).astype(_BF))
        return p_heads

    def values(r, p_heads):
        for hd, p in enumerate(p_heads):
            cat_s[_row_block(r), head_lanes(hd)] = _dot(p, v_ref[0, :, head_lanes(hd)]).astype(_BF)

    def out_proj(r, _):
        return jnp.dot(cat_s[_row_block(r), :], wo_ref[...], preferred_element_type=_F32)

    def residual(r, y):
        rows = _row_block(r)
        out_ref[0, rows, :] = x_ref[0, rows, :] + _rms(y, gpost_ref[...])

    _skewed(x_ref.shape[1] // ROW_BLOCK, query, scores, softmax, values, out_proj, residual)


def _ffn_kernel(x_ref, gpre_ref, wgu_ref, wdown_ref, gpost_ref, out_ref):
    x = x_ref[0]
    h = _rms(x, gpre_ref[...]).astype(_BF)
    y = jnp.zeros((x.shape[0], D_MODEL), _F32)
    for c in range(D_FF // FF_CHUNK):
        cols = slice(c * FF_CHUNK, (c + 1) * FF_CHUNK)
        gate = jnp.dot(h, wgu_ref[:, cols], preferred_element_type=_F32)
        up = jnp.dot(h, wgu_ref[:, D_FF + c * FF_CHUNK:D_FF + (c + 1) * FF_CHUNK],
                     preferred_element_type=_F32)
        y = y + _dot(_silu(gate) * up, wdown_ref[cols, :])
    out_ref[0] = x + _rms(y, gpost_ref[...])


def _params():
    return pltpu.CompilerParams(dimension_semantics=("arbitrary", "arbitrary"),
                                vmem_limit_bytes=VMEM_LIMIT_BYTES)


def _const_spec(shape):
    return pl.BlockSpec(shape, lambda b, t: (0,) * len(shape), pipeline_mode=pl.Buffered(1))


def _tile_spec(width, rev_tiles=None):
    if rev_tiles is None:
        return pl.BlockSpec((1, SEQ_TILE, width), lambda b, t: (b, t, 0))
    return pl.BlockSpec((1, SEQ_TILE, width), lambda b, t: (b, rev_tiles - 1 - t, 0))


def _scan_scratch():
    tile = lambda dt: pltpu.VMEM((SEQ_TILE, B_WIDTH), dt)
    chunk = lambda: pltpu.VMEM((SCAN_CHUNK, B_WIDTH), _F32)
    return [tile(_F32), tile(_F32), tile(_F32), chunk(), chunk(), tile(_BF), tile(_BF),
            pltpu.VMEM((B_HEADS, SEQ_TILE, SCAN_CHUNK), _BF),
            pltpu.VMEM((SEQ_TILE // SCAN_CHUNK, B_HEADS, B_DK, B_DK), _F32)]


def _state_scratch():
    return pltpu.VMEM((B_HEADS, B_DK, B_DK), _F32)


def _bwd_call(x, layer, gpre, w_bwd, lbraw):
    bn, ln, _ = x.shape
    nt = ln // SEQ_TILE
    return pl.pallas_call(
        functools.partial(_bwd_kernel, layer),
        grid=(bn, nt),
        in_specs=[_tile_spec(D_MODEL, nt), _const_spec(gpre.shape), _const_spec(w_bwd.shape),
                  _const_spec(lbraw.shape)],
        out_specs=[_tile_spec(B_WIDTH, nt)] * 3,
        out_shape=[jax.ShapeDtypeStruct((bn, ln, B_WIDTH), _F32),
                   jax.ShapeDtypeStruct((bn, ln, B_WIDTH), _F32),
                   jax.ShapeDtypeStruct((bn, ln, B_WIDTH), _BF)],
        scratch_shapes=_scan_scratch() + [_state_scratch()],
        compiler_params=_params(),
        name=f"hgrn_bwd_l{layer}",
    )(x, gpre, w_bwd, lbraw)


def _mix_call(x, obwd, q, v, layer, gpre, w_mix, vlng, vlnb, wsp, bsp, lbraw, onorm, wout, gpost):
    bn, ln, _ = x.shape
    nt = ln // SEQ_TILE
    consts = (gpre, w_mix, vlng, vlnb, wsp, bsp, lbraw, onorm, wout, gpost)
    return pl.pallas_call(
        functools.partial(_mix_kernel, layer),
        grid=(bn, nt),
        in_specs=[_tile_spec(D_MODEL)] + [_tile_spec(B_WIDTH)] * 3 + [_const_spec(c.shape) for c in consts],
        out_specs=_tile_spec(D_MODEL),
        out_shape=jax.ShapeDtypeStruct(x.shape, _F32),
        scratch_shapes=_scan_scratch() + [pltpu.VMEM((SEQ_TILE, B_WIDTH), _F32), _state_scratch(),
                                          pltpu.VMEM((SEQ_TILE, D_MODEL), _BF)],
        compiler_params=_params(),
        name=f"token_mix_l{layer}",
    )(x, obwd, q, v, *consts)


def _kv_call(mem, layer, g, w):
    bn = mem.shape[0]
    spec = pl.BlockSpec((1, MEM_TOKENS, D_MODEL), lambda b, t: (b, 0, 0))
    return pl.pallas_call(
        _kv_kernel,
        grid=(bn, 1),
        in_specs=[spec, _const_spec(g.shape), _const_spec(w.shape)],
        out_specs=[spec, spec],
        out_shape=[jax.ShapeDtypeStruct(mem.shape, _BF)] * 2,
        compiler_params=_params(),
        name=f"mem_kv_l{layer}",
    )(mem, g, w)


def _xattn_call(x, k, v, layer, gpre, wq, wo, gpost):
    bn, ln, _ = x.shape
    nt = ln // SEQ_TILE
    mem_spec = pl.BlockSpec((1, MEM_TOKENS, D_MODEL), lambda b, t: (b, 0, 0))
    consts = (gpre, wq, wo, gpost)
    return pl.pallas_call(
        _xattn_kernel,
        grid=(bn, nt),
        in_specs=[_tile_spec(D_MODEL), mem_spec, mem_spec] + [_const_spec(c.shape) for c in consts],
        out_specs=_tile_spec(D_MODEL),
        out_shape=jax.ShapeDtypeStruct(x.shape, _F32),
        scratch_shapes=[pltpu.VMEM((SEQ_TILE, D_MODEL), _BF)],
        compiler_params=_params(),
        name=f"mem_xattn_l{layer}",
    )(x, k, v, *consts)


def _ffn_call(x, layer, gpre, wgu, wdown, gpost):
    bn, ln, _ = x.shape
    nt = ln // SEQ_TILE
    consts = (gpre, wgu, wdown, gpost)
    return pl.pallas_call(
        _ffn_kernel,
        grid=(bn, nt),
        in_specs=[_tile_spec(D_MODEL)] + [_const_spec(c.shape) for c in consts],
        out_specs=_tile_spec(D_MODEL),
        out_shape=jax.ShapeDtypeStruct(x.shape, _F32),
        compiler_params=_params(),
        name=f"swiglu_ffn_l{layer}",
    )(x, *consts)


def _layer_weights(l, p):
    w_in = p['w_in'][l].astype(_BF)
    cols = lambda a, b: w_in[:, a:b]
    row = lambda a: a[l][None, :]
    return dict(
        w_mix=jnp.concatenate([cols(0, 1024), cols(1536, 2048), cols(3072, 3584)], axis=1),
        w_bwd=jnp.concatenate([cols(1024, 1536), cols(2048, 3072)], axis=1),
        pre_mix_g=row(p['pre_mix_g']), v_ln_g=row(p['v_ln_g']), v_ln_b=row(p['v_ln_b']),
        w_spatial=p['w_spatial'][l].astype(_BF),
        b_spatial=jnp.broadcast_to(p['b_spatial'][l][:, :, None], (A_GROUPS, A_CHUNK, A_GROUP_DIM)),
        lb_fwd=p['lb_raw'][0], lb_bwd=p['lb_raw'][1],
        onorm_g=row(p['onorm_g']), w_out=p['w_out'][l].astype(_BF), post_mix_g=row(p['post_mix_g']),
        pre_x_g=row(p['pre_x_g']), mem_norm_g=row(p['mem_norm_g']),
        w_xq=p['w_xq'][l].astype(_BF), w_xkv=p['w_xkv'][l].astype(_BF), w_xo=p['w_xo'][l].astype(_BF),
        post_x_g=row(p['post_x_g']), pre_ffn_g=row(p['pre_ffn_g']),
        w_gu=p['w_gu'][l].astype(_BF), w_down=p['w_down'][l].astype(_BF), post_ffn_g=row(p['post_ffn_g']),
    )


def _trunk(x, mem, layers):
    assert x.shape[1] % SEQ_TILE == 0 and x.shape[2] == D_MODEL
    assert SEQ_TILE % ROW_BLOCK == 0 and ROW_BLOCK % SCAN_CHUNK == 0 and ROW_BLOCK % A_CHUNK == 0
    for l, w in enumerate(layers):
        obwd, q, v = _bwd_call(x, l, w['pre_mix_g'], w['w_bwd'], w['lb_bwd'])
        x = _mix_call(x, obwd, q, v, l, w['pre_mix_g'], w['w_mix'], w['v_ln_g'], w['v_ln_b'], w['w_spatial'],
                      w['b_spatial'], w['lb_fwd'], w['onorm_g'], w['w_out'], w['post_mix_g'])
        k, v = _kv_call(mem, l, w['mem_norm_g'], w['w_xkv'])
        x = _xattn_call(x, k, v, l, w['pre_x_g'], w['w_xq'], w['w_xo'], w['post_x_g'])
        x = _ffn_call(x, l, w['pre_ffn_g'], w['w_gu'], w['w_down'], w['post_ffn_g'])
    return x


def kernel(x_prompt, x_sample, mem_prompt, mem_sample, pre_mix_g, w_in, v_ln_g, v_ln_b, w_spatial, b_spatial, lb_raw, onorm_g, w_out, post_mix_g, pre_x_g, mem_norm_g, w_xq, w_xkv, w_xo, post_x_g, pre_ffn_g, w_gu, w_down, post_ffn_g):
    p = dict(pre_mix_g=pre_mix_g, w_in=w_in, v_ln_g=v_ln_g, v_ln_b=v_ln_b,
             w_spatial=w_spatial, b_spatial=b_spatial, lb_raw=lb_raw, onorm_g=onorm_g,
             w_out=w_out, post_mix_g=post_mix_g, pre_x_g=pre_x_g, mem_norm_g=mem_norm_g,
             w_xq=w_xq, w_xkv=w_xkv, w_xo=w_xo, post_x_g=post_x_g, pre_ffn_g=pre_ffn_g,
             w_gu=w_gu, w_down=w_down, post_ffn_g=post_ffn_g)
    layers = [_layer_weights(l, p) for l in range(w_in.shape[0])]
    return (_trunk(x_prompt, mem_prompt, layers), _trunk(x_sample, mem_sample, layers))
```

```python
import functools

import jax
import jax.numpy as jnp
from jax import lax
from jax.experimental import pallas as pl
from jax.experimental.pallas import tpu as pltpu

D_MODEL = 1024
A_WIDTH = 512
A_GROUPS = 4
A_GROUP_DIM = 128
A_CHUNK = 128
B_WIDTH = 512
B_HEADS = 4
B_DK = 128
MEM_TOKENS = 256
X_HEADS = 4
X_HEAD_DIM = 256
D_FF = 2816
EPS = 1e-6

SEQ_TILE = 1024
ROW_BLOCK = 256
SCAN_CHUNK = 64
SUB_BLOCK = 16
FACTORED_DECAY_LIMIT = 55.0
VMEM_LIMIT_BYTES = 60 * 1024 * 1024

_BF = jnp.bfloat16
_F32 = jnp.float32


def _dot(a, b):
    return jnp.dot(a.astype(_BF), b.astype(_BF), preferred_element_type=_F32)


def _dot_tb(a, b):
    return lax.dot_general(a.astype(_BF), b.astype(_BF), (((1,), (1,)), ((), ())),
                           preferred_element_type=_F32)


def _dot_ta(a, b):
    return lax.dot_general(a.astype(_BF), b.astype(_BF), (((0,), (0,)), ((), ())),
                           preferred_element_type=_F32)


def _rms(x, g):
    return x * lax.rsqrt(jnp.mean(x * x, axis=-1, keepdims=True) + EPS) * g


def _gelu(x):
    return 0.5 * x * (1.0 + lax.erf(x * (2.0 ** -0.5)))


def _silu(x):
    return x * jax.nn.sigmoid(x)


def _log_sigmoid(x):
    return jnp.minimum(x, 0.0) - jnp.log1p(jnp.exp(-jnp.abs(x)))


def _lower_bound(lbraw_ref, layer):
    depth = lbraw_ref.shape[0]
    rows = [lbraw_ref[j:j + 1, :] for j in range(depth)]
    m = functools.reduce(jnp.maximum, rows)
    e = [jnp.exp(r - m) for r in rows]
    denom = functools.reduce(lambda a, b: a + b, e)
    return functools.reduce(lambda a, b: a + b, e[1:layer + 1]) / denom


def _skewed(n, *stages):
    vals = {}
    for i in range(n + len(stages) - 1):
        for k, stage in enumerate(stages):
            r = i - k
            if 0 <= r < n:
                vals[(k, r)] = stage(r, vals.pop((k - 1, r), None))


def _row_block(r):
    return slice(r * ROW_BLOCK, (r + 1) * ROW_BLOCK)


def _split3(x):
    hi = x.astype(_BF)
    r1 = x - hi.astype(_F32)
    mid = r1.astype(_BF)
    lo = (r1 - mid.astype(_F32)).astype(_BF)
    return hi, mid, lo


def _cumulative(g, fwd):
    n = g.shape[0]
    row = lax.broadcasted_iota(jnp.int32, (n, n), 0)
    col = lax.broadcasted_iota(jnp.int32, (n, n), 1)
    tri = jnp.where(col <= row if fwd else col >= row, 1.0, 0.0).astype(_BF)
    g_hi, g_mid, g_lo = _split3(g)
    return (jnp.dot(tri, g_hi, preferred_element_type=_F32)
            + jnp.dot(tri, g_mid, preferred_element_type=_F32)
            + jnp.dot(tri, g_lo, preferred_element_type=_F32))


def _hgrn_gates(f, rows, lb, scan_scratch):
    k_s, g_s = scan_scratch[1], scan_scratch[3]
    if lb is None:
        g = _log_sigmoid(f)
        k_s[rows, :] = jax.nn.sigmoid(-f)
    else:
        forget = lb + (1.0 - lb) * jax.nn.sigmoid(f)
        g = jnp.log(forget)
        k_s[rows, :] = 1.0 - forget
    g_s[rows, :] = g
    return g


def _hgrn_cumulative(g, rows, scan_scratch, fwd):
    b_s = scan_scratch[4]
    for c in range(ROW_BLOCK // SCAN_CHUNK):
        lo = c * SCAN_CHUNK
        b_s[rows.start + lo:rows.start + lo + SCAN_CHUNK, :] = _cumulative(g[lo:lo + SCAN_CHUNK], fwd)


def _hgrn_scan(scan_scratch, st_s, fwd, emit):
    q_s, k_s, v_s, g_s, b_s, kc_s, bc_s, x_s, y_s, a_s, u_s = scan_scratch
    tile = q_s.shape[0]
    n_chunks = tile // SCAN_CHUNK
    n_sub = SCAN_CHUNK // SUB_BLOCK
    C = SCAN_CHUNK

    row = lax.broadcasted_iota(jnp.int32, (C, C), 0)
    col = lax.broadcasted_iota(jnp.int32, (C, C), 1)
    sub_row = lax.broadcasted_iota(jnp.int32, (SUB_BLOCK, C), 0)
    sub_col = lax.broadcasted_iota(jnp.int32, (SUB_BLOCK, C), 1)
    mid = C // 2 - 1 if fwd else C // 2
    last = C - 1 if fwd else 0

    def chunk_body(c, carry):
        idx = c if fwd else n_chunks - 1 - c
        base = pl.multiple_of(idx * C, C)
        rows = pl.ds(base, C)
        bc_s[...] = b_s[rows, :]
        kc_s[...] = k_s[rows, :]
        for hd in range(B_HEADS):
            lanes = slice(hd * B_DK, (hd + 1) * B_DK)
            q = q_s[rows, lanes]
            k = kc_s[:, lanes]
            v = v_s[rows, lanes]
            b = bc_s[:, lanes]
            st = st_s[hd]
            b_tot = b[last:last + 1]
            o = _dot_tb(q * jnp.exp(b), st)
            kd = k * jnp.exp(b_tot - b)
            st_s[hd] = st * jnp.exp(b_tot) + _dot_ta(v, kd)

            a_rows = []
            for i in range(n_sub):
                lo, hi = i * SUB_BLOCK, (i + 1) * SUB_BLOCK
                qi, bi = q[lo:hi], b[lo:hi]
                if fwd and i > 0:
                    r = b[lo - 1:lo]
                    ke = k[:lo] * jnp.exp(r - b[:lo])
                    ke = jnp.concatenate([ke, jnp.zeros((C - lo, B_DK), _F32)], axis=0)
                    a_i = _dot_tb(qi * jnp.exp(bi - r), ke)
                elif (not fwd) and i < n_sub - 1:
                    r = b[hi:hi + 1]
                    ke = k[hi:] * jnp.exp(r - b[hi:])
                    ke = jnp.concatenate([jnp.zeros((hi, B_DK), _F32), ke], axis=0)
                    a_i = _dot_tb(qi * jnp.exp(bi - r), ke)
                else:
                    a_i = jnp.zeros((SUB_BLOCK, C), _F32)
                for s in range(SUB_BLOCK):
                    j = lo + s
                    k_row = kc_s[j:j + 1, lanes]
                    b_row = bc_s[j:j + 1, lanes]
                    p = qi * k_row * jnp.exp(jnp.minimum(bi - b_row, 0.0))
                    a_i = jnp.where(sub_col == j, jnp.sum(p, axis=-1, keepdims=True), a_i)
                local = sub_col - lo
                if fwd:
                    keep = (sub_col < lo) | ((sub_col < hi) & (sub_row >= local))
                else:
                    keep = (sub_col >= hi) | ((sub_col >= lo) & (sub_row <= local))
                a_rows.append(jnp.where(keep, a_i, 0.0))
            a = jnp.concatenate(a_rows, axis=0)
            emit(rows, lanes, o + _dot(a, v))
        return carry

    def factored_tile():
        causal = (col <= row) if fwd else (col >= row)
        order = list(range(n_chunks)) if fwd else list(reversed(range(n_chunks)))
        chunk_rows = lambda c: slice(c * C, (c + 1) * C)
        head_lanes = lambda hd: slice(hd * B_DK, (hd + 1) * B_DK)

        def scale(hd, _):
            lanes = head_lanes(hd)
            for c in order:
                rows = chunk_rows(c)
                b = b_s[rows, lanes]
                e = jnp.exp(b - b[mid:mid + 1])
                x_s[rows, lanes] = (q_s[rows, lanes] * e).astype(_BF)
                y_s[rows, lanes] = (k_s[rows, lanes] / e).astype(_BF)

        def products(hd, _):
            lanes = head_lanes(hd)
            for c in order:
                rows = chunk_rows(c)
                yh = y_s[rows, lanes]
                a_s[hd, rows, :] = jnp.where(causal, _dot_tb(x_s[rows, lanes], yh), 0.0).astype(_BF)
                u_s[c, hd] = _dot_ta(v_s[rows, lanes], yh)

        def states(hd, _):
            lanes = head_lanes(hd)
            st = st_s[hd]
            for c in order:
                rho = b_s[c * C + mid:c * C + mid + 1, lanes]
                b_tot = b_s[c * C + last:c * C + last + 1, lanes]
                u = u_s[c, hd]
                u_s[c, hd] = st * jnp.exp(rho)
                st = st * jnp.exp(b_tot) + u * jnp.exp(b_tot - rho)
            st_s[hd] = st

        def outputs(hd, _):
            lanes = head_lanes(hd)
            for c in order:
                rows = chunk_rows(c)
                emit(rows, lanes, _dot_tb(x_s[rows, lanes], u_s[c, hd])
                     + _dot(a_s[hd, rows, :], v_s[rows, lanes]))

        _skewed(B_HEADS, scale, products, states, outputs)

    worst = jnp.zeros((1, B_WIDTH), _F32)
    for c in range(n_chunks):
        b_mid = b_s[c * C + mid:c * C + mid + 1, :]
        b_tot = b_s[c * C + last:c * C + last + 1, :]
        worst = jnp.maximum(worst, jnp.maximum(-b_mid, b_mid - b_tot))
    factored_ok = jnp.max(worst) < FACTORED_DECAY_LIMIT

    @pl.when(factored_ok)
    def _():
        factored_tile()

    @pl.when(jnp.logical_not(factored_ok))
    def _():
        lax.fori_loop(0, n_chunks, chunk_body, 0)


def _zero_state_at_sequence_start(st_s):
    @pl.when(pl.program_id(1) == 0)
    def _():
        st_s[...] = jnp.zeros_like(st_s)


def _bwd_kernel(layer, x_ref, gpre_ref, w_ref, lbraw_ref, o_ref, q_ref, v_ref, *scratch):
    (k_s, g_s), rest, st_s = scratch[:2], scratch[2:-1], scratch[-1]
    scan_scratch = (q_ref.at[0], k_s, v_ref.at[0], g_s) + tuple(rest)
    _zero_state_at_sequence_start(st_s)
    lb = _lower_bound(lbraw_ref, layer) if layer > 0 else None

    def project(r, _):
        h = _rms(x_ref[0, _row_block(r), :], gpre_ref[...]).astype(_BF)
        return jnp.dot(h, w_ref[...], preferred_element_type=_F32)

    def gates(r, p):
        rows = _row_block(r)
        q_ref[0, rows, :] = _silu(p[:, 0:512]) * (B_DK ** -0.5)
        v_ref[0, rows, :] = p[:, 1024:1536].astype(_BF)
        return _hgrn_gates(p[:, 512:1024], rows, lb, scan_scratch)

    def cumulate(r, g):
        _hgrn_cumulative(g, _row_block(r), scan_scratch, fwd=False)

    def emit(rows, lanes, o):
        o_ref[0, rows, lanes] = o

    _skewed(x_ref.shape[1] // ROW_BLOCK, project, gates, cumulate)
    _hgrn_scan(scan_scratch, st_s, False, emit)


def _mix_kernel(layer, x_ref, obwd_ref, q_ref, v_ref, gpre_ref, w_ref, vlng_ref, vlnb_ref, wsp_ref,
                bsp_ref, lbraw_ref, onorm_ref, wout_ref, gpost_ref, out_ref, *scratch):
    (k_s, g_s), rest, (gate_s, st_s, cat_s) = scratch[:2], scratch[2:-3], scratch[-3:]
    scan_scratch = (q_ref.at[0], k_s, v_ref.at[0], g_s) + tuple(rest)
    _zero_state_at_sequence_start(st_s)
    lb = _lower_bound(lbraw_ref, layer) if layer > 0 else None
    n_blocks = x_ref.shape[1] // ROW_BLOCK

    def project(r, _):
        h = _rms(x_ref[0, _row_block(r), :], gpre_ref[...]).astype(_BF)
        return jnp.dot(h, w_ref[...], preferred_element_type=_F32)

    def elementwise(r, p):
        rows = _row_block(r)
        u = _gelu(p[:, 0:512])
        v = _gelu(p[:, 512:1024])
        vc = v - jnp.mean(v, axis=-1, keepdims=True)
        v = vc * lax.rsqrt(jnp.mean(vc * vc, axis=-1, keepdims=True) + EPS) * vlng_ref[...] + vlnb_ref[...]
        g = _hgrn_gates(p[:, 1024:1536], rows, lb, scan_scratch)
        gate_s[rows, :] = _silu(p[:, 1536:2048])
        return u, v.astype(_BF), g

    def dependent_matmuls(r, uvg):
        rows = _row_block(r)
        u, v, g = uvg
        for n in range(ROW_BLOCK // A_CHUNK):
            chunk = slice(n * A_CHUNK, (n + 1) * A_CHUNK)
            for gi in range(A_GROUPS):
                lanes = slice(gi * A_GROUP_DIM, (gi + 1) * A_GROUP_DIM)
                sv = jnp.dot(wsp_ref[gi], v[chunk, lanes], preferred_element_type=_F32) + bsp_ref[gi]
                cat_s[rows.start + chunk.start:rows.start + chunk.stop, lanes] = (
                    u[chunk, lanes] * sv).astype(_BF)
        _hgrn_cumulative(g, rows, scan_scratch, fwd=True)

    def emit(rows, lanes, o):
        o_sum = o + obwd_ref[0, rows, lanes]
        cat_s[rows, A_WIDTH + lanes.start:A_WIDTH + lanes.stop] = (
            _rms(o_sum, onorm_ref[...]) * gate_s[rows, lanes]).astype(_BF)

    def out_proj(r, _):
        return jnp.dot(cat_s[_row_block(r), :], wout_ref[...], preferred_element_type=_F32)

    def residual(r, y):
        rows = _row_block(r)
        out_ref[0, rows, :] = x_ref[0, rows, :] + _rms(y, gpost_ref[...])

    _skewed(n_blocks, project, elementwise, dependent_matmuls)
    _hgrn_scan(scan_scratch, st_s, True, emit)
    _skewed(n_blocks, out_proj, residual)


def _kv_kernel(mem_ref, g_ref, w_ref, k_ref, v_ref):
    m = _rms(mem_ref[0], g_ref[...])
    kv = _dot(m, w_ref[...])
    k_ref[0] = kv[:, :D_MODEL].astype(_BF)
    v_ref[0] = kv[:, D_MODEL:].astype(_BF)


def _xattn_kernel(x_ref, k_ref, v_ref, gpre_ref, wq_ref, wo_ref, gpost_ref, out_ref, cat_s):
    head_lanes = lambda hd: slice(hd * X_HEAD_DIM, (hd + 1) * X_HEAD_DIM)

    def query(r, _):
        h = _rms(x_ref[0, _row_block(r), :], gpre_ref[...])
        return _dot(h, wq_ref[...]).astype(_BF)

    def scores(r, q):
        return [_dot_tb(q[:, head_lanes(hd)], k_ref[0, :, head_lanes(hd)]) for hd in range(X_HEADS)]

    def softmax(r, s_heads):
        p_heads = []
        for s in s_heads:
            s = s * (X_HEAD_DIM ** -0.5)
            e = jnp.exp(s - jnp.max(s, axis=-1, keepdims=True))
            p_heads.append((e / jnp.sum(e, axis=-1, keepdims=True)).astype(_BF))
        return p_heads

    def values(r, p_heads):
        for hd, p in enumerate(p_heads):
            cat_s[_row_block(r), head_lanes(hd)] = _dot(p, v_ref[0, :, head_lanes(hd)]).astype(_BF)

    def out_proj(r, _):
        return jnp.dot(cat_s[_row_block(r), :], wo_ref[...], preferred_element_type=_F32)

    def residual(r, y):
        rows = _row_block(r)
        out_ref[0, rows, :] = x_ref[0, rows, :] + _rms(y, gpost_ref[...])

    _skewed(x_ref.shape[1] // ROW_BLOCK, query, scores, softmax, values, out_proj, residual)


def _ffn_kernel(x_ref, gpre_ref, wgu_ref, wdown_ref, gpost_ref, out_ref):
    def gate_up(r, _):
        h = _rms(x_ref[0, _row_block(r), :], gpre_ref[...]).astype(_BF)
        return jnp.dot(h, wgu_ref[...], preferred_element_type=_F32)

    def activate(r, p):
        return (_silu(p[:, :D_FF]) * p[:, D_FF:]).astype(_BF)

    def down(r, act):
        return jnp.dot(act, wdown_ref[...], preferred_element_type=_F32)

    def residual(r, y):
        rows = _row_block(r)
        out_ref[0, rows, :] = x_ref[0, rows, :] + _rms(y, gpost_ref[...])

    _skewed(x_ref.shape[1] // ROW_BLOCK, gate_up, activate, down, residual)


def _params():
    return pltpu.CompilerParams(dimension_semantics=("arbitrary", "arbitrary"),
                                vmem_limit_bytes=VMEM_LIMIT_BYTES)


def _const_spec(shape):
    return pl.BlockSpec(shape, lambda b, t: (0,) * len(shape), pipeline_mode=pl.Buffered(1))


def _tile_spec(width, rev_tiles=None):
    if rev_tiles is None:
        return pl.BlockSpec((1, SEQ_TILE, width), lambda b, t: (b, t, 0))
    return pl.BlockSpec((1, SEQ_TILE, width), lambda b, t: (b, rev_tiles - 1 - t, 0))


def _scan_scratch():
    tile = lambda dt: pltpu.VMEM((SEQ_TILE, B_WIDTH), dt)
    chunk = lambda: pltpu.VMEM((SCAN_CHUNK, B_WIDTH), _F32)
    return [tile(_F32), tile(_F32), tile(_F32), chunk(), chunk(), tile(_BF), tile(_BF),
            pltpu.VMEM((B_HEADS, SEQ_TILE, SCAN_CHUNK), _BF),
            pltpu.VMEM((SEQ_TILE // SCAN_CHUNK, B_HEADS, B_DK, B_DK), _F32)]


def _state_scratch():
    return pltpu.VMEM((B_HEADS, B_DK, B_DK), _F32)


def _bwd_call(x, layer, gpre, w_bwd, lbraw):
    bn, ln, _ = x.shape
    nt = ln // SEQ_TILE
    return pl.pallas_call(
        functools.partial(_bwd_kernel, layer),
        grid=(bn, nt),
        in_specs=[_tile_spec(D_MODEL, nt), _const_spec(gpre.shape), _const_spec(w_bwd.shape),
                  _const_spec(lbraw.shape)],
        out_specs=[_tile_spec(B_WIDTH, nt)] * 3,
        out_shape=[jax.ShapeDtypeStruct((bn, ln, B_WIDTH), _F32),
                   jax.ShapeDtypeStruct((bn, ln, B_WIDTH), _F32),
                   jax.ShapeDtypeStruct((bn, ln, B_WIDTH), _BF)],
        scratch_shapes=_scan_scratch() + [_state_scratch()],
        compiler_params=_params(),
        name=f"hgrn_bwd_l{layer}",
    )(x, gpre, w_bwd, lbraw)


def _mix_call(x, obwd, q, v, layer, gpre, w_mix, vlng, vlnb, wsp, bsp, lbraw, onorm, wout, gpost):
    bn, ln, _ = x.shape
    nt = ln // SEQ_TILE
    consts = (gpre, w_mix, vlng, vlnb, wsp, bsp, lbraw, onorm, wout, gpost)
    return pl.pallas_call(
        functools.partial(_mix_kernel, layer),
        grid=(bn, nt),
        in_specs=[_tile_spec(D_MODEL)] + [_tile_spec(B_WIDTH)] * 3 + [_const_spec(c.shape) for c in consts],
        out_specs=_tile_spec(D_MODEL),
        out_shape=jax.ShapeDtypeStruct(x.shape, _F32),
        scratch_shapes=_scan_scratch() + [pltpu.VMEM((SEQ_TILE, B_WIDTH), _F32), _state_scratch(),
                                          pltpu.VMEM((SEQ_TILE, D_MODEL), _BF)],
        compiler_params=_params(),
        name=f"token_mix_l{layer}",
    )(x, obwd, q, v, *consts)


def _kv_call(mem, layer, g, w):
    bn = mem.shape[0]
    spec = pl.BlockSpec((1, MEM_TOKENS, D_MODEL), lambda b, t: (b, 0, 0))
    return pl.pallas_call(
        _kv_kernel,
        grid=(bn, 1),
        in_specs=[spec, _const_spec(g.shape), _const_spec(w.shape)],
        out_specs=[spec, spec],
        out_shape=[jax.ShapeDtypeStruct(mem.shape, _BF)] * 2,
        compiler_params=_params(),
        name=f"mem_kv_l{layer}",
    )(mem, g, w)


def _xattn_call(x, k, v, layer, gpre, wq, wo, gpost):
    bn, ln, _ = x.shape
    nt = ln // SEQ_TILE
    mem_spec = pl.BlockSpec((1, MEM_TOKENS, D_MODEL), lambda b, t: (b, 0, 0))
    consts = (gpre, wq, wo, gpost)
    return pl.pallas_call(
        _xattn_kernel,
        grid=(bn, nt),
        in_specs=[_tile_spec(D_MODEL), mem_spec, mem_spec] + [_const_spec(c.shape) for c in consts],
        out_specs=_tile_spec(D_MODEL),
        out_shape=jax.ShapeDtypeStruct(x.shape, _F32),
        scratch_shapes=[pltpu.VMEM((SEQ_TILE, D_MODEL), _BF)],
        compiler_params=_params(),
        name=f"mem_xattn_l{layer}",
    )(x, k, v, *consts)


def _ffn_call(x, layer, gpre, wgu, wdown, gpost):
    bn, ln, _ = x.shape
    nt = ln // SEQ_TILE
    consts = (gpre, wgu, wdown, gpost)
    return pl.pallas_call(
        _ffn_kernel,
        grid=(bn, nt),
        in_specs=[_tile_spec(D_MODEL)] + [_const_spec(c.shape) for c in consts],
        out_specs=_tile_spec(D_MODEL),
        out_shape=jax.ShapeDtypeStruct(x.shape, _F32),
        compiler_params=_params(),
        name=f"swiglu_ffn_l{layer}",
    )(x, *consts)


def _layer_weights(l, p):
    w_in = p['w_in'][l].astype(_BF)
    cols = lambda a, b: w_in[:, a:b]
    row = lambda a: a[l][None, :]
    return dict(
        w_mix=jnp.concatenate([cols(0, 1024), cols(1536, 2048), cols(3072, 3584)], axis=1),
        w_bwd=jnp.concatenate([cols(1024, 1536), cols(2048, 3072)], axis=1),
        pre_mix_g=row(p['pre_mix_g']), v_ln_g=row(p['v_ln_g']), v_ln_b=row(p['v_ln_b']),
        w_spatial=p['w_spatial'][l].astype(_BF),
        b_spatial=jnp.broadcast_to(p['b_spatial'][l][:, :, None], (A_GROUPS, A_CHUNK, A_GROUP_DIM)),
        lb_fwd=p['lb_raw'][0], lb_bwd=p['lb_raw'][1],
        onorm_g=row(p['onorm_g']), w_out=p['w_out'][l].astype(_BF), post_mix_g=row(p['post_mix_g']),
        pre_x_g=row(p['pre_x_g']), mem_norm_g=row(p['mem_norm_g']),
        w_xq=p['w_xq'][l].astype(_BF), w_xkv=p['w_xkv'][l].astype(_BF), w_xo=p['w_xo'][l].astype(_BF),
        post_x_g=row(p['post_x_g']), pre_ffn_g=row(p['pre_ffn_g']),
        w_gu=p['w_gu'][l].astype(_BF), w_down=p['w_down'][l].astype(_BF), post_ffn_g=row(p['post_ffn_g']),
    )


def _trunk(x, mem, layers):
    assert x.shape[1] % SEQ_TILE == 0 and x.shape[2] == D_MODEL
    assert SEQ_TILE % ROW_BLOCK == 0 and ROW_BLOCK % SCAN_CHUNK == 0 and ROW_BLOCK % A_CHUNK == 0
    for l, w in enumerate(layers):
        obwd, q, v = _bwd_call(x, l, w['pre_mix_g'], w['w_bwd'], w['lb_bwd'])
        x = _mix_call(x, obwd, q, v, l, w['pre_mix_g'], w['w_mix'], w['v_ln_g'], w['v_ln_b'], w['w_spatial'],
                      w['b_spatial'], w['lb_fwd'], w['onorm_g'], w['w_out'], w['post_mix_g'])
        k, v = _kv_call(mem, l, w['mem_norm_g'], w['w_xkv'])
        x = _xattn_call(x, k, v, l, w['pre_x_g'], w['w_xq'], w['w_xo'], w['post_x_g'])
        x = _ffn_call(x, l, w['pre_ffn_g'], w['w_gu'], w['w_down'], w['post_ffn_g'])
    return x


def kernel(x_prompt, x_sample, mem_prompt, mem_sample, pre_mix_g, w_in, v_ln_g, v_ln_b, w_spatial, b_spatial, lb_raw, onorm_g, w_out, post_mix_g, pre_x_g, mem_norm_g, w_xq, w_xkv, w_xo, post_x_g, pre_ffn_g, w_gu, w_down, post_ffn_g):
    p = dict(pre_mix_g=pre_mix_g, w_in=w_in, v_ln_g=v_ln_g, v_ln_b=v_ln_b,
             w_spatial=w_spatial, b_spatial=b_spatial, lb_raw=lb_raw, onorm_g=onorm_g,
             w_out=w_out, post_mix_g=post_mix_g, pre_x_g=pre_x_g, mem_norm_g=mem_norm_g,
             w_xq=w_xq, w_xkv=w_xkv, w_xo=w_xo, post_x_g=post_x_g, pre_ffn_g=pre_ffn_g,
             w_gu=w_gu, w_down=w_down, post_ffn_g=post_ffn_g)
    layers = [_layer_weights(l, p) for l in range(w_in.shape[0])]
    return (_trunk(x_prompt, mem_prompt, layers), _trunk(x_sample, mem_sample, layers))
```

```python
import functools

import jax
import jax.numpy as jnp
from jax import lax
from jax.experimental import pallas as pl
from jax.experimental.pallas import tpu as pltpu

D_MODEL = 1024
A_WIDTH = 512
A_GROUPS = 4
A_GROUP_DIM = 128
A_CHUNK = 128
B_WIDTH = 512
B_HEADS = 4
B_DK = 128
MEM_TOKENS = 256
X_HEADS = 4
X_HEAD_DIM = 256
D_FF = 2816
EPS = 1e-6

SEQ_TILE = 1024
ROW_BLOCK = 256
SCAN_CHUNK = 64
SUB_BLOCK = 16
FACTORED_DECAY_LIMIT = 55.0
VMEM_LIMIT_BYTES = 60 * 1024 * 1024

_BF = jnp.bfloat16
_F32 = jnp.float32


def _dot(a, b):
    return jnp.dot(a.astype(_BF), b.astype(_BF), preferred_element_type=_F32)


def _dot_tb(a, b):
    return lax.dot_general(a.astype(_BF), b.astype(_BF), (((1,), (1,)), ((), ())),
                           preferred_element_type=_F32)


def _dot_ta(a, b):
    return lax.dot_general(a.astype(_BF), b.astype(_BF), (((0,), (0,)), ((), ())),
                           preferred_element_type=_F32)


def _rms(x, g):
    return x * lax.rsqrt(jnp.mean(x * x, axis=-1, keepdims=True) + EPS) * g


def _gelu(x):
    return 0.5 * x * (1.0 + lax.erf(x * (2.0 ** -0.5)))


def _silu(x):
    return x * jax.nn.sigmoid(x)


def _log_sigmoid(x):
    return jnp.minimum(x, 0.0) - jnp.log1p(jnp.exp(-jnp.abs(x)))


def _lower_bound(lbraw_ref, layer):
    depth = lbraw_ref.shape[0]
    rows = [lbraw_ref[j:j + 1, :] for j in range(depth)]
    m = functools.reduce(jnp.maximum, rows)
    e = [jnp.exp(r - m) for r in rows]
    denom = functools.reduce(lambda a, b: a + b, e)
    return functools.reduce(lambda a, b: a + b, e[1:layer + 1]) / denom


def _skewed(order, *stages):
    order = list(order)
    vals = {}
    for i in range(len(order) + len(stages) - 1):
        for k, stage in enumerate(stages):
            if 0 <= i - k < len(order):
                r = order[i - k]
                vals[(k, r)] = stage(r, vals.pop((k - 1, r), None))


def _row_block(r):
    return slice(r * ROW_BLOCK, (r + 1) * ROW_BLOCK)


def _split3(x):
    hi = x.astype(_BF)
    r1 = x - hi.astype(_F32)
    mid = r1.astype(_BF)
    lo = (r1 - mid.astype(_F32)).astype(_BF)
    return hi, mid, lo


def _cumulative(g, fwd):
    n = g.shape[0]
    row = lax.broadcasted_iota(jnp.int32, (n, n), 0)
    col = lax.broadcasted_iota(jnp.int32, (n, n), 1)
    tri = jnp.where(col <= row if fwd else col >= row, 1.0, 0.0).astype(_BF)
    g_hi, g_mid, g_lo = _split3(g)
    return (jnp.dot(tri, g_hi, preferred_element_type=_F32)
            + jnp.dot(tri, g_mid, preferred_element_type=_F32)
            + jnp.dot(tri, g_lo, preferred_element_type=_F32))


def _hgrn_gates(f, rows, lb, scan_scratch):
    k_s, g_s = scan_scratch[1], scan_scratch[3]
    if lb is None:
        g = _log_sigmoid(f)
        k_s[rows, :] = jax.nn.sigmoid(-f)
    else:
        forget = lb + (1.0 - lb) * jax.nn.sigmoid(f)
        g = jnp.log(forget)
        k_s[rows, :] = 1.0 - forget
    g_s[rows, :] = g
    return g


def _hgrn_cumulative(g, rows, scan_scratch, fwd):
    b_s = scan_scratch[4]
    for c in range(ROW_BLOCK // SCAN_CHUNK):
        lo = c * SCAN_CHUNK
        b_s[rows.start + lo:rows.start + lo + SCAN_CHUNK, :] = _cumulative(g[lo:lo + SCAN_CHUNK], fwd)


def _scan_exact_loop(scan_scratch, st_s, fwd, emit):
    q_s, k_s, v_s, g_s, b_s, kc_s, bc_s = scan_scratch[:7]
    C = SCAN_CHUNK
    n_chunks = q_s.shape[0] // C
    n_sub = C // SUB_BLOCK
    sub_row = lax.broadcasted_iota(jnp.int32, (SUB_BLOCK, C), 0)
    sub_col = lax.broadcasted_iota(jnp.int32, (SUB_BLOCK, C), 1)
    last = C - 1 if fwd else 0

    def chunk_body(c, carry):
        idx = c if fwd else n_chunks - 1 - c
        base = pl.multiple_of(idx * C, C)
        rows = pl.ds(base, C)
        bc_s[...] = b_s[rows, :]
        kc_s[...] = k_s[rows, :]
        for hd in range(B_HEADS):
            lanes = slice(hd * B_DK, (hd + 1) * B_DK)
            q = q_s[rows, lanes]
            k = kc_s[:, lanes]
            v = v_s[rows, lanes]
            b = bc_s[:, lanes]
            st = st_s[hd]
            b_tot = b[last:last + 1]
            o = _dot_tb(q * jnp.exp(b), st)
            kd = k * jnp.exp(b_tot - b)
            st_s[hd] = st * jnp.exp(b_tot) + _dot_ta(v, kd)

            a_rows = []
            for i in range(n_sub):
                lo, hi = i * SUB_BLOCK, (i + 1) * SUB_BLOCK
                qi, bi = q[lo:hi], b[lo:hi]
                if fwd and i > 0:
                    r = b[lo - 1:lo]
                    ke = k[:lo] * jnp.exp(r - b[:lo])
                    ke = jnp.concatenate([ke, jnp.zeros((C - lo, B_DK), _F32)], axis=0)
                    a_i = _dot_tb(qi * jnp.exp(bi - r), ke)
                elif (not fwd) and i < n_sub - 1:
                    r = b[hi:hi + 1]
                    ke = k[hi:] * jnp.exp(r - b[hi:])
                    ke = jnp.concatenate([jnp.zeros((hi, B_DK), _F32), ke], axis=0)
                    a_i = _dot_tb(qi * jnp.exp(bi - r), ke)
                else:
                    a_i = jnp.zeros((SUB_BLOCK, C), _F32)
                for s in range(SUB_BLOCK):
                    j = lo + s
                    k_row = kc_s[j:j + 1, lanes]
                    b_row = bc_s[j:j + 1, lanes]
                    p = qi * k_row * jnp.exp(jnp.minimum(bi - b_row, 0.0))
                    a_i = jnp.where(sub_col == j, jnp.sum(p, axis=-1, keepdims=True), a_i)
                local = sub_col - lo
                if fwd:
                    keep = (sub_col < lo) | ((sub_col < hi) & (sub_row >= local))
                else:
                    keep = (sub_col >= hi) | ((sub_col >= lo) & (sub_row <= local))
                a_rows.append(jnp.where(keep, a_i, 0.0))
            a = jnp.concatenate(a_rows, axis=0)
            emit(rows, lanes, o + _dot(a, v))
        return carry

    lax.fori_loop(0, n_chunks, chunk_body, 0)


def _scan_reference_rows(fwd):
    return (SCAN_CHUNK // 2 - 1, SCAN_CHUNK - 1) if fwd else (SCAN_CHUNK // 2, 0)


def _scan_factored_stages(scan_scratch, st_s, fwd, emit):
    q_s, k_s, v_s, g_s, b_s, kc_s, bc_s, x_s, y_s, a_s, u_s = scan_scratch
    C = SCAN_CHUNK
    mid, last = _scan_reference_rows(fwd)
    row = lax.broadcasted_iota(jnp.int32, (C, C), 0)
    col = lax.broadcasted_iota(jnp.int32, (C, C), 1)
    causal = (col <= row) if fwd else (col >= row)
    chunk_rows = lambda c: slice(c * C, (c + 1) * C)
    head_lanes = lambda hd: slice(hd * B_DK, (hd + 1) * B_DK)

    def chunks(r):
        per_block = ROW_BLOCK // C
        ids = list(range(r * per_block, (r + 1) * per_block))
        return ids if fwd else ids[::-1]

    def scale(r, _):
        for c in chunks(r):
            rows = chunk_rows(c)
            b = b_s[rows, :]
            e = jnp.exp(b - b[mid:mid + 1])
            x_s[rows, :] = (q_s[rows, :] * e).astype(_BF)
            y_s[rows, :] = (k_s[rows, :] / e).astype(_BF)

    def products(r, _):
        for c in chunks(r):
            rows = chunk_rows(c)
            for hd in range(B_HEADS):
                lanes = head_lanes(hd)
                yh = y_s[rows, lanes]
                a_s[hd, rows, :] = jnp.where(causal, _dot_tb(x_s[rows, lanes], yh), 0.0).astype(_BF)
                u_s[c, hd] = _dot_ta(v_s[rows, lanes], yh)

    def states(r, _):
        for hd in range(B_HEADS):
            lanes = head_lanes(hd)
            st = st_s[hd]
            for c in chunks(r):
                rho = b_s[c * C + mid:c * C + mid + 1, lanes]
                b_tot = b_s[c * C + last:c * C + last + 1, lanes]
                u = u_s[c, hd]
                u_s[c, hd] = st * jnp.exp(rho)
                st = st * jnp.exp(b_tot) + u * jnp.exp(b_tot - rho)
            st_s[hd] = st

    def outputs(r, _):
        for c in chunks(r):
            rows = chunk_rows(c)
            for hd in range(B_HEADS):
                lanes = head_lanes(hd)
                emit(rows, lanes, _dot_tb(x_s[rows, lanes], u_s[c, hd])
                     + _dot(a_s[hd, rows, :], v_s[rows, lanes]))

    return scale, products, states, outputs


def _scan_factored_ok(scan_scratch, fwd):
    b_s = scan_scratch[4]
    C = SCAN_CHUNK
    mid, last = _scan_reference_rows(fwd)
    worst = jnp.zeros((1, B_WIDTH), _F32)
    for c in range(b_s.shape[0] // C):
        b_mid = b_s[c * C + mid:c * C + mid + 1, :]
        b_tot = b_s[c * C + last:c * C + last + 1, :]
        worst = jnp.maximum(worst, jnp.maximum(-b_mid, b_mid - b_tot))
    return jnp.max(worst) < FACTORED_DECAY_LIMIT


def _begin_tile(st_s, st0_s):
    @pl.when(pl.program_id(1) == 0)
    def _():
        st_s[...] = jnp.zeros_like(st_s)
    st0_s[...] = st_s[...]


def _bwd_kernel(layer, x_ref, gpre_ref, w_ref, lbraw_ref, o_ref, q_ref, v_ref, *scratch):
    (k_s, g_s), rest, (st_s, st0_s) = scratch[:2], scratch[2:-2], scratch[-2:]
    scan_scratch = (q_ref.at[0], k_s, v_ref.at[0], g_s) + tuple(rest)
    _begin_tile(st_s, st0_s)
    lb = _lower_bound(lbraw_ref, layer) if layer > 0 else None

    def project(r, _):
        h = _rms(x_ref[0, _row_block(r), :], gpre_ref[...]).astype(_BF)
        return jnp.dot(h, w_ref[...], preferred_element_type=_F32)

    def gates(r, p):
        rows = _row_block(r)
        q_ref[0, rows, :] = _silu(p[:, 0:512]) * (B_DK ** -0.5)
        v_ref[0, rows, :] = p[:, 1024:1536].astype(_BF)
        return _hgrn_gates(p[:, 512:1024], rows, lb, scan_scratch)

    def cumulate(r, g):
        _hgrn_cumulative(g, _row_block(r), scan_scratch, fwd=False)

    def emit(rows, lanes, o):
        o_ref[0, rows, lanes] = o

    blocks = reversed(range(x_ref.shape[1] // ROW_BLOCK))
    _skewed(blocks, project, gates, cumulate, *_scan_factored_stages(scan_scratch, st_s, False, emit))

    @pl.when(jnp.logical_not(_scan_factored_ok(scan_scratch, False)))
    def _():
        st_s[...] = st0_s[...]
        _scan_exact_loop(scan_scratch, st_s, False, emit)


def _mix_kernel(layer, x_ref, obwd_ref, q_ref, v_ref, gpre_ref, w_ref, vlng_ref, vlnb_ref, wsp_ref,
                bsp_ref, lbraw_ref, onorm_ref, wout_ref, gpost_ref, out_ref, *scratch):
    (k_s, g_s), rest, (gate_s, st_s, st0_s, cat_s) = scratch[:2], scratch[2:-4], scratch[-4:]
    scan_scratch = (q_ref.at[0], k_s, v_ref.at[0], g_s) + tuple(rest)
    _begin_tile(st_s, st0_s)
    lb = _lower_bound(lbraw_ref, layer) if layer > 0 else None
    blocks = range(x_ref.shape[1] // ROW_BLOCK)

    def project(r, _):
        h = _rms(x_ref[0, _row_block(r), :], gpre_ref[...]).astype(_BF)
        return jnp.dot(h, w_ref[...], preferred_element_type=_F32)

    def elementwise(r, p):
        rows = _row_block(r)
        u = _gelu(p[:, 0:512])
        v = _gelu(p[:, 512:1024])
        vc = v - jnp.mean(v, axis=-1, keepdims=True)
        v = vc * lax.rsqrt(jnp.mean(vc * vc, axis=-1, keepdims=True) + EPS) * vlng_ref[...] + vlnb_ref[...]
        g = _hgrn_gates(p[:, 1024:1536], rows, lb, scan_scratch)
        gate_s[rows, :] = _silu(p[:, 1536:2048])
        return u, v.astype(_BF), g

    def dependent_matmuls(r, uvg):
        rows = _row_block(r)
        u, v, g = uvg
        for n in range(ROW_BLOCK // A_CHUNK):
            chunk = slice(n * A_CHUNK, (n + 1) * A_CHUNK)
            for gi in range(A_GROUPS):
                lanes = slice(gi * A_GROUP_DIM, (gi + 1) * A_GROUP_DIM)
                sv = jnp.dot(wsp_ref[gi], v[chunk, lanes], preferred_element_type=_F32) + bsp_ref[gi]
                cat_s[rows.start + chunk.start:rows.start + chunk.stop, lanes] = (
                    u[chunk, lanes] * sv).astype(_BF)
        _hgrn_cumulative(g, rows, scan_scratch, fwd=True)

    def emit(rows, lanes, o):
        o_sum = o + obwd_ref[0, rows, lanes]
        cat_s[rows, A_WIDTH + lanes.start:A_WIDTH + lanes.stop] = (
            _rms(o_sum, onorm_ref[...]) * gate_s[rows, lanes]).astype(_BF)

    def out_proj(r, _):
        return jnp.dot(cat_s[_row_block(r), :], wout_ref[...], preferred_element_type=_F32)

    def residual(r, y):
        rows = _row_block(r)
        out_ref[0, rows, :] = x_ref[0, rows, :] + _rms(y, gpost_ref[...])

    _skewed(blocks, project, elementwise, dependent_matmuls,
            *_scan_factored_stages(scan_scratch, st_s, True, emit), out_proj, residual)

    @pl.when(jnp.logical_not(_scan_factored_ok(scan_scratch, True)))
    def _():
        st_s[...] = st0_s[...]
        _scan_exact_loop(scan_scratch, st_s, True, emit)
        _skewed(blocks, out_proj, residual)


def _kv_kernel(mem_ref, g_ref, w_ref, k_ref, v_ref):
    m = _rms(mem_ref[0], g_ref[...])
    kv = _dot(m, w_ref[...])
    k_ref[0] = kv[:, :D_MODEL].astype(_BF)
    v_ref[0] = kv[:, D_MODEL:].astype(_BF)


def _xattn_kernel(x_ref, k_ref, v_ref, gpre_ref, wq_ref, wo_ref, gpost_ref, out_ref, cat_s):
    head_lanes = lambda hd: slice(hd * X_HEAD_DIM, (hd + 1) * X_HEAD_DIM)

    def query(r, _):
        h = _rms(x_ref[0, _row_block(r), :], gpre_ref[...])
        return _dot(h, wq_ref[...]).astype(_BF)

    def scores(r, q):
        return [_dot_tb(q[:, head_lanes(hd)], k_ref[0, :, head_lanes(hd)]) for hd in range(X_HEADS)]

    def softmax(r, s_heads):
        p_heads = []
        for s in s_heads:
            s = s * (X_HEAD_DIM ** -0.5)
            e = jnp.exp(s - jnp.max(s, axis=-1, keepdims=True))
            p_heads.append((e / jnp.sum(e, axis=-1, keepdims=True)).astype(_BF))
        return p_heads

    def values(r, p_heads):
        for hd, p in enumerate(p_heads):
            cat_s[_row_block(r), head_lanes(hd)] = _dot(p, v_ref[0, :, head_lanes(hd)]).astype(_BF)

    def out_proj(r, _):
        return jnp.dot(cat_s[_row_block(r), :], wo_ref[...], preferred_element_type=_F32)

    def residual(r, y):
        rows = _row_block(r)
        out_ref[0, rows, :] = x_ref[0, rows, :] + _rms(y, gpost_ref[...])

    _skewed(range(x_ref.shape[1] // ROW_BLOCK), query, scores, softmax, values, out_proj, residual)


def _ffn_kernel(x_ref, gpre_ref, wgu_ref, wdown_ref, gpost_ref, out_ref):
    def gate_up(r, _):
        h = _rms(x_ref[0, _row_block(r), :], gpre_ref[...]).astype(_BF)
        return jnp.dot(h, wgu_ref[...], preferred_element_type=_F32)

    def activate(r, p):
        return (_silu(p[:, :D_FF]) * p[:, D_FF:]).astype(_BF)

    def down(r, act):
        return jnp.dot(act, wdown_ref[...], preferred_element_type=_F32)

    def residual(r, y):
        rows = _row_block(r)
        out_ref[0, rows, :] = x_ref[0, rows, :] + _rms(y, gpost_ref[...])

    _skewed(range(x_ref.shape[1] // ROW_BLOCK), gate_up, activate, down, residual)


def _params():
    return pltpu.CompilerParams(dimension_semantics=("arbitrary", "arbitrary"),
                                vmem_limit_bytes=VMEM_LIMIT_BYTES)


def _const_spec(shape):
    return pl.BlockSpec(shape, lambda b, t: (0,) * len(shape), pipeline_mode=pl.Buffered(1))


def _tile_spec(width, rev_tiles=None):
    if rev_tiles is None:
        return pl.BlockSpec((1, SEQ_TILE, width), lambda b, t: (b, t, 0))
    return pl.BlockSpec((1, SEQ_TILE, width), lambda b, t: (b, rev_tiles - 1 - t, 0))


def _scan_scratch():
    tile = lambda dt: pltpu.VMEM((SEQ_TILE, B_WIDTH), dt)
    chunk = lambda: pltpu.VMEM((SCAN_CHUNK, B_WIDTH), _F32)
    return [tile(_F32), tile(_F32), tile(_F32), chunk(), chunk(), tile(_BF), tile(_BF),
            pltpu.VMEM((B_HEADS, SEQ_TILE, SCAN_CHUNK), _BF),
            pltpu.VMEM((SEQ_TILE // SCAN_CHUNK, B_HEADS, B_DK, B_DK), _F32)]


def _state_scratch():
    return pltpu.VMEM((B_HEADS, B_DK, B_DK), _F32)


def _bwd_call(x, layer, gpre, w_bwd, lbraw):
    bn, ln, _ = x.shape
    nt = ln // SEQ_TILE
    return pl.pallas_call(
        functools.partial(_bwd_kernel, layer),
        grid=(bn, nt),
        in_specs=[_tile_spec(D_MODEL, nt), _const_spec(gpre.shape), _const_spec(w_bwd.shape),
                  _const_spec(lbraw.shape)],
        out_specs=[_tile_spec(B_WIDTH, nt)] * 3,
        out_shape=[jax.ShapeDtypeStruct((bn, ln, B_WIDTH), _F32),
                   jax.ShapeDtypeStruct((bn, ln, B_WIDTH), _F32),
                   jax.ShapeDtypeStruct((bn, ln, B_WIDTH), _BF)],
        scratch_shapes=_scan_scratch() + [_state_scratch(), _state_scratch()],
        compiler_params=_params(),
        name=f"hgrn_bwd_l{layer}",
    )(x, gpre, w_bwd, lbraw)


def _mix_call(x, obwd, q, v, layer, gpre, w_mix, vlng, vlnb, wsp, bsp, lbraw, onorm, wout, gpost):
    bn, ln, _ = x.shape
    nt = ln // SEQ_TILE
    consts = (gpre, w_mix, vlng, vlnb, wsp, bsp, lbraw, onorm, wout, gpost)
    return pl.pallas_call(
        functools.partial(_mix_kernel, layer),
        grid=(bn, nt),
        in_specs=[_tile_spec(D_MODEL)] + [_tile_spec(B_WIDTH)] * 3 + [_const_spec(c.shape) for c in consts],
        out_specs=_tile_spec(D_MODEL),
        out_shape=jax.ShapeDtypeStruct(x.shape, _F32),
        scratch_shapes=_scan_scratch() + [pltpu.VMEM((SEQ_TILE, B_WIDTH), _F32), _state_scratch(),
                                          _state_scratch(),
                                          pltpu.VMEM((SEQ_TILE, D_MODEL), _BF)],
        compiler_params=_params(),
        name=f"token_mix_l{layer}",
    )(x, obwd, q, v, *consts)


def _kv_call(mem, layer, g, w):
    bn = mem.shape[0]
    spec = pl.BlockSpec((1, MEM_TOKENS, D_MODEL), lambda b, t: (b, 0, 0))
    return pl.pallas_call(
        _kv_kernel,
        grid=(bn, 1),
        in_specs=[spec, _const_spec(g.shape), _const_spec(w.shape)],
        out_specs=[spec, spec],
        out_shape=[jax.ShapeDtypeStruct(mem.shape, _BF)] * 2,
        compiler_params=_params(),
        name=f"mem_kv_l{layer}",
    )(mem, g, w)


def _xattn_call(x, k, v, layer, gpre, wq, wo, gpost):
    bn, ln, _ = x.shape
    nt = ln // SEQ_TILE
    mem_spec = pl.BlockSpec((1, MEM_TOKENS, D_MODEL), lambda b, t: (b, 0, 0))
    consts = (gpre, wq, wo, gpost)
    return pl.pallas_call(
        _xattn_kernel,
        grid=(bn, nt),
        in_specs=[_tile_spec(D_MODEL), mem_spec, mem_spec] + [_const_spec(c.shape) for c in consts],
        out_specs=_tile_spec(D_MODEL),
        out_shape=jax.ShapeDtypeStruct(x.shape, _F32),
        scratch_shapes=[pltpu.VMEM((SEQ_TILE, D_MODEL), _BF)],
        compiler_params=_params(),
        name=f"mem_xattn_l{layer}",
    )(x, k, v, *consts)


def _ffn_call(x, layer, gpre, wgu, wdown, gpost):
    bn, ln, _ = x.shape
    nt = ln // SEQ_TILE
    consts = (gpre, wgu, wdown, gpost)
    return pl.pallas_call(
        _ffn_kernel,
        grid=(bn, nt),
        in_specs=[_tile_spec(D_MODEL)] + [_const_spec(c.shape) for c in consts],
        out_specs=_tile_spec(D_MODEL),
        out_shape=jax.ShapeDtypeStruct(x.shape, _F32),
        compiler_params=_params(),
        name=f"swiglu_ffn_l{layer}",
    )(x, *consts)


def _layer_weights(l, p):
    w_in = p['w_in'][l].astype(_BF)
    cols = lambda a, b: w_in[:, a:b]
    row = lambda a: a[l][None, :]
    return dict(
        w_mix=jnp.concatenate([cols(0, 1024), cols(1536, 2048), cols(3072, 3584)], axis=1),
        w_bwd=jnp.concatenate([cols(1024, 1536), cols(2048, 3072)], axis=1),
        pre_mix_g=row(p['pre_mix_g']), v_ln_g=row(p['v_ln_g']), v_ln_b=row(p['v_ln_b']),
        w_spatial=p['w_spatial'][l].astype(_BF),
        b_spatial=jnp.broadcast_to(p['b_spatial'][l][:, :, None], (A_GROUPS, A_CHUNK, A_GROUP_DIM)),
        lb_fwd=p['lb_raw'][0], lb_bwd=p['lb_raw'][1],
        onorm_g=row(p['onorm_g']), w_out=p['w_out'][l].astype(_BF), post_mix_g=row(p['post_mix_g']),
        pre_x_g=row(p['pre_x_g']), mem_norm_g=row(p['mem_norm_g']),
        w_xq=p['w_xq'][l].astype(_BF), w_xkv=p['w_xkv'][l].astype(_BF), w_xo=p['w_xo'][l].astype(_BF),
        post_x_g=row(p['post_x_g']), pre_ffn_g=row(p['pre_ffn_g']),
        w_gu=p['w_gu'][l].astype(_BF), w_down=p['w_down'][l].astype(_BF), post_ffn_g=row(p['post_ffn_g']),
    )


def _trunk(x, mem, layers):
    assert x.shape[1] % SEQ_TILE == 0 and x.shape[2] == D_MODEL
    assert SEQ_TILE % ROW_BLOCK == 0 and ROW_BLOCK % SCAN_CHUNK == 0 and ROW_BLOCK % A_CHUNK == 0
    for l, w in enumerate(layers):
        obwd, q, v = _bwd_call(x, l, w['pre_mix_g'], w['w_bwd'], w['lb_bwd'])
        x = _mix_call(x, obwd, q, v, l, w['pre_mix_g'], w['w_mix'], w['v_ln_g'], w['v_ln_b'], w['w_spatial'],
                      w['b_spatial'], w['lb_fwd'], w['onorm_g'], w['w_out'], w['post_mix_g'])
        k, v = _kv_call(mem, l, w['mem_norm_g'], w['w_xkv'])
        x = _xattn_call(x, k, v, l, w['pre_x_g'], w['w_xq'], w['w_xo'], w['post_x_g'])
        x = _ffn_call(x, l, w['pre_ffn_g'], w['w_gu'], w['w_down'], w['post_ffn_g'])
    return x


def kernel(x_prompt, x_sample, mem_prompt, mem_sample, pre_mix_g, w_in, v_ln_g, v_ln_b, w_spatial, b_spatial, lb_raw, onorm_g, w_out, post_mix_g, pre_x_g, mem_norm_g, w_xq, w_xkv, w_xo, post_x_g, pre_ffn_g, w_gu, w_down, post_ffn_g):
    p = dict(pre_mix_g=pre_mix_g, w_in=w_in, v_ln_g=v_ln_g, v_ln_b=v_ln_b,
             w_spatial=w_spatial, b_spatial=b_spatial, lb_raw=lb_raw, onorm_g=onorm_g,
             w_out=w_out, post_mix_g=post_mix_g, pre_x_g=pre_x_g, mem_norm_g=mem_norm_g,
             w_xq=w_xq, w_xkv=w_xkv, w_xo=w_xo, post_x_g=post_x_g, pre_ffn_g=pre_ffn_g,
             w_gu=w_gu, w_down=w_down, post_ffn_g=post_ffn_g)
    layers = [_layer_weights(l, p) for l in range(w_in.shape[0])]
    return (_trunk(x_prompt, mem_prompt, layers), _trunk(x_sample, mem_sample, layers))
```

```python
import functools

import jax
import jax.numpy as jnp
from jax import lax
from jax.experimental import pallas as pl
from jax.experimental.pallas import tpu as pltpu

D_MODEL = 1024
A_WIDTH = 512
A_GROUPS = 4
A_GROUP_DIM = 128
A_CHUNK = 128
B_WIDTH = 512
B_HEADS = 4
B_DK = 128
MEM_TOKENS = 256
X_HEADS = 4
X_HEAD_DIM = 256
D_FF = 2816
EPS = 1e-6

SEQ_TILE = 1024
ROW_BLOCK = 256
SCAN_CHUNK = 64
SUB_BLOCK = 16
FACTORED_DECAY_LIMIT = 55.0
VMEM_LIMIT_BYTES = 60 * 1024 * 1024

_BF = jnp.bfloat16
_F32 = jnp.float32


def _dot(a, b):
    return jnp.dot(a.astype(_BF), b.astype(_BF), preferred_element_type=_F32)


def _dot_tb(a, b):
    return lax.dot_general(a.astype(_BF), b.astype(_BF), (((1,), (1,)), ((), ())),
                           preferred_element_type=_F32)


def _dot_ta(a, b):
    return lax.dot_general(a.astype(_BF), b.astype(_BF), (((0,), (0,)), ((), ())),
                           preferred_element_type=_F32)


def _rms(x, g):
    return x * lax.rsqrt(jnp.mean(x * x, axis=-1, keepdims=True) + EPS) * g


def _gelu(x):
    return 0.5 * x * (1.0 + lax.erf(x * (2.0 ** -0.5)))


def _silu(x):
    return x * (0.5 * jnp.tanh(0.5 * x) + 0.5)


def _lower_bound(lbraw_ref, layer):
    depth = lbraw_ref.shape[0]
    rows = [lbraw_ref[j:j + 1, :] for j in range(depth)]
    m = functools.reduce(jnp.maximum, rows)
    e = [jnp.exp(r - m) for r in rows]
    denom = functools.reduce(lambda a, b: a + b, e)
    return functools.reduce(lambda a, b: a + b, e[1:layer + 1]) / denom


def _skewed(order, *stages):
    order = list(order)
    vals = {}
    for i in range(len(order) + len(stages) - 1):
        for k, stage in enumerate(stages):
            if 0 <= i - k < len(order):
                r = order[i - k]
                vals[(k, r)] = stage(r, vals.pop((k - 1, r), None))


def _row_block(r):
    return slice(r * ROW_BLOCK, (r + 1) * ROW_BLOCK)


def _split3(x):
    hi = x.astype(_BF)
    r1 = x - hi.astype(_F32)
    mid = r1.astype(_BF)
    lo = (r1 - mid.astype(_F32)).astype(_BF)
    return hi, mid, lo


def _cumulative(g, fwd):
    n = g.shape[0]
    row = lax.broadcasted_iota(jnp.int32, (n, n), 0)
    col = lax.broadcasted_iota(jnp.int32, (n, n), 1)
    tri = jnp.where(col <= row if fwd else col >= row, 1.0, 0.0).astype(_BF)
    g_hi, g_mid, g_lo = _split3(g)
    return (jnp.dot(tri, g_hi, preferred_element_type=_F32)
            + jnp.dot(tri, g_mid, preferred_element_type=_F32)
            + jnp.dot(tri, g_lo, preferred_element_type=_F32))


def _hgrn_gates(f, rows, lb, scan_scratch):
    k_s, g_s = scan_scratch[1], scan_scratch[3]
    e = jnp.exp(-jnp.abs(f))
    if lb is None:
        g = jnp.minimum(f, 0.0) - jnp.log(1.0 + e)
        k_s[rows, :] = jnp.where(f >= 0.0, e, 1.0) / (1.0 + e)
    else:
        forget = lb + (1.0 - lb) * (jnp.where(f >= 0.0, 1.0, e) / (1.0 + e))
        g = jnp.log(forget)
        k_s[rows, :] = 1.0 - forget
    g_s[rows, :] = g
    return g


def _hgrn_cumulative(g, rows, scan_scratch, fwd):
    b_s = scan_scratch[4]
    for c in range(ROW_BLOCK // SCAN_CHUNK):
        lo = c * SCAN_CHUNK
        b_s[rows.start + lo:rows.start + lo + SCAN_CHUNK, :] = _cumulative(g[lo:lo + SCAN_CHUNK], fwd)


def _scan_exact_loop(scan_scratch, st_s, fwd, emit):
    q_s, k_s, v_s, g_s, b_s, kc_s, bc_s = scan_scratch[:7]
    C = SCAN_CHUNK
    n_chunks = q_s.shape[0] // C
    n_sub = C // SUB_BLOCK
    sub_row = lax.broadcasted_iota(jnp.int32, (SUB_BLOCK, C), 0)
    sub_col = lax.broadcasted_iota(jnp.int32, (SUB_BLOCK, C), 1)
    last = C - 1 if fwd else 0

    def chunk_body(c, carry):
        idx = c if fwd else n_chunks - 1 - c
        base = pl.multiple_of(idx * C, C)
        rows = pl.ds(base, C)
        bc_s[...] = b_s[rows, :]
        kc_s[...] = k_s[rows, :]
        for hd in range(B_HEADS):
            lanes = slice(hd * B_DK, (hd + 1) * B_DK)
            q = q_s[rows, lanes]
            k = kc_s[:, lanes]
            v = v_s[rows, lanes]
            b = bc_s[:, lanes]
            st = st_s[hd]
            b_tot = b[last:last + 1]
            o = _dot_tb(q * jnp.exp(b), st)
            kd = k * jnp.exp(b_tot - b)
            st_s[hd] = st * jnp.exp(b_tot) + _dot_ta(v, kd)

            a_rows = []
            for i in range(n_sub):
                lo, hi = i * SUB_BLOCK, (i + 1) * SUB_BLOCK
                qi, bi = q[lo:hi], b[lo:hi]
                if fwd and i > 0:
                    r = b[lo - 1:lo]
                    ke = k[:lo] * jnp.exp(r - b[:lo])
                    ke = jnp.concatenate([ke, jnp.zeros((C - lo, B_DK), _F32)], axis=0)
                    a_i = _dot_tb(qi * jnp.exp(bi - r), ke)
                elif (not fwd) and i < n_sub - 1:
                    r = b[hi:hi + 1]
                    ke = k[hi:] * jnp.exp(r - b[hi:])
                    ke = jnp.concatenate([jnp.zeros((hi, B_DK), _F32), ke], axis=0)
                    a_i = _dot_tb(qi * jnp.exp(bi - r), ke)
                else:
                    a_i = jnp.zeros((SUB_BLOCK, C), _F32)
                for s in range(SUB_BLOCK):
                    j = lo + s
                    k_row = kc_s[j:j + 1, lanes]
                    b_row = bc_s[j:j + 1, lanes]
                    p = qi * k_row * jnp.exp(jnp.minimum(bi - b_row, 0.0))
                    a_i = jnp.where(sub_col == j, jnp.sum(p, axis=-1, keepdims=True), a_i)
                local = sub_col - lo
                if fwd:
                    keep = (sub_col < lo) | ((sub_col < hi) & (sub_row >= local))
                else:
                    keep = (sub_col >= hi) | ((sub_col >= lo) & (sub_row <= local))
                a_rows.append(jnp.where(keep, a_i, 0.0))
            a = jnp.concatenate(a_rows, axis=0)
            emit(rows, lanes, o + _dot(a, v))
        return carry

    lax.fori_loop(0, n_chunks, chunk_body, 0)


def _scan_reference_rows(fwd):
    return (SCAN_CHUNK // 2 - 1, SCAN_CHUNK - 1) if fwd else (SCAN_CHUNK // 2, 0)


def _scan_factored_stages(scan_scratch, st_s, fwd, emit):
    q_s, k_s, v_s, g_s, b_s, kc_s, bc_s, x_s, y_s, a_s, u_s = scan_scratch
    C = SCAN_CHUNK
    mid, last = _scan_reference_rows(fwd)
    row = lax.broadcasted_iota(jnp.int32, (C, C), 0)
    col = lax.broadcasted_iota(jnp.int32, (C, C), 1)
    causal = (col <= row) if fwd else (col >= row)
    chunk_rows = lambda c: slice(c * C, (c + 1) * C)
    head_lanes = lambda hd: slice(hd * B_DK, (hd + 1) * B_DK)

    def chunks(r):
        per_block = ROW_BLOCK // C
        ids = list(range(r * per_block, (r + 1) * per_block))
        return ids if fwd else ids[::-1]

    def scale(r, _):
        for c in chunks(r):
            rows = chunk_rows(c)
            b = b_s[rows, :]
            e = jnp.exp(b - b[mid:mid + 1])
            x_s[rows, :] = (q_s[rows, :] * e).astype(_BF)
            y_s[rows, :] = (k_s[rows, :] / e).astype(_BF)

    def products(r, _):
        for c in chunks(r):
            rows = chunk_rows(c)
            for hd in range(B_HEADS):
                lanes = head_lanes(hd)
                yh = y_s[rows, lanes]
                a_s[hd, rows, :] = jnp.where(causal, _dot_tb(x_s[rows, lanes], yh), 0.0).astype(_BF)
                u_s[c, hd] = _dot_ta(v_s[rows, lanes], yh)

    def states(r, _):
        for hd in range(B_HEADS):
            lanes = head_lanes(hd)
            st = st_s[hd]
            for c in chunks(r):
                rho = b_s[c * C + mid:c * C + mid + 1, lanes]
                b_tot = b_s[c * C + last:c * C + last + 1, lanes]
                u = u_s[c, hd]
                u_s[c, hd] = st * jnp.exp(rho)
                st = st * jnp.exp(b_tot) + u * jnp.exp(b_tot - rho)
            st_s[hd] = st

    def outputs(r, _):
        for c in chunks(r):
            rows = chunk_rows(c)
            for hd in range(B_HEADS):
                lanes = head_lanes(hd)
                emit(rows, lanes, _dot_tb(x_s[rows, lanes], u_s[c, hd])
                     + _dot(a_s[hd, rows, :], v_s[rows, lanes]))

    return scale, products, states, outputs


def _scan_factored_ok(scan_scratch, fwd):
    b_s = scan_scratch[4]
    C = SCAN_CHUNK
    mid, last = _scan_reference_rows(fwd)
    worst = jnp.zeros((1, B_WIDTH), _F32)
    for c in range(b_s.shape[0] // C):
        b_mid = b_s[c * C + mid:c * C + mid + 1, :]
        b_tot = b_s[c * C + last:c * C + last + 1, :]
        worst = jnp.maximum(worst, jnp.maximum(-b_mid, b_mid - b_tot))
    return jnp.max(worst) < FACTORED_DECAY_LIMIT


def _begin_tile(st_s, st0_s):
    @pl.when(pl.program_id(1) == 0)
    def _():
        st_s[...] = jnp.zeros_like(st_s)
    st0_s[...] = st_s[...]


def _bwd_kernel(layer, x_ref, gpre_ref, w_ref, lbraw_ref, o_ref, q_ref, v_ref, *scratch):
    (k_s, g_s), rest, (st_s, st0_s) = scratch[:2], scratch[2:-2], scratch[-2:]
    scan_scratch = (q_ref.at[0], k_s, v_ref.at[0], g_s) + tuple(rest)
    _begin_tile(st_s, st0_s)
    lb = _lower_bound(lbraw_ref, layer) if layer > 0 else None

    def project(r, _):
        h = _rms(x_ref[0, _row_block(r), :], gpre_ref[...]).astype(_BF)
        return jnp.dot(h, w_ref[...], preferred_element_type=_F32)

    def gates(r, p):
        rows = _row_block(r)
        q_ref[0, rows, :] = _silu(p[:, 0:512]) * (B_DK ** -0.5)
        v_ref[0, rows, :] = p[:, 1024:1536].astype(_BF)
        return _hgrn_gates(p[:, 512:1024], rows, lb, scan_scratch)

    def cumulate(r, g):
        _hgrn_cumulative(g, _row_block(r), scan_scratch, fwd=False)

    def emit(rows, lanes, o):
        o_ref[0, rows, lanes] = o

    blocks = reversed(range(x_ref.shape[1] // ROW_BLOCK))
    _skewed(blocks, project, gates, cumulate, *_scan_factored_stages(scan_scratch, st_s, False, emit))

    @pl.when(jnp.logical_not(_scan_factored_ok(scan_scratch, False)))
    def _():
        st_s[...] = st0_s[...]
        _scan_exact_loop(scan_scratch, st_s, False, emit)


def _mix_kernel(layer, x_ref, obwd_ref, q_ref, v_ref, gpre_ref, w_ref, vlng_ref, vlnb_ref, wsp_ref,
                bsp_ref, lbraw_ref, onorm_ref, wout_ref, gpost_ref, out_ref, *scratch):
    (k_s, g_s), rest, (gate_s, st_s, st0_s, cat_s) = scratch[:2], scratch[2:-4], scratch[-4:]
    scan_scratch = (q_ref.at[0], k_s, v_ref.at[0], g_s) + tuple(rest)
    _begin_tile(st_s, st0_s)
    lb = _lower_bound(lbraw_ref, layer) if layer > 0 else None
    blocks = range(x_ref.shape[1] // ROW_BLOCK)

    def project(r, _):
        h = _rms(x_ref[0, _row_block(r), :], gpre_ref[...]).astype(_BF)
        return jnp.dot(h, w_ref[...], preferred_element_type=_F32)

    def elementwise(r, p):
        rows = _row_block(r)
        u = _gelu(p[:, 0:512])
        v = _gelu(p[:, 512:1024])
        vc = v - jnp.mean(v, axis=-1, keepdims=True)
        v = vc * lax.rsqrt(jnp.mean(vc * vc, axis=-1, keepdims=True) + EPS) * vlng_ref[...] + vlnb_ref[...]
        g = _hgrn_gates(p[:, 1024:1536], rows, lb, scan_scratch)
        gate_s[rows, :] = _silu(p[:, 1536:2048])
        return u, v.astype(_BF), g

    def dependent_matmuls(r, uvg):
        rows = _row_block(r)
        u, v, g = uvg
        for n in range(ROW_BLOCK // A_CHUNK):
            chunk = slice(n * A_CHUNK, (n + 1) * A_CHUNK)
            for gi in range(A_GROUPS):
                lanes = slice(gi * A_GROUP_DIM, (gi + 1) * A_GROUP_DIM)
                sv = jnp.dot(wsp_ref[gi], v[chunk, lanes], preferred_element_type=_F32) + bsp_ref[gi]
                cat_s[rows.start + chunk.start:rows.start + chunk.stop, lanes] = (
                    u[chunk, lanes] * sv).astype(_BF)
        _hgrn_cumulative(g, rows, scan_scratch, fwd=True)

    def emit(rows, lanes, o):
        o_sum = o + obwd_ref[0, rows, lanes]
        cat_s[rows, A_WIDTH + lanes.start:A_WIDTH + lanes.stop] = (
            _rms(o_sum, onorm_ref[...]) * gate_s[rows, lanes]).astype(_BF)

    def out_proj(r, _):
        return jnp.dot(cat_s[_row_block(r), :], wout_ref[...], preferred_element_type=_F32)

    def residual(r, y):
        rows = _row_block(r)
        out_ref[0, rows, :] = x_ref[0, rows, :] + _rms(y, gpost_ref[...])

    _skewed(blocks, project, elementwise, dependent_matmuls,
            *_scan_factored_stages(scan_scratch, st_s, True, emit), out_proj, residual)

    @pl.when(jnp.logical_not(_scan_factored_ok(scan_scratch, True)))
    def _():
        st_s[...] = st0_s[...]
        _scan_exact_loop(scan_scratch, st_s, True, emit)
        _skewed(blocks, out_proj, residual)


def _kv_kernel(mem_ref, g_ref, w_ref, k_ref, v_ref):
    m = _rms(mem_ref[0], g_ref[...])
    kv = _dot(m, w_ref[...])
    k_ref[0] = kv[:, :D_MODEL].astype(_BF)
    v_ref[0] = kv[:, D_MODEL:].astype(_BF)


def _xattn_kernel(x_ref, k_ref, v_ref, gpre_ref, wq_ref, wo_ref, gpost_ref, out_ref, cat_s):
    head_lanes = lambda hd: slice(hd * X_HEAD_DIM, (hd + 1) * X_HEAD_DIM)

    def query(r, _):
        h = _rms(x_ref[0, _row_block(r), :], gpre_ref[...])
        return _dot(h, wq_ref[...]).astype(_BF)

    def scores(r, q):
        return [_dot_tb(q[:, head_lanes(hd)], k_ref[0, :, head_lanes(hd)]) for hd in range(X_HEADS)]

    def softmax(r, s_heads):
        p_heads = []
        for s in s_heads:
            s = s * (X_HEAD_DIM ** -0.5)
            e = jnp.exp(s - jnp.max(s, axis=-1, keepdims=True))
            p_heads.append((e / jnp.sum(e, axis=-1, keepdims=True)).astype(_BF))
        return p_heads

    def values(r, p_heads):
        for hd, p in enumerate(p_heads):
            cat_s[_row_block(r), head_lanes(hd)] = _dot(p, v_ref[0, :, head_lanes(hd)]).astype(_BF)

    def out_proj(r, _):
        return jnp.dot(cat_s[_row_block(r), :], wo_ref[...], preferred_element_type=_F32)

    def residual(r, y):
        rows = _row_block(r)
        out_ref[0, rows, :] = x_ref[0, rows, :] + _rms(y, gpost_ref[...])

    _skewed(range(x_ref.shape[1] // ROW_BLOCK), query, scores, softmax, values, out_proj, residual)


def _ffn_kernel(x_ref, gpre_ref, wgu_ref, wdown_ref, gpost_ref, out_ref):
    def gate_up(r, _):
        h = _rms(x_ref[0, _row_block(r), :], gpre_ref[...]).astype(_BF)
        return jnp.dot(h, wgu_ref[...], preferred_element_type=_F32)

    def activate(r, p):
        return (_silu(p[:, :D_FF]) * p[:, D_FF:]).astype(_BF)

    def down(r, act):
        return jnp.dot(act, wdown_ref[...], preferred_element_type=_F32)

    def residual(r, y):
        rows = _row_block(r)
        out_ref[0, rows, :] = x_ref[0, rows, :] + _rms(y, gpost_ref[...])

    _skewed(range(x_ref.shape[1] // ROW_BLOCK), gate_up, activate, down, residual)


def _params():
    return pltpu.CompilerParams(dimension_semantics=("arbitrary", "arbitrary"),
                                vmem_limit_bytes=VMEM_LIMIT_BYTES)


def _const_spec(shape):
    return pl.BlockSpec(shape, lambda b, t: (0,) * len(shape), pipeline_mode=pl.Buffered(1))


def _tile_spec(width, rev_tiles=None):
    if rev_tiles is None:
        return pl.BlockSpec((1, SEQ_TILE, width), lambda b, t: (b, t, 0))
    return pl.BlockSpec((1, SEQ_TILE, width), lambda b, t: (b, rev_tiles - 1 - t, 0))


def _scan_scratch():
    tile = lambda dt: pltpu.VMEM((SEQ_TILE, B_WIDTH), dt)
    chunk = lambda: pltpu.VMEM((SCAN_CHUNK, B_WIDTH), _F32)
    return [tile(_F32), tile(_F32), tile(_F32), chunk(), chunk(), tile(_BF), tile(_BF),
            pltpu.VMEM((B_HEADS, SEQ_TILE, SCAN_CHUNK), _BF),
            pltpu.VMEM((SEQ_TILE // SCAN_CHUNK, B_HEADS, B_DK, B_DK), _F32)]


def _state_scratch():
    return pltpu.VMEM((B_HEADS, B_DK, B_DK), _F32)


def _bwd_call(x, layer, gpre, w_bwd, lbraw):
    bn, ln, _ = x.shape
    nt = ln // SEQ_TILE
    return pl.pallas_call(
        functools.partial(_bwd_kernel, layer),
        grid=(bn, nt),
        in_specs=[_tile_spec(D_MODEL, nt), _const_spec(gpre.shape), _const_spec(w_bwd.shape),
                  _const_spec(lbraw.shape)],
        out_specs=[_tile_spec(B_WIDTH, nt)] * 3,
        out_shape=[jax.ShapeDtypeStruct((bn, ln, B_WIDTH), _F32),
                   jax.ShapeDtypeStruct((bn, ln, B_WIDTH), _F32),
                   jax.ShapeDtypeStruct((bn, ln, B_WIDTH), _BF)],
        scratch_shapes=_scan_scratch() + [_state_scratch(), _state_scratch()],
        compiler_params=_params(),
        name=f"hgrn_bwd_l{layer}",
    )(x, gpre, w_bwd, lbraw)


def _mix_call(x, obwd, q, v, layer, gpre, w_mix, vlng, vlnb, wsp, bsp, lbraw, onorm, wout, gpost):
    bn, ln, _ = x.shape
    nt = ln // SEQ_TILE
    consts = (gpre, w_mix, vlng, vlnb, wsp, bsp, lbraw, onorm, wout, gpost)
    return pl.pallas_call(
        functools.partial(_mix_kernel, layer),
        grid=(bn, nt),
        in_specs=[_tile_spec(D_MODEL)] + [_tile_spec(B_WIDTH)] * 3 + [_const_spec(c.shape) for c in consts],
        out_specs=_tile_spec(D_MODEL),
        out_shape=jax.ShapeDtypeStruct(x.shape, _F32),
        scratch_shapes=_scan_scratch() + [pltpu.VMEM((SEQ_TILE, B_WIDTH), _F32), _state_scratch(),
                                          _state_scratch(),
                                          pltpu.VMEM((SEQ_TILE, D_MODEL), _BF)],
        compiler_params=_params(),
        name=f"token_mix_l{layer}",
    )(x, obwd, q, v, *consts)


def _kv_call(mem, layer, g, w):
    bn = mem.shape[0]
    spec = pl.BlockSpec((1, MEM_TOKENS, D_MODEL), lambda b, t: (b, 0, 0))
    return pl.pallas_call(
        _kv_kernel,
        grid=(bn, 1),
        in_specs=[spec, _const_spec(g.shape), _const_spec(w.shape)],
        out_specs=[spec, spec],
        out_shape=[jax.ShapeDtypeStruct(mem.shape, _BF)] * 2,
        compiler_params=_params(),
        name=f"mem_kv_l{layer}",
    )(mem, g, w)


def _xattn_call(x, k, v, layer, gpre, wq, wo, gpost):
    bn, ln, _ = x.shape
    nt = ln // SEQ_TILE
    mem_spec = pl.BlockSpec((1, MEM_TOKENS, D_MODEL), lambda b, t: (b, 0, 0))
    consts = (gpre, wq, wo, gpost)
    return pl.pallas_call(
        _xattn_kernel,
        grid=(bn, nt),
        in_specs=[_tile_spec(D_MODEL), mem_spec, mem_spec] + [_const_spec(c.shape) for c in consts],
        out_specs=_tile_spec(D_MODEL),
        out_shape=jax.ShapeDtypeStruct(x.shape, _F32),
        scratch_shapes=[pltpu.VMEM((SEQ_TILE, D_MODEL), _BF)],
        compiler_params=_params(),
        name=f"mem_xattn_l{layer}",
    )(x, k, v, *consts)


def _ffn_call(x, layer, gpre, wgu, wdown, gpost):
    bn, ln, _ = x.shape
    nt = ln // SEQ_TILE
    consts = (gpre, wgu, wdown, gpost)
    return pl.pallas_call(
        _ffn_kernel,
        grid=(bn, nt),
        in_specs=[_tile_spec(D_MODEL)] + [_const_spec(c.shape) for c in consts],
        out_specs=_tile_spec(D_MODEL),
        out_shape=jax.ShapeDtypeStruct(x.shape, _F32),
        compiler_params=_params(),
        name=f"swiglu_ffn_l{layer}",
    )(x, *consts)


def _layer_weights(l, p):
    w_in = p['w_in'][l].astype(_BF)
    cols = lambda a, b: w_in[:, a:b]
    row = lambda a: a[l][None, :]
    return dict(
        w_mix=jnp.concatenate([cols(0, 1024), cols(1536, 2048), cols(3072, 3584)], axis=1),
        w_bwd=jnp.concatenate([cols(1024, 1536), cols(2048, 3072)], axis=1),
        pre_mix_g=row(p['pre_mix_g']), v_ln_g=row(p['v_ln_g']), v_ln_b=row(p['v_ln_b']),
        w_spatial=p['w_spatial'][l].astype(_BF),
        b_spatial=jnp.broadcast_to(p['b_spatial'][l][:, :, None], (A_GROUPS, A_CHUNK, A_GROUP_DIM)),
        lb_fwd=p['lb_raw'][0], lb_bwd=p['lb_raw'][1],
        onorm_g=row(p['onorm_g']), w_out=p['w_out'][l].astype(_BF), post_mix_g=row(p['post_mix_g']),
        pre_x_g=row(p['pre_x_g']), mem_norm_g=row(p['mem_norm_g']),
        w_xq=p['w_xq'][l].astype(_BF), w_xkv=p['w_xkv'][l].astype(_BF), w_xo=p['w_xo'][l].astype(_BF),
        post_x_g=row(p['post_x_g']), pre_ffn_g=row(p['pre_ffn_g']),
        w_gu=p['w_gu'][l].astype(_BF), w_down=p['w_down'][l].astype(_BF), post_ffn_g=row(p['post_ffn_g']),
    )


def _trunk(x, mem, layers):
    assert x.shape[1] % SEQ_TILE == 0 and x.shape[2] == D_MODEL
    assert SEQ_TILE % ROW_BLOCK == 0 and ROW_BLOCK % SCAN_CHUNK == 0 and ROW_BLOCK % A_CHUNK == 0
    for l, w in enumerate(layers):
        obwd, q, v = _bwd_call(x, l, w['pre_mix_g'], w['w_bwd'], w['lb_bwd'])
        x = _mix_call(x, obwd, q, v, l, w['pre_mix_g'], w['w_mix'], w['v_ln_g'], w['v_ln_b'], w['w_spatial'],
                      w['b_spatial'], w['lb_fwd'], w['onorm_g'], w['w_out'], w['post_mix_g'])
        k, v = _kv_call(mem, l, w['mem_norm_g'], w['w_xkv'])
        x = _xattn_call(x, k, v, l, w['pre_x_g'], w['w_xq'], w['w_xo'], w['post_x_g'])
        x = _ffn_call(x, l, w['pre_ffn_g'], w['w_gu'], w['w_down'], w['post_ffn_g'])
    return x


def kernel(x_prompt, x_sample, mem_prompt, mem_sample, pre_mix_g, w_in, v_ln_g, v_ln_b, w_spatial, b_spatial, lb_raw, onorm_g, w_out, post_mix_g, pre_x_g, mem_norm_g, w_xq, w_xkv, w_xo, post_x_g, pre_ffn_g, w_gu, w_down, post_ffn_g):
    p = dict(pre_mix_g=pre_mix_g, w_in=w_in, v_ln_g=v_ln_g, v_ln_b=v_ln_b,
             w_spatial=w_spatial, b_spatial=b_spatial, lb_raw=lb_raw, onorm_g=onorm_g,
             w_out=w_out, post_mix_g=post_mix_g, pre_x_g=pre_x_g, mem_norm_g=mem_norm_g,
             w_xq=w_xq, w_xkv=w_xkv, w_xo=w_xo, post_x_g=post_x_g, pre_ffn_g=pre_ffn_g,
             w_gu=w_gu, w_down=w_down, post_ffn_g=post_ffn_g)
    layers = [_layer_weights(l, p) for l in range(w_in.shape[0])]
    return (_trunk(x_prompt, mem_prompt, layers), _trunk(x_sample, mem_sample, layers))
```

```python
import functools

import jax
import jax.numpy as jnp
from jax import lax
from jax.experimental import pallas as pl
from jax.experimental.pallas import tpu as pltpu

D_MODEL = 1024
A_WIDTH = 512
A_GROUPS = 4
A_GROUP_DIM = 128
A_CHUNK = 128
B_WIDTH = 512
B_HEADS = 4
B_DK = 128
MEM_TOKENS = 256
X_HEADS = 4
X_HEAD_DIM = 256
D_FF = 2816
EPS = 1e-6

SEQ_TILE = 1024
XATTN_TILE = 2048
ROW_BLOCK = 256
SCAN_CHUNK = 64
SUB_BLOCK = 16
FACTORED_DECAY_LIMIT = 55.0
VMEM_LIMIT_BYTES = 60 * 1024 * 1024

_BF = jnp.bfloat16
_F32 = jnp.float32


def _dot(a, b):
    return jnp.dot(a.astype(_BF), b.astype(_BF), preferred_element_type=_F32)


def _dot_tb(a, b):
    return lax.dot_general(a.astype(_BF), b.astype(_BF), (((1,), (1,)), ((), ())),
                           preferred_element_type=_F32)


def _dot_ta(a, b):
    return lax.dot_general(a.astype(_BF), b.astype(_BF), (((0,), (0,)), ((), ())),
                           preferred_element_type=_F32)


def _rms(x, g):
    return x * lax.rsqrt(jnp.mean(x * x, axis=-1, keepdims=True) + EPS) * g


def _gelu(x):
    return 0.5 * x * (1.0 + lax.erf(x * (2.0 ** -0.5)))


def _silu(x):
    return x * (0.5 * jnp.tanh(0.5 * x) + 0.5)


def _lower_bound(lbraw_ref, layer):
    depth = lbraw_ref.shape[0]
    rows = [lbraw_ref[j:j + 1, :] for j in range(depth)]
    m = functools.reduce(jnp.maximum, rows)
    e = [jnp.exp(r - m) for r in rows]
    denom = functools.reduce(lambda a, b: a + b, e)
    return functools.reduce(lambda a, b: a + b, e[1:layer + 1]) / denom


def _skewed(order, *stages):
    order = list(order)
    vals = {}
    for i in range(len(order) + len(stages) - 1):
        for k, stage in enumerate(stages):
            if 0 <= i - k < len(order):
                r = order[i - k]
                vals[(k, r)] = stage(r, vals.pop((k - 1, r), None))


def _row_block(r):
    return slice(r * ROW_BLOCK, (r + 1) * ROW_BLOCK)


def _split3(x):
    hi = x.astype(_BF)
    r1 = x - hi.astype(_F32)
    mid = r1.astype(_BF)
    lo = (r1 - mid.astype(_F32)).astype(_BF)
    return hi, mid, lo


def _cumulative(g, fwd):
    n = g.shape[0]
    row = lax.broadcasted_iota(jnp.int32, (n, n), 0)
    col = lax.broadcasted_iota(jnp.int32, (n, n), 1)
    tri = jnp.where(col <= row if fwd else col >= row, 1.0, 0.0).astype(_BF)
    g_hi, g_mid, g_lo = _split3(g)
    return (jnp.dot(tri, g_hi, preferred_element_type=_F32)
            + jnp.dot(tri, g_mid, preferred_element_type=_F32)
            + jnp.dot(tri, g_lo, preferred_element_type=_F32))


def _hgrn_gates(f, rows, lb, scan_scratch):
    k_s, g_s = scan_scratch[1], scan_scratch[3]
    if lb is None:
        e = jnp.exp(-jnp.abs(f))
        g = jnp.minimum(f, 0.0) - jnp.log(1.0 + e)
        k_s[rows, :] = jnp.where(f >= 0.0, e, 1.0) / (1.0 + e)
    else:
        forget = lb + (1.0 - lb) * (0.5 * jnp.tanh(0.5 * f) + 0.5)
        g = jnp.log(forget)
        k_s[rows, :] = 1.0 - forget
    g_s[rows, :] = g
    return g


def _hgrn_cumulative(g, rows, scan_scratch, fwd):
    b_s = scan_scratch[4]
    for c in range(ROW_BLOCK // SCAN_CHUNK):
        lo = c * SCAN_CHUNK
        b_s[rows.start + lo:rows.start + lo + SCAN_CHUNK, :] = _cumulative(g[lo:lo + SCAN_CHUNK], fwd)


def _scan_exact_loop(scan_scratch, st_s, fwd, emit):
    q_s, k_s, v_s, g_s, b_s, kc_s, bc_s = scan_scratch[:7]
    C = SCAN_CHUNK
    n_chunks = q_s.shape[0] // C
    n_sub = C // SUB_BLOCK
    sub_row = lax.broadcasted_iota(jnp.int32, (SUB_BLOCK, C), 0)
    sub_col = lax.broadcasted_iota(jnp.int32, (SUB_BLOCK, C), 1)
    last = C - 1 if fwd else 0

    def chunk_body(c, carry):
        idx = c if fwd else n_chunks - 1 - c
        base = pl.multiple_of(idx * C, C)
        rows = pl.ds(base, C)
        bc_s[...] = b_s[rows, :]
        kc_s[...] = k_s[rows, :]
        for hd in range(B_HEADS):
            lanes = slice(hd * B_DK, (hd + 1) * B_DK)
            q = q_s[rows, lanes]
            k = kc_s[:, lanes]
            v = v_s[rows, lanes]
            b = bc_s[:, lanes]
            st = st_s[hd]
            b_tot = b[last:last + 1]
            o = _dot_tb(q * jnp.exp(b), st)
            kd = k * jnp.exp(b_tot - b)
            st_s[hd] = st * jnp.exp(b_tot) + _dot_ta(v, kd)

            a_rows = []
            for i in range(n_sub):
                lo, hi = i * SUB_BLOCK, (i + 1) * SUB_BLOCK
                qi, bi = q[lo:hi], b[lo:hi]
                if fwd and i > 0:
                    r = b[lo - 1:lo]
                    ke = k[:lo] * jnp.exp(r - b[:lo])
                    ke = jnp.concatenate([ke, jnp.zeros((C - lo, B_DK), _F32)], axis=0)
                    a_i = _dot_tb(qi * jnp.exp(bi - r), ke)
                elif (not fwd) and i < n_sub - 1:
                    r = b[hi:hi + 1]
                    ke = k[hi:] * jnp.exp(r - b[hi:])
                    ke = jnp.concatenate([jnp.zeros((hi, B_DK), _F32), ke], axis=0)
                    a_i = _dot_tb(qi * jnp.exp(bi - r), ke)
                else:
                    a_i = jnp.zeros((SUB_BLOCK, C), _F32)
                for s in range(SUB_BLOCK):
                    j = lo + s
                    k_row = kc_s[j:j + 1, lanes]
                    b_row = bc_s[j:j + 1, lanes]
                    p = qi * k_row * jnp.exp(jnp.minimum(bi - b_row, 0.0))
                    a_i = jnp.where(sub_col == j, jnp.sum(p, axis=-1, keepdims=True), a_i)
                local = sub_col - lo
                if fwd:
                    keep = (sub_col < lo) | ((sub_col < hi) & (sub_row >= local))
                else:
                    keep = (sub_col >= hi) | ((sub_col >= lo) & (sub_row <= local))
                a_rows.append(jnp.where(keep, a_i, 0.0))
            a = jnp.concatenate(a_rows, axis=0)
            emit(rows, lanes, o + _dot(a, v))
        return carry

    lax.fori_loop(0, n_chunks, chunk_body, 0)


def _scan_reference_rows(fwd):
    return (SCAN_CHUNK // 2 - 1, SCAN_CHUNK - 1) if fwd else (SCAN_CHUNK // 2, 0)


def _scan_factored_stages(scan_scratch, st_s, fwd, emit):
    q_s, k_s, v_s, g_s, b_s, kc_s, bc_s, x_s, y_s, a_s, u_s = scan_scratch
    C = SCAN_CHUNK
    mid, last = _scan_reference_rows(fwd)
    row = lax.broadcasted_iota(jnp.int32, (C, C), 0)
    col = lax.broadcasted_iota(jnp.int32, (C, C), 1)
    causal = (col <= row) if fwd else (col >= row)
    chunk_rows = lambda c: slice(c * C, (c + 1) * C)
    head_lanes = lambda hd: slice(hd * B_DK, (hd + 1) * B_DK)

    def chunks(r):
        per_block = ROW_BLOCK // C
        ids = list(range(r * per_block, (r + 1) * per_block))
        return ids if fwd else ids[::-1]

    def scale(r, _):
        for c in chunks(r):
            rows = chunk_rows(c)
            b = b_s[rows, :]
            e = jnp.exp(b - b[mid:mid + 1])
            x_s[rows, :] = (q_s[rows, :] * e).astype(_BF)
            y_s[rows, :] = (k_s[rows, :] / e).astype(_BF)

    def products(r, _):
        for c in chunks(r):
            rows = chunk_rows(c)
            for hd in range(B_HEADS):
                lanes = head_lanes(hd)
                yh = y_s[rows, lanes]
                a_s[hd, rows, :] = jnp.where(causal, _dot_tb(x_s[rows, lanes], yh), 0.0).astype(_BF)
                u_s[c, hd] = _dot_ta(v_s[rows, lanes], yh)

    def states(r, _):
        for hd in range(B_HEADS):
            lanes = head_lanes(hd)
            st = st_s[hd]
            for c in chunks(r):
                rho = b_s[c * C + mid:c * C + mid + 1, lanes]
                b_tot = b_s[c * C + last:c * C + last + 1, lanes]
                u = u_s[c, hd]
                u_s[c, hd] = st * jnp.exp(rho)
                st = st * jnp.exp(b_tot) + u * jnp.exp(b_tot - rho)
            st_s[hd] = st

    def outputs(r, _):
        for c in chunks(r):
            rows = chunk_rows(c)
            for hd in range(B_HEADS):
                lanes = head_lanes(hd)
                emit(rows, lanes, _dot_tb(x_s[rows, lanes], u_s[c, hd])
                     + _dot(a_s[hd, rows, :], v_s[rows, lanes]))

    return scale, products, states, outputs


def _scan_factored_ok(scan_scratch, fwd):
    b_s = scan_scratch[4]
    C = SCAN_CHUNK
    mid, last = _scan_reference_rows(fwd)
    worst = jnp.zeros((1, B_WIDTH), _F32)
    for c in range(b_s.shape[0] // C):
        b_mid = b_s[c * C + mid:c * C + mid + 1, :]
        b_tot = b_s[c * C + last:c * C + last + 1, :]
        worst = jnp.maximum(worst, jnp.maximum(-b_mid, b_mid - b_tot))
    return jnp.max(worst) < FACTORED_DECAY_LIMIT


def _begin_tile(st_s, st0_s):
    @pl.when(pl.program_id(1) == 0)
    def _():
        st_s[...] = jnp.zeros_like(st_s)
    st0_s[...] = st_s[...]


def _bwd_kernel(layer, x_ref, gpre_ref, w_ref, lbraw_ref, o_ref, q_ref, v_ref, *scratch):
    (k_s, g_s), rest, (st_s, st0_s) = scratch[:2], scratch[2:-2], scratch[-2:]
    scan_scratch = (q_ref.at[0], k_s, v_ref.at[0], g_s) + tuple(rest)
    _begin_tile(st_s, st0_s)
    lb = _lower_bound(lbraw_ref, layer) if layer > 0 else None

    def project(r, _):
        h = _rms(x_ref[0, _row_block(r), :], gpre_ref[...]).astype(_BF)
        return jnp.dot(h, w_ref[...], preferred_element_type=_F32)

    def gates(r, p):
        rows = _row_block(r)
        q_ref[0, rows, :] = _silu(p[:, 0:512]) * (B_DK ** -0.5)
        v_ref[0, rows, :] = p[:, 1024:1536].astype(_BF)
        return _hgrn_gates(p[:, 512:1024], rows, lb, scan_scratch)

    def cumulate(r, g):
        _hgrn_cumulative(g, _row_block(r), scan_scratch, fwd=False)

    def emit(rows, lanes, o):
        o_ref[0, rows, lanes] = o

    blocks = reversed(range(x_ref.shape[1] // ROW_BLOCK))
    _skewed(blocks, project, gates, cumulate, *_scan_factored_stages(scan_scratch, st_s, False, emit))

    @pl.when(jnp.logical_not(_scan_factored_ok(scan_scratch, False)))
    def _():
        st_s[...] = st0_s[...]
        _scan_exact_loop(scan_scratch, st_s, False, emit)


def _mix_kernel(layer, x_ref, obwd_ref, q_ref, v_ref, gpre_ref, w_ref, vlng_ref, vlnb_ref, wsp_ref,
                bsp_ref, lbraw_ref, onorm_ref, wout_ref, gpost_ref, out_ref, *scratch):
    (k_s, g_s), rest, (gate_s, st_s, st0_s, cat_s) = scratch[:2], scratch[2:-4], scratch[-4:]
    scan_scratch = (q_ref.at[0], k_s, v_ref.at[0], g_s) + tuple(rest)
    _begin_tile(st_s, st0_s)
    lb = _lower_bound(lbraw_ref, layer) if layer > 0 else None
    blocks = range(x_ref.shape[1] // ROW_BLOCK)

    def project(r, _):
        h = _rms(x_ref[0, _row_block(r), :], gpre_ref[...]).astype(_BF)
        return jnp.dot(h, w_ref[...], preferred_element_type=_F32)

    def elementwise(r, p):
        rows = _row_block(r)
        u = _gelu(p[:, 0:512])
        v = _gelu(p[:, 512:1024])
        vc = v - jnp.mean(v, axis=-1, keepdims=True)
        v = vc * lax.rsqrt(jnp.mean(vc * vc, axis=-1, keepdims=True) + EPS) * vlng_ref[...] + vlnb_ref[...]
        g = _hgrn_gates(p[:, 1024:1536], rows, lb, scan_scratch)
        gate_s[rows, :] = _silu(p[:, 1536:2048])
        return u, v.astype(_BF), g

    def dependent_matmuls(r, uvg):
        rows = _row_block(r)
        u, v, g = uvg
        chunks = [slice(n * A_CHUNK, (n + 1) * A_CHUNK) for n in range(ROW_BLOCK // A_CHUNK)]
        for gi in range(A_GROUPS):
            lanes = slice(gi * A_GROUP_DIM, (gi + 1) * A_GROUP_DIM)
            sv = jnp.dot(wsp_ref[gi], jnp.concatenate([v[c, lanes] for c in chunks], axis=1),
                         preferred_element_type=_F32)
            for n, c in enumerate(chunks):
                cat_s[rows.start + c.start:rows.start + c.stop, lanes] = (
                    u[c, lanes] * (sv[:, n * A_GROUP_DIM:(n + 1) * A_GROUP_DIM] + bsp_ref[gi])).astype(_BF)
        _hgrn_cumulative(g, rows, scan_scratch, fwd=True)

    def emit(rows, lanes, o):
        o_sum = o + obwd_ref[0, rows, lanes]
        cat_s[rows, A_WIDTH + lanes.start:A_WIDTH + lanes.stop] = (
            _rms(o_sum, onorm_ref[...]) * gate_s[rows, lanes]).astype(_BF)

    def out_proj(r, _):
        return jnp.dot(cat_s[_row_block(r), :], wout_ref[...], preferred_element_type=_F32)

    def residual(r, y):
        rows = _row_block(r)
        out_ref[0, rows, :] = x_ref[0, rows, :] + _rms(y, gpost_ref[...])

    _skewed(blocks, project, elementwise, dependent_matmuls,
            *_scan_factored_stages(scan_scratch, st_s, True, emit), out_proj, residual)

    @pl.when(jnp.logical_not(_scan_factored_ok(scan_scratch, True)))
    def _():
        st_s[...] = st0_s[...]
        _scan_exact_loop(scan_scratch, st_s, True, emit)
        _skewed(blocks, out_proj, residual)


def _kv_kernel(mem_ref, g_ref, w_ref, k_ref, v_ref):
    m = _rms(mem_ref[0], g_ref[...])
    kv = _dot(m, w_ref[...])
    k_ref[0] = kv[:, :D_MODEL].astype(_BF)
    v_ref[0] = kv[:, D_MODEL:].astype(_BF)


def _xattn_kernel(x_ref, k_ref, v_ref, gpre_ref, wq_ref, wo_ref, gpost_ref, out_ref, cat_s):
    head_lanes = lambda hd: slice(hd * X_HEAD_DIM, (hd + 1) * X_HEAD_DIM)

    def query(r, _):
        h = _rms(x_ref[0, _row_block(r), :], gpre_ref[...])
        return _dot(h, wq_ref[...]).astype(_BF)

    def scores(r, q):
        return [_dot_tb(q[:, head_lanes(hd)], k_ref[0, :, head_lanes(hd)]) for hd in range(X_HEADS)]

    def softmax(r, s_heads):
        p_heads = []
        for s in s_heads:
            s = s * (X_HEAD_DIM ** -0.5)
            e = jnp.exp(s - jnp.max(s, axis=-1, keepdims=True))
            p_heads.append((e / jnp.sum(e, axis=-1, keepdims=True)).astype(_BF))
        return p_heads

    def values(r, p_heads):
        for hd, p in enumerate(p_heads):
            cat_s[_row_block(r), head_lanes(hd)] = _dot(p, v_ref[0, :, head_lanes(hd)]).astype(_BF)

    def out_proj(r, _):
        return jnp.dot(cat_s[_row_block(r), :], wo_ref[...], preferred_element_type=_F32)

    def residual(r, y):
        rows = _row_block(r)
        out_ref[0, rows, :] = x_ref[0, rows, :] + _rms(y, gpost_ref[...])

    _skewed(range(x_ref.shape[1] // ROW_BLOCK), query, scores, softmax, values, out_proj, residual)


def _ffn_kernel(x_ref, gpre_ref, wgu_ref, wdown_ref, gpost_ref, out_ref):
    def gate_up(r, _):
        h = _rms(x_ref[0, _row_block(r), :], gpre_ref[...]).astype(_BF)
        return jnp.dot(h, wgu_ref[...], preferred_element_type=_F32)

    def activate(r, p):
        return (_silu(p[:, :D_FF]) * p[:, D_FF:]).astype(_BF)

    def down(r, act):
        return jnp.dot(act, wdown_ref[...], preferred_element_type=_F32)

    def residual(r, y):
        rows = _row_block(r)
        out_ref[0, rows, :] = x_ref[0, rows, :] + _rms(y, gpost_ref[...])

    _skewed(range(x_ref.shape[1] // ROW_BLOCK), gate_up, activate, down, residual)


def _params():
    return pltpu.CompilerParams(dimension_semantics=("arbitrary", "arbitrary"),
                                vmem_limit_bytes=VMEM_LIMIT_BYTES)


def _const_spec(shape):
    return pl.BlockSpec(shape, lambda b, t: (0,) * len(shape), pipeline_mode=pl.Buffered(1))


def _tile_spec(width, rev_tiles=None, tile=SEQ_TILE):
    if rev_tiles is None:
        return pl.BlockSpec((1, tile, width), lambda b, t: (b, t, 0))
    return pl.BlockSpec((1, tile, width), lambda b, t: (b, rev_tiles - 1 - t, 0))


def _scan_scratch():
    tile = lambda dt: pltpu.VMEM((SEQ_TILE, B_WIDTH), dt)
    chunk = lambda: pltpu.VMEM((SCAN_CHUNK, B_WIDTH), _F32)
    return [tile(_F32), tile(_F32), tile(_F32), chunk(), chunk(), tile(_BF), tile(_BF),
            pltpu.VMEM((B_HEADS, SEQ_TILE, SCAN_CHUNK), _BF),
            pltpu.VMEM((SEQ_TILE // SCAN_CHUNK, B_HEADS, B_DK, B_DK), _F32)]


def _state_scratch():
    return pltpu.VMEM((B_HEADS, B_DK, B_DK), _F32)


def _bwd_call(x, layer, gpre, w_bwd, lbraw):
    bn, ln, _ = x.shape
    nt = ln // SEQ_TILE
    return pl.pallas_call(
        functools.partial(_bwd_kernel, layer),
        grid=(bn, nt),
        in_specs=[_tile_spec(D_MODEL, nt), _const_spec(gpre.shape), _const_spec(w_bwd.shape),
                  _const_spec(lbraw.shape)],
        out_specs=[_tile_spec(B_WIDTH, nt)] * 3,
        out_shape=[jax.ShapeDtypeStruct((bn, ln, B_WIDTH), _F32),
                   jax.ShapeDtypeStruct((bn, ln, B_WIDTH), _F32),
                   jax.ShapeDtypeStruct((bn, ln, B_WIDTH), _BF)],
        scratch_shapes=_scan_scratch() + [_state_scratch(), _state_scratch()],
        compiler_params=_params(),
        name=f"hgrn_bwd_l{layer}",
    )(x, gpre, w_bwd, lbraw)


def _mix_call(x, obwd, q, v, layer, gpre, w_mix, vlng, vlnb, wsp, bsp, lbraw, onorm, wout, gpost):
    bn, ln, _ = x.shape
    nt = ln // SEQ_TILE
    consts = (gpre, w_mix, vlng, vlnb, wsp, bsp, lbraw, onorm, wout, gpost)
    return pl.pallas_call(
        functools.partial(_mix_kernel, layer),
        grid=(bn, nt),
        in_specs=[_tile_spec(D_MODEL)] + [_tile_spec(B_WIDTH)] * 3 + [_const_spec(c.shape) for c in consts],
        out_specs=_tile_spec(D_MODEL),
        out_shape=jax.ShapeDtypeStruct(x.shape, _F32),
        scratch_shapes=_scan_scratch() + [pltpu.VMEM((SEQ_TILE, B_WIDTH), _F32), _state_scratch(),
                                          _state_scratch(),
                                          pltpu.VMEM((SEQ_TILE, D_MODEL), _BF)],
        compiler_params=_params(),
        name=f"token_mix_l{layer}",
    )(x, obwd, q, v, *consts)


def _kv_call(mem, layer, g, w):
    bn = mem.shape[0]
    spec = pl.BlockSpec((1, MEM_TOKENS, D_MODEL), lambda b, t: (b, 0, 0))
    return pl.pallas_call(
        _kv_kernel,
        grid=(bn, 1),
        in_specs=[spec, _const_spec(g.shape), _const_spec(w.shape)],
        out_specs=[spec, spec],
        out_shape=[jax.ShapeDtypeStruct(mem.shape, _BF)] * 2,
        compiler_params=_params(),
        name=f"mem_kv_l{layer}",
    )(mem, g, w)


def _xattn_call(x, k, v, layer, gpre, wq, wo, gpost):
    bn, ln, _ = x.shape
    nt = ln // XATTN_TILE
    mem_spec = pl.BlockSpec((1, MEM_TOKENS, D_MODEL), lambda b, t: (b, 0, 0))
    consts = (gpre, wq, wo, gpost)
    x_spec = _tile_spec(D_MODEL, tile=XATTN_TILE)
    return pl.pallas_call(
        _xattn_kernel,
        grid=(bn, nt),
        in_specs=[x_spec, mem_spec, mem_spec] + [_const_spec(c.shape) for c in consts],
        out_specs=x_spec,
        out_shape=jax.ShapeDtypeStruct(x.shape, _F32),
        scratch_shapes=[pltpu.VMEM((XATTN_TILE, D_MODEL), _BF)],
        compiler_params=_params(),
        name=f"mem_xattn_l{layer}",
    )(x, k, v, *consts)


def _ffn_call(x, layer, gpre, wgu, wdown, gpost):
    bn, ln, _ = x.shape
    nt = ln // SEQ_TILE
    consts = (gpre, wgu, wdown, gpost)
    return pl.pallas_call(
        _ffn_kernel,
        grid=(bn, nt),
        in_specs=[_tile_spec(D_MODEL)] + [_const_spec(c.shape) for c in consts],
        out_specs=_tile_spec(D_MODEL),
        out_shape=jax.ShapeDtypeStruct(x.shape, _F32),
        compiler_params=_params(),
        name=f"swiglu_ffn_l{layer}",
    )(x, *consts)


def _layer_weights(l, p):
    w_in = p['w_in'][l].astype(_BF)
    cols = lambda a, b: w_in[:, a:b]
    row = lambda a: a[l][None, :]
    return dict(
        w_mix=jnp.concatenate([cols(0, 1024), cols(1536, 2048), cols(3072, 3584)], axis=1),
        w_bwd=jnp.concatenate([cols(1024, 1536), cols(2048, 3072)], axis=1),
        pre_mix_g=row(p['pre_mix_g']), v_ln_g=row(p['v_ln_g']), v_ln_b=row(p['v_ln_b']),
        w_spatial=p['w_spatial'][l].astype(_BF),
        b_spatial=jnp.broadcast_to(p['b_spatial'][l][:, :, None], (A_GROUPS, A_CHUNK, A_GROUP_DIM)),
        lb_fwd=p['lb_raw'][0], lb_bwd=p['lb_raw'][1],
        onorm_g=row(p['onorm_g']), w_out=p['w_out'][l].astype(_BF), post_mix_g=row(p['post_mix_g']),
        pre_x_g=row(p['pre_x_g']), mem_norm_g=row(p['mem_norm_g']),
        w_xq=p['w_xq'][l].astype(_BF), w_xkv=p['w_xkv'][l].astype(_BF), w_xo=p['w_xo'][l].astype(_BF),
        post_x_g=row(p['post_x_g']), pre_ffn_g=row(p['pre_ffn_g']),
        w_gu=p['w_gu'][l].astype(_BF), w_down=p['w_down'][l].astype(_BF), post_ffn_g=row(p['post_ffn_g']),
    )


def _trunk(x, mem, layers):
    assert x.shape[1] % SEQ_TILE == 0 and x.shape[1] % XATTN_TILE == 0 and x.shape[2] == D_MODEL
    assert SEQ_TILE % ROW_BLOCK == 0 and ROW_BLOCK % SCAN_CHUNK == 0 and ROW_BLOCK % A_CHUNK == 0
    for l, w in enumerate(layers):
        obwd, q, v = _bwd_call(x, l, w['pre_mix_g'], w['w_bwd'], w['lb_bwd'])
        x = _mix_call(x, obwd, q, v, l, w['pre_mix_g'], w['w_mix'], w['v_ln_g'], w['v_ln_b'], w['w_spatial'],
                      w['b_spatial'], w['lb_fwd'], w['onorm_g'], w['w_out'], w['post_mix_g'])
        k, v = _kv_call(mem, l, w['mem_norm_g'], w['w_xkv'])
        x = _xattn_call(x, k, v, l, w['pre_x_g'], w['w_xq'], w['w_xo'], w['post_x_g'])
        x = _ffn_call(x, l, w['pre_ffn_g'], w['w_gu'], w['w_down'], w['post_ffn_g'])
    return x


def kernel(x_prompt, x_sample, mem_prompt, mem_sample, pre_mix_g, w_in, v_ln_g, v_ln_b, w_spatial, b_spatial, lb_raw, onorm_g, w_out, post_mix_g, pre_x_g, mem_norm_g, w_xq, w_xkv, w_xo, post_x_g, pre_ffn_g, w_gu, w_down, post_ffn_g):
    p = dict(pre_mix_g=pre_mix_g, w_in=w_in, v_ln_g=v_ln_g, v_ln_b=v_ln_b,
             w_spatial=w_spatial, b_spatial=b_spatial, lb_raw=lb_raw, onorm_g=onorm_g,
             w_out=w_out, post_mix_g=post_mix_g, pre_x_g=pre_x_g, mem_norm_g=mem_norm_g,
             w_xq=w_xq, w_xkv=w_xkv, w_xo=w_xo, post_x_g=post_x_g, pre_ffn_g=pre_ffn_g,
             w_gu=w_gu, w_down=w_down, post_ffn_g=post_ffn_g)
    layers = [_layer_weights(l, p) for l in range(w_in.shape[0])]
    return (_trunk(x_prompt, mem_prompt, layers), _trunk(x_sample, mem_sample, layers))
```

```python
import functools

import jax
import jax.numpy as jnp
from jax import lax
from jax.experimental import pallas as pl
from jax.experimental.pallas import tpu as pltpu

D_MODEL = 1024
A_WIDTH = 512
A_GROUPS = 4
A_GROUP_DIM = 128
A_CHUNK = 128
B_WIDTH = 512
B_HEADS = 4
B_DK = 128
MEM_TOKENS = 256
X_HEADS = 4
X_HEAD_DIM = 256
D_FF = 2816
EPS = 1e-6

SEQ_TILE = 1024
XATTN_TILE = 2048
ROW_BLOCK = 256
SCAN_CHUNK = 64
SUB_BLOCK = 16
FACTORED_DECAY_LIMIT = 55.0
VMEM_LIMIT_BYTES = 60 * 1024 * 1024

_BF = jnp.bfloat16
_F32 = jnp.float32


def _dot(a, b):
    return jnp.dot(a.astype(_BF), b.astype(_BF), preferred_element_type=_F32)


def _dot_tb(a, b):
    return lax.dot_general(a.astype(_BF), b.astype(_BF), (((1,), (1,)), ((), ())),
                           preferred_element_type=_F32)


def _dot_ta(a, b):
    return lax.dot_general(a.astype(_BF), b.astype(_BF), (((0,), (0,)), ((), ())),
                           preferred_element_type=_F32)


def _rms(x, g):
    return x * lax.rsqrt(jnp.mean(x * x, axis=-1, keepdims=True) + EPS) * g


def _gelu(x):
    return 0.5 * x * (1.0 + lax.erf(x * (2.0 ** -0.5)))


def _silu(x):
    return x * (0.5 * jnp.tanh(0.5 * x) + 0.5)


def _lower_bound(lbraw_ref, layer):
    depth = lbraw_ref.shape[0]
    rows = [lbraw_ref[j:j + 1, :] for j in range(depth)]
    m = functools.reduce(jnp.maximum, rows)
    e = [jnp.exp(r - m) for r in rows]
    denom = functools.reduce(lambda a, b: a + b, e)
    return functools.reduce(lambda a, b: a + b, e[1:layer + 1]) / denom


def _skewed(order, *stages):
    order = list(order)
    vals = {}
    for i in range(len(order) + len(stages) - 1):
        for k, stage in enumerate(stages):
            if 0 <= i - k < len(order):
                r = order[i - k]
                vals[(k, r)] = stage(r, vals.pop((k - 1, r), None))


def _row_block(r):
    return slice(r * ROW_BLOCK, (r + 1) * ROW_BLOCK)


def _split3(x):
    hi = x.astype(_BF)
    r1 = x - hi.astype(_F32)
    mid = r1.astype(_BF)
    lo = (r1 - mid.astype(_F32)).astype(_BF)
    return hi, mid, lo


def _cumulative(g, fwd):
    n = g.shape[0]
    row = lax.broadcasted_iota(jnp.int32, (n, n), 0)
    col = lax.broadcasted_iota(jnp.int32, (n, n), 1)
    tri = jnp.where(col <= row if fwd else col >= row, 1.0, 0.0).astype(_BF)
    g_hi, g_mid, g_lo = _split3(g)
    return (jnp.dot(tri, g_hi, preferred_element_type=_F32)
            + jnp.dot(tri, g_mid, preferred_element_type=_F32)
            + jnp.dot(tri, g_lo, preferred_element_type=_F32))


def _hgrn_gates(f, rows, lb, scan_scratch):
    k_s, g_s = scan_scratch[1], scan_scratch[3]
    if lb is None:
        e = jnp.exp(-jnp.abs(f))
        g = jnp.minimum(f, 0.0) - jnp.log(1.0 + e)
        k_s[rows, :] = jnp.where(f >= 0.0, e, 1.0) / (1.0 + e)
    else:
        forget = lb + (1.0 - lb) * (0.5 * jnp.tanh(0.5 * f) + 0.5)
        g = jnp.log(forget)
        k_s[rows, :] = 1.0 - forget
    g_s[rows, :] = g
    return g


def _hgrn_cumulative(g, rows, scan_scratch, fwd):
    b_s = scan_scratch[4]
    for c in range(ROW_BLOCK // SCAN_CHUNK):
        lo = c * SCAN_CHUNK
        b_s[rows.start + lo:rows.start + lo + SCAN_CHUNK, :] = _cumulative(g[lo:lo + SCAN_CHUNK], fwd)


def _scan_exact_loop(scan_scratch, st_s, fwd, emit):
    q_s, k_s, v_s, g_s, b_s, kc_s, bc_s = scan_scratch[:7]
    C = SCAN_CHUNK
    n_chunks = q_s.shape[0] // C
    n_sub = C // SUB_BLOCK
    sub_row = lax.broadcasted_iota(jnp.int32, (SUB_BLOCK, C), 0)
    sub_col = lax.broadcasted_iota(jnp.int32, (SUB_BLOCK, C), 1)
    last = C - 1 if fwd else 0

    def chunk_body(c, carry):
        idx = c if fwd else n_chunks - 1 - c
        base = pl.multiple_of(idx * C, C)
        rows = pl.ds(base, C)
        bc_s[...] = b_s[rows, :]
        kc_s[...] = k_s[rows, :]
        for hd in range(B_HEADS):
            lanes = slice(hd * B_DK, (hd + 1) * B_DK)
            q = q_s[rows, lanes]
            k = kc_s[:, lanes]
            v = v_s[rows, lanes]
            b = bc_s[:, lanes]
            st = st_s[hd]
            b_tot = b[last:last + 1]
            o = _dot_tb(q * jnp.exp(b), st)
            kd = k * jnp.exp(b_tot - b)
            st_s[hd] = st * jnp.exp(b_tot) + _dot_ta(v, kd)

            a_rows = []
            for i in range(n_sub):
                lo, hi = i * SUB_BLOCK, (i + 1) * SUB_BLOCK
                qi, bi = q[lo:hi], b[lo:hi]
                if fwd and i > 0:
                    r = b[lo - 1:lo]
                    ke = k[:lo] * jnp.exp(r - b[:lo])
                    ke = jnp.concatenate([ke, jnp.zeros((C - lo, B_DK), _F32)], axis=0)
                    a_i = _dot_tb(qi * jnp.exp(bi - r), ke)
                elif (not fwd) and i < n_sub - 1:
                    r = b[hi:hi + 1]
                    ke = k[hi:] * jnp.exp(r - b[hi:])
                    ke = jnp.concatenate([jnp.zeros((hi, B_DK), _F32), ke], axis=0)
                    a_i = _dot_tb(qi * jnp.exp(bi - r), ke)
                else:
                    a_i = jnp.zeros((SUB_BLOCK, C), _F32)
                for s in range(SUB_BLOCK):
                    j = lo + s
                    k_row = kc_s[j:j + 1, lanes]
                    b_row = bc_s[j:j + 1, lanes]
                    p = qi * k_row * jnp.exp(jnp.minimum(bi - b_row, 0.0))
                    a_i = jnp.where(sub_col == j, jnp.sum(p, axis=-1, keepdims=True), a_i)
                local = sub_col - lo
                if fwd:
                    keep = (sub_col < lo) | ((sub_col < hi) & (sub_row >= local))
                else:
                    keep = (sub_col >= hi) | ((sub_col >= lo) & (sub_row <= local))
                a_rows.append(jnp.where(keep, a_i, 0.0))
            a = jnp.concatenate(a_rows, axis=0)
            emit(rows, lanes, o + _dot(a, v))
        return carry

    lax.fori_loop(0, n_chunks, chunk_body, 0)


def _scan_reference_rows(fwd):
    return (SCAN_CHUNK // 2 - 1, SCAN_CHUNK - 1) if fwd else (SCAN_CHUNK // 2, 0)


def _scan_factored_stages(scan_scratch, st_s, fwd, emit):
    q_s, k_s, v_s, g_s, b_s, kc_s, bc_s, x_s, y_s, a_s, u_s = scan_scratch
    C = SCAN_CHUNK
    mid, last = _scan_reference_rows(fwd)
    row = lax.broadcasted_iota(jnp.int32, (C, C), 0)
    col = lax.broadcasted_iota(jnp.int32, (C, C), 1)
    causal = (col <= row) if fwd else (col >= row)
    chunk_rows = lambda c: slice(c * C, (c + 1) * C)
    head_lanes = lambda hd: slice(hd * B_DK, (hd + 1) * B_DK)

    def chunks(r):
        per_block = ROW_BLOCK // C
        ids = list(range(r * per_block, (r + 1) * per_block))
        return ids if fwd else ids[::-1]

    def scale(r, _):
        for c in chunks(r):
            rows = chunk_rows(c)
            b = b_s[rows, :]
            e = jnp.exp(b - b[mid:mid + 1])
            x_s[rows, :] = (q_s[rows, :] * e).astype(_BF)
            y_s[rows, :] = (k_s[rows, :] / e).astype(_BF)

    def products(r, _):
        for c in chunks(r):
            rows = chunk_rows(c)
            for hd in range(B_HEADS):
                lanes = head_lanes(hd)
                yh = y_s[rows, lanes]
                a_s[hd, rows, :] = jnp.where(causal, _dot_tb(x_s[rows, lanes], yh), 0.0).astype(_BF)
                u_s[c, hd] = _dot_ta(v_s[rows, lanes], yh)

    def states(r, _):
        for hd in range(B_HEADS):
            lanes = head_lanes(hd)
            st = st_s[hd]
            for c in chunks(r):
                rho = b_s[c * C + mid:c * C + mid + 1, lanes]
                b_tot = b_s[c * C + last:c * C + last + 1, lanes]
                u = u_s[c, hd]
                u_s[c, hd] = st * jnp.exp(rho)
                st = st * jnp.exp(b_tot) + u * jnp.exp(b_tot - rho)
            st_s[hd] = st

    def outputs(r, _):
        for c in chunks(r):
            rows = chunk_rows(c)
            for hd in range(B_HEADS):
                lanes = head_lanes(hd)
                emit(rows, lanes, _dot_tb(x_s[rows, lanes], u_s[c, hd])
                     + _dot(a_s[hd, rows, :], v_s[rows, lanes]))

    return scale, products, states, outputs


def _scan_factored_ok(scan_scratch, fwd):
    b_s = scan_scratch[4]
    C = SCAN_CHUNK
    mid, last = _scan_reference_rows(fwd)
    worst = jnp.zeros((1, B_WIDTH), _F32)
    for c in range(b_s.shape[0] // C):
        b_mid = b_s[c * C + mid:c * C + mid + 1, :]
        b_tot = b_s[c * C + last:c * C + last + 1, :]
        worst = jnp.maximum(worst, jnp.maximum(-b_mid, b_mid - b_tot))
    return jnp.max(worst) < FACTORED_DECAY_LIMIT


def _begin_tile(st_s, st0_s):
    @pl.when(pl.program_id(1) == 0)
    def _():
        st_s[...] = jnp.zeros_like(st_s)
    st0_s[...] = st_s[...]


def _bwd_kernel(layer, x_ref, gpre_ref, w_ref, lbraw_ref, o_ref, q_ref, v_ref, *scratch):
    (k_s, g_s), rest, (st_s, st0_s) = scratch[:2], scratch[2:-2], scratch[-2:]
    scan_scratch = (q_ref.at[0], k_s, v_ref.at[0], g_s) + tuple(rest)
    _begin_tile(st_s, st0_s)
    lb = _lower_bound(lbraw_ref, layer) if layer > 0 else None

    def project(r, _):
        h = _rms(x_ref[0, _row_block(r), :], gpre_ref[...]).astype(_BF)
        return jnp.dot(h, w_ref[...], preferred_element_type=_F32)

    def gates(r, p):
        rows = _row_block(r)
        q_ref[0, rows, :] = _silu(p[:, 0:512]) * (B_DK ** -0.5)
        v_ref[0, rows, :] = p[:, 1024:1536].astype(_BF)
        return _hgrn_gates(p[:, 512:1024], rows, lb, scan_scratch)

    def cumulate(r, g):
        _hgrn_cumulative(g, _row_block(r), scan_scratch, fwd=False)

    def emit(rows, lanes, o):
        o_ref[0, rows, lanes] = o.astype(o_ref.dtype)

    blocks = reversed(range(x_ref.shape[1] // ROW_BLOCK))
    _skewed(blocks, project, gates, cumulate, *_scan_factored_stages(scan_scratch, st_s, False, emit))

    @pl.when(jnp.logical_not(_scan_factored_ok(scan_scratch, False)))
    def _():
        st_s[...] = st0_s[...]
        _scan_exact_loop(scan_scratch, st_s, False, emit)


def _mix_kernel(layer, x_ref, obwd_ref, q_ref, v_ref, gpre_ref, w_ref, vlng_ref, vlnb_ref, wsp_ref,
                bsp_ref, lbraw_ref, onorm_ref, wout_ref, gpost_ref, out_ref, *scratch):
    (k_s, g_s), rest, (gate_s, st_s, st0_s, cat_s) = scratch[:2], scratch[2:-4], scratch[-4:]
    scan_scratch = (q_ref.at[0], k_s, v_ref.at[0], g_s) + tuple(rest)
    _begin_tile(st_s, st0_s)
    lb = _lower_bound(lbraw_ref, layer) if layer > 0 else None
    blocks = range(x_ref.shape[1] // ROW_BLOCK)

    def project(r, _):
        h = _rms(x_ref[0, _row_block(r), :], gpre_ref[...]).astype(_BF)
        return jnp.dot(h, w_ref[...], preferred_element_type=_F32)

    def elementwise(r, p):
        rows = _row_block(r)
        u = _gelu(p[:, 0:512])
        v = _gelu(p[:, 512:1024])
        vc = v - jnp.mean(v, axis=-1, keepdims=True)
        v = vc * lax.rsqrt(jnp.mean(vc * vc, axis=-1, keepdims=True) + EPS) * vlng_ref[...] + vlnb_ref[...]
        g = _hgrn_gates(p[:, 1024:1536], rows, lb, scan_scratch)
        gate_s[rows, :] = _silu(p[:, 1536:2048])
        return u, v.astype(_BF), g

    def dependent_matmuls(r, uvg):
        rows = _row_block(r)
        u, v, g = uvg
        chunks = [slice(n * A_CHUNK, (n + 1) * A_CHUNK) for n in range(ROW_BLOCK // A_CHUNK)]
        for gi in range(A_GROUPS):
            lanes = slice(gi * A_GROUP_DIM, (gi + 1) * A_GROUP_DIM)
            sv = jnp.dot(wsp_ref[gi], jnp.concatenate([v[c, lanes] for c in chunks], axis=1),
                         preferred_element_type=_F32)
            for n, c in enumerate(chunks):
                cat_s[rows.start + c.start:rows.start + c.stop, lanes] = (
                    u[c, lanes] * (sv[:, n * A_GROUP_DIM:(n + 1) * A_GROUP_DIM] + bsp_ref[gi])).astype(_BF)
        _hgrn_cumulative(g, rows, scan_scratch, fwd=True)

    def emit(rows, lanes, o):
        o_sum = o + obwd_ref[0, rows, lanes]
        cat_s[rows, A_WIDTH + lanes.start:A_WIDTH + lanes.stop] = (
            _rms(o_sum, onorm_ref[...]) * gate_s[rows, lanes]).astype(_BF)

    def out_proj(r, _):
        return jnp.dot(cat_s[_row_block(r), :], wout_ref[...], preferred_element_type=_F32)

    def residual(r, y):
        rows = _row_block(r)
        out_ref[0, rows, :] = x_ref[0, rows, :] + _rms(y, gpost_ref[...])

    _skewed(blocks, project, elementwise, dependent_matmuls,
            *_scan_factored_stages(scan_scratch, st_s, True, emit), out_proj, residual)

    @pl.when(jnp.logical_not(_scan_factored_ok(scan_scratch, True)))
    def _():
        st_s[...] = st0_s[...]
        _scan_exact_loop(scan_scratch, st_s, True, emit)
        _skewed(blocks, out_proj, residual)


def _kv_kernel(mem_ref, g_ref, w_ref, k_ref, v_ref):
    m = _rms(mem_ref[0], g_ref[...])
    kv = _dot(m, w_ref[...])
    k_ref[0] = kv[:, :D_MODEL].astype(_BF)
    v_ref[0] = kv[:, D_MODEL:].astype(_BF)


def _xattn_kernel(x_ref, k_ref, v_ref, gpre_ref, wq_ref, wo_ref, gpost_ref, out_ref, cat_s):
    head_lanes = lambda hd: slice(hd * X_HEAD_DIM, (hd + 1) * X_HEAD_DIM)

    def query(r, _):
        h = _rms(x_ref[0, _row_block(r), :], gpre_ref[...])
        return _dot(h, wq_ref[...]).astype(_BF)

    def scores(r, q):
        return [_dot_tb(q[:, head_lanes(hd)], k_ref[0, :, head_lanes(hd)]) for hd in range(X_HEADS)]

    def softmax(r, s_heads):
        p_heads = []
        for s in s_heads:
            s = s * (X_HEAD_DIM ** -0.5)
            e = jnp.exp(s - jnp.max(s, axis=-1, keepdims=True))
            p_heads.append((e / jnp.sum(e, axis=-1, keepdims=True)).astype(_BF))
        return p_heads

    def values(r, p_heads):
        for hd, p in enumerate(p_heads):
            cat_s[_row_block(r), head_lanes(hd)] = _dot(p, v_ref[0, :, head_lanes(hd)]).astype(_BF)

    def out_proj(r, _):
        return jnp.dot(cat_s[_row_block(r), :], wo_ref[...], preferred_element_type=_F32)

    def residual(r, y):
        rows = _row_block(r)
        out_ref[0, rows, :] = x_ref[0, rows, :] + _rms(y, gpost_ref[...])

    _skewed(range(x_ref.shape[1] // ROW_BLOCK), query, scores, softmax, values, out_proj, residual)


def _ffn_kernel(x_ref, gpre_ref, wgu_ref, wdown_ref, gpost_ref, out_ref):
    def gate_up(r, _):
        h = _rms(x_ref[0, _row_block(r), :], gpre_ref[...]).astype(_BF)
        return jnp.dot(h, wgu_ref[...], preferred_element_type=_F32)

    def activate(r, p):
        return (_silu(p[:, :D_FF]) * p[:, D_FF:]).astype(_BF)

    def down(r, act):
        return jnp.dot(act, wdown_ref[...], preferred_element_type=_F32)

    def residual(r, y):
        rows = _row_block(r)
        out_ref[0, rows, :] = x_ref[0, rows, :] + _rms(y, gpost_ref[...])

    _skewed(range(x_ref.shape[1] // ROW_BLOCK), gate_up, activate, down, residual)


def _params():
    return pltpu.CompilerParams(dimension_semantics=("arbitrary", "arbitrary"),
                                vmem_limit_bytes=VMEM_LIMIT_BYTES)


def _const_spec(shape):
    return pl.BlockSpec(shape, lambda b, t: (0,) * len(shape), pipeline_mode=pl.Buffered(1))


def _tile_spec(width, rev_tiles=None, tile=SEQ_TILE):
    if rev_tiles is None:
        return pl.BlockSpec((1, tile, width), lambda b, t: (b, t, 0))
    return pl.BlockSpec((1, tile, width), lambda b, t: (b, rev_tiles - 1 - t, 0))


def _scan_scratch():
    tile = lambda dt: pltpu.VMEM((SEQ_TILE, B_WIDTH), dt)
    chunk = lambda: pltpu.VMEM((SCAN_CHUNK, B_WIDTH), _F32)
    return [tile(_F32), tile(_F32), tile(_F32), chunk(), chunk(), tile(_BF), tile(_BF),
            pltpu.VMEM((B_HEADS, SEQ_TILE, SCAN_CHUNK), _BF),
            pltpu.VMEM((SEQ_TILE // SCAN_CHUNK, B_HEADS, B_DK, B_DK), _F32)]


def _state_scratch():
    return pltpu.VMEM((B_HEADS, B_DK, B_DK), _F32)


def _bwd_call(x, layer, gpre, w_bwd, lbraw):
    bn, ln, _ = x.shape
    nt = ln // SEQ_TILE
    return pl.pallas_call(
        functools.partial(_bwd_kernel, layer),
        grid=(bn, nt),
        in_specs=[_tile_spec(D_MODEL, nt), _const_spec(gpre.shape), _const_spec(w_bwd.shape),
                  _const_spec(lbraw.shape)],
        out_specs=[_tile_spec(B_WIDTH, nt)] * 3,
        out_shape=[jax.ShapeDtypeStruct((bn, ln, B_WIDTH), _BF),
                   jax.ShapeDtypeStruct((bn, ln, B_WIDTH), _F32),
                   jax.ShapeDtypeStruct((bn, ln, B_WIDTH), _BF)],
        scratch_shapes=_scan_scratch() + [_state_scratch(), _state_scratch()],
        compiler_params=_params(),
        name=f"hgrn_bwd_l{layer}",
    )(x, gpre, w_bwd, lbraw)


def _mix_call(x, obwd, q, v, layer, gpre, w_mix, vlng, vlnb, wsp, bsp, lbraw, onorm, wout, gpost):
    bn, ln, _ = x.shape
    nt = ln // SEQ_TILE
    consts = (gpre, w_mix, vlng, vlnb, wsp, bsp, lbraw, onorm, wout, gpost)
    return pl.pallas_call(
        functools.partial(_mix_kernel, layer),
        grid=(bn, nt),
        in_specs=[_tile_spec(D_MODEL)] + [_tile_spec(B_WIDTH)] * 3 + [_const_spec(c.shape) for c in consts],
        out_specs=_tile_spec(D_MODEL),
        out_shape=jax.ShapeDtypeStruct(x.shape, _F32),
        scratch_shapes=_scan_scratch() + [pltpu.VMEM((SEQ_TILE, B_WIDTH), _F32), _state_scratch(),
                                          _state_scratch(),
                                          pltpu.VMEM((SEQ_TILE, D_MODEL), _BF)],
        compiler_params=_params(),
        name=f"token_mix_l{layer}",
    )(x, obwd, q, v, *consts)


def _kv_call(mem, layer, g, w):
    bn = mem.shape[0]
    spec = pl.BlockSpec((1, MEM_TOKENS, D_MODEL), lambda b, t: (b, 0, 0))
    return pl.pallas_call(
        _kv_kernel,
        grid=(bn, 1),
        in_specs=[spec, _const_spec(g.shape), _const_spec(w.shape)],
        out_specs=[spec, spec],
        out_shape=[jax.ShapeDtypeStruct(mem.shape, _BF)] * 2,
        compiler_params=_params(),
        name=f"mem_kv_l{layer}",
    )(mem, g, w)


def _xattn_call(x, k, v, layer, gpre, wq, wo, gpost):
    bn, ln, _ = x.shape
    nt = ln // XATTN_TILE
    mem_spec = pl.BlockSpec((1, MEM_TOKENS, D_MODEL), lambda b, t: (b, 0, 0))
    consts = (gpre, wq, wo, gpost)
    x_spec = _tile_spec(D_MODEL, tile=XATTN_TILE)
    return pl.pallas_call(
        _xattn_kernel,
        grid=(bn, nt),
        in_specs=[x_spec, mem_spec, mem_spec] + [_const_spec(c.shape) for c in consts],
        out_specs=x_spec,
        out_shape=jax.ShapeDtypeStruct(x.shape, _F32),
        scratch_shapes=[pltpu.VMEM((XATTN_TILE, D_MODEL), _BF)],
        compiler_params=_params(),
        name=f"mem_xattn_l{layer}",
    )(x, k, v, *consts)


def _ffn_call(x, layer, gpre, wgu, wdown, gpost):
    bn, ln, _ = x.shape
    nt = ln // SEQ_TILE
    consts = (gpre, wgu, wdown, gpost)
    return pl.pallas_call(
        _ffn_kernel,
        grid=(bn, nt),
        in_specs=[_tile_spec(D_MODEL)] + [_const_spec(c.shape) for c in consts],
        out_specs=_tile_spec(D_MODEL),
        out_shape=jax.ShapeDtypeStruct(x.shape, _F32),
        compiler_params=_params(),
        name=f"swiglu_ffn_l{layer}",
    )(x, *consts)


def _layer_weights(l, p):
    w_in = p['w_in'][l].astype(_BF)
    cols = lambda a, b: w_in[:, a:b]
    row = lambda a: a[l][None, :]
    return dict(
        w_mix=jnp.concatenate([cols(0, 1024), cols(1536, 2048), cols(3072, 3584)], axis=1),
        w_bwd=jnp.concatenate([cols(1024, 1536), cols(2048, 3072)], axis=1),
        pre_mix_g=row(p['pre_mix_g']), v_ln_g=row(p['v_ln_g']), v_ln_b=row(p['v_ln_b']),
        w_spatial=p['w_spatial'][l].astype(_BF),
        b_spatial=jnp.broadcast_to(p['b_spatial'][l][:, :, None], (A_GROUPS, A_CHUNK, A_GROUP_DIM)),
        lb_fwd=p['lb_raw'][0], lb_bwd=p['lb_raw'][1],
        onorm_g=row(p['onorm_g']), w_out=p['w_out'][l].astype(_BF), post_mix_g=row(p['post_mix_g']),
        pre_x_g=row(p['pre_x_g']), mem_norm_g=row(p['mem_norm_g']),
        w_xq=p['w_xq'][l].astype(_BF), w_xkv=p['w_xkv'][l].astype(_BF), w_xo=p['w_xo'][l].astype(_BF),
        post_x_g=row(p['post_x_g']), pre_ffn_g=row(p['pre_ffn_g']),
        w_gu=p['w_gu'][l].astype(_BF), w_down=p['w_down'][l].astype(_BF), post_ffn_g=row(p['post_ffn_g']),
    )


def _trunk(x, mem, layers):
    assert x.shape[1] % SEQ_TILE == 0 and x.shape[1] % XATTN_TILE == 0 and x.shape[2] == D_MODEL
    assert SEQ_TILE % ROW_BLOCK == 0 and ROW_BLOCK % SCAN_CHUNK == 0 and ROW_BLOCK % A_CHUNK == 0
    for l, w in enumerate(layers):
        obwd, q, v = _bwd_call(x, l, w['pre_mix_g'], w['w_bwd'], w['lb_bwd'])
        x = _mix_call(x, obwd, q, v, l, w['pre_mix_g'], w['w_mix'], w['v_ln_g'], w['v_ln_b'], w['w_spatial'],
                      w['b_spatial'], w['lb_fwd'], w['onorm_g'], w['w_out'], w['post_mix_g'])
        k, v = _kv_call(mem, l, w['mem_norm_g'], w['w_xkv'])
        x = _xattn_call(x, k, v, l, w['pre_x_g'], w['w_xq'], w['w_xo'], w['post_x_g'])
        x = _ffn_call(x, l, w['pre_ffn_g'], w['w_gu'], w['w_down'], w['post_ffn_g'])
    return x


def kernel(x_prompt, x_sample, mem_prompt, mem_sample, pre_mix_g, w_in, v_ln_g, v_ln_b, w_spatial, b_spatial, lb_raw, onorm_g, w_out, post_mix_g, pre_x_g, mem_norm_g, w_xq, w_xkv, w_xo, post_x_g, pre_ffn_g, w_gu, w_down, post_ffn_g):
    p = dict(pre_mix_g=pre_mix_g, w_in=w_in, v_ln_g=v_ln_g, v_ln_b=v_ln_b,
             w_spatial=w_spatial, b_spatial=b_spatial, lb_raw=lb_raw, onorm_g=onorm_g,
             w_out=w_out, post_mix_g=post_mix_g, pre_x_g=pre_x_g, mem_norm_g=mem_norm_g,
             w_xq=w_xq, w_xkv=w_xkv, w_xo=w_xo, post_x_g=post_x_g, pre_ffn_g=pre_ffn_g,
             w_gu=w_gu, w_down=w_down, post_ffn_g=post_ffn_g)
    layers = [_layer_weights(l, p) for l in range(w_in.shape[0])]
    return (_trunk(x_prompt, mem_prompt, layers), _trunk(x_sample, mem_sample, layers))
```

```python
import functools

import jax
import jax.numpy as jnp
from jax import lax
from jax.experimental import pallas as pl
from jax.experimental.pallas import tpu as pltpu

D_MODEL = 1024
A_WIDTH = 512
A_GROUPS = 4
A_GROUP_DIM = 128
A_CHUNK = 128
B_WIDTH = 512
B_HEADS = 4
B_DK = 128
MEM_TOKENS = 256
X_HEADS = 4
X_HEAD_DIM = 256
D_FF = 2816
EPS = 1e-6

SEQ_TILE = 1024
XATTN_TILE = 2048
MIX_TILE = 512
ROW_BLOCK = 256
SCAN_CHUNK = 64
SUB_BLOCK = 16
FACTORED_DECAY_LIMIT = 55.0
VMEM_LIMIT_BYTES = 60 * 1024 * 1024

_BF = jnp.bfloat16
_F32 = jnp.float32


def _dot(a, b):
    return jnp.dot(a.astype(_BF), b.astype(_BF), preferred_element_type=_F32)


def _dot_tb(a, b):
    return lax.dot_general(a.astype(_BF), b.astype(_BF), (((1,), (1,)), ((), ())),
                           preferred_element_type=_F32)


def _dot_ta(a, b):
    return lax.dot_general(a.astype(_BF), b.astype(_BF), (((0,), (0,)), ((), ())),
                           preferred_element_type=_F32)


def _rms(x, g):
    return x * lax.rsqrt(jnp.mean(x * x, axis=-1, keepdims=True) + EPS) * g


def _gelu(x):
    return 0.5 * x * (1.0 + lax.erf(x * (2.0 ** -0.5)))


def _silu(x):
    return x * (0.5 * jnp.tanh(0.5 * x) + 0.5)


def _lower_bound(lbraw_ref, layer):
    depth = lbraw_ref.shape[0]
    rows = [lbraw_ref[j:j + 1, :] for j in range(depth)]
    m = functools.reduce(jnp.maximum, rows)
    e = [jnp.exp(r - m) for r in rows]
    denom = functools.reduce(lambda a, b: a + b, e)
    return functools.reduce(lambda a, b: a + b, e[1:layer + 1]) / denom


def _skewed_steps(order, *stages):
    order = list(order)
    vals = {}

    def make_step(i):
        def step():
            for k, stage in enumerate(stages):
                if 0 <= i - k < len(order):
                    r = order[i - k]
                    vals[(k, r)] = stage(r, vals.pop((k - 1, r), None))
        return step

    return [make_step(i) for i in range(len(order) + len(stages) - 1)]


def _interleaved(*pipelines):
    for i in range(max(len(p) for p in pipelines)):
        for p in pipelines:
            if i < len(p):
                p[i]()


def _skewed(order, *stages):
    _interleaved(_skewed_steps(order, *stages))


def _row_block(r):
    return slice(r * ROW_BLOCK, (r + 1) * ROW_BLOCK)


def _split3(x):
    hi = x.astype(_BF)
    r1 = x - hi.astype(_F32)
    mid = r1.astype(_BF)
    lo = (r1 - mid.astype(_F32)).astype(_BF)
    return hi, mid, lo


def _cumulative(g, fwd):
    n = g.shape[0]
    row = lax.broadcasted_iota(jnp.int32, (n, n), 0)
    col = lax.broadcasted_iota(jnp.int32, (n, n), 1)
    tri = jnp.where(col <= row if fwd else col >= row, 1.0, 0.0).astype(_BF)
    g_hi, g_mid, g_lo = _split3(g)
    return (jnp.dot(tri, g_hi, preferred_element_type=_F32)
            + jnp.dot(tri, g_mid, preferred_element_type=_F32)
            + jnp.dot(tri, g_lo, preferred_element_type=_F32))


def _hgrn_gates(f, rows, lb, scan_scratch):
    k_s, g_s = scan_scratch[1], scan_scratch[3]
    if lb is None:
        e = jnp.exp(-jnp.abs(f))
        g = jnp.minimum(f, 0.0) - jnp.log(1.0 + e)
        k_s[rows, :] = jnp.where(f >= 0.0, e, 1.0) / (1.0 + e)
    else:
        forget = lb + (1.0 - lb) * (0.5 * jnp.tanh(0.5 * f) + 0.5)
        g = jnp.log(forget)
        k_s[rows, :] = 1.0 - forget
    g_s[rows, :] = g
    return g


def _hgrn_cumulative(g, rows, scan_scratch, fwd):
    b_s = scan_scratch[4]
    for c in range(ROW_BLOCK // SCAN_CHUNK):
        lo = c * SCAN_CHUNK
        b_s[rows.start + lo:rows.start + lo + SCAN_CHUNK, :] = _cumulative(g[lo:lo + SCAN_CHUNK], fwd)


def _scan_exact_loop(scan_scratch, st_s, fwd, emit):
    q_s, k_s, v_s, g_s, b_s, kc_s, bc_s = scan_scratch[:7]
    C = SCAN_CHUNK
    n_chunks = q_s.shape[0] // C
    n_sub = C // SUB_BLOCK
    sub_row = lax.broadcasted_iota(jnp.int32, (SUB_BLOCK, C), 0)
    sub_col = lax.broadcasted_iota(jnp.int32, (SUB_BLOCK, C), 1)
    last = C - 1 if fwd else 0

    def chunk_body(c, carry):
        idx = c if fwd else n_chunks - 1 - c
        base = pl.multiple_of(idx * C, C)
        rows = pl.ds(base, C)
        bc_s[...] = b_s[rows, :]
        kc_s[...] = k_s[rows, :]
        for hd in range(B_HEADS):
            lanes = slice(hd * B_DK, (hd + 1) * B_DK)
            q = q_s[rows, lanes]
            k = kc_s[:, lanes]
            v = v_s[rows, lanes]
            b = bc_s[:, lanes]
            st = st_s[hd]
            b_tot = b[last:last + 1]
            o = _dot_tb(q * jnp.exp(b), st)
            kd = k * jnp.exp(b_tot - b)
            st_s[hd] = st * jnp.exp(b_tot) + _dot_ta(v, kd)

            a_rows = []
            for i in range(n_sub):
                lo, hi = i * SUB_BLOCK, (i + 1) * SUB_BLOCK
                qi, bi = q[lo:hi], b[lo:hi]
                if fwd and i > 0:
                    r = b[lo - 1:lo]
                    ke = k[:lo] * jnp.exp(r - b[:lo])
                    ke = jnp.concatenate([ke, jnp.zeros((C - lo, B_DK), _F32)], axis=0)
                    a_i = _dot_tb(qi * jnp.exp(bi - r), ke)
                elif (not fwd) and i < n_sub - 1:
                    r = b[hi:hi + 1]
                    ke = k[hi:] * jnp.exp(r - b[hi:])
                    ke = jnp.concatenate([jnp.zeros((hi, B_DK), _F32), ke], axis=0)
                    a_i = _dot_tb(qi * jnp.exp(bi - r), ke)
                else:
                    a_i = jnp.zeros((SUB_BLOCK, C), _F32)
                for s in range(SUB_BLOCK):
                    j = lo + s
                    k_row = kc_s[j:j + 1, lanes]
                    b_row = bc_s[j:j + 1, lanes]
                    p = qi * k_row * jnp.exp(jnp.minimum(bi - b_row, 0.0))
                    a_i = jnp.where(sub_col == j, jnp.sum(p, axis=-1, keepdims=True), a_i)
                local = sub_col - lo
                if fwd:
                    keep = (sub_col < lo) | ((sub_col < hi) & (sub_row >= local))
                else:
                    keep = (sub_col >= hi) | ((sub_col >= lo) & (sub_row <= local))
                a_rows.append(jnp.where(keep, a_i, 0.0))
            a = jnp.concatenate(a_rows, axis=0)
            emit(rows, lanes, o + _dot(a, v))
        return carry

    lax.fori_loop(0, n_chunks, chunk_body, 0)


def _scan_reference_rows(fwd):
    return (SCAN_CHUNK // 2 - 1, SCAN_CHUNK - 1) if fwd else (SCAN_CHUNK // 2, 0)


def _scan_factored_stages(scan_scratch, st_s, fwd, emit):
    q_s, k_s, v_s, g_s, b_s, kc_s, bc_s, x_s, y_s, a_s, u_s = scan_scratch
    C = SCAN_CHUNK
    mid, last = _scan_reference_rows(fwd)
    row = lax.broadcasted_iota(jnp.int32, (C, C), 0)
    col = lax.broadcasted_iota(jnp.int32, (C, C), 1)
    causal = (col <= row) if fwd else (col >= row)
    chunk_rows = lambda c: slice(c * C, (c + 1) * C)
    head_lanes = lambda hd: slice(hd * B_DK, (hd + 1) * B_DK)

    def chunks(r):
        per_block = ROW_BLOCK // C
        ids = list(range(r * per_block, (r + 1) * per_block))
        return ids if fwd else ids[::-1]

    def scale(r, _):
        for c in chunks(r):
            rows = chunk_rows(c)
            b = b_s[rows, :]
            e = jnp.exp(b - b[mid:mid + 1])
            x_s[rows, :] = (q_s[rows, :] * e).astype(_BF)
            y_s[rows, :] = (k_s[rows, :] / e).astype(_BF)

    def products(r, _):
        for c in chunks(r):
            rows = chunk_rows(c)
            for hd in range(B_HEADS):
                lanes = head_lanes(hd)
                yh = y_s[rows, lanes]
                a_s[hd, rows, :] = jnp.where(causal, _dot_tb(x_s[rows, lanes], yh), 0.0).astype(_BF)
                u_s[c, hd] = _dot_ta(v_s[rows, lanes], yh)

    def states(r, _):
        for hd in range(B_HEADS):
            lanes = head_lanes(hd)
            st = st_s[hd]
            for c in chunks(r):
                rho = b_s[c * C + mid:c * C + mid + 1, lanes]
                b_tot = b_s[c * C + last:c * C + last + 1, lanes]
                u = u_s[c, hd]
                u_s[c, hd] = st * jnp.exp(rho)
                st = st * jnp.exp(b_tot) + u * jnp.exp(b_tot - rho)
            st_s[hd] = st

    def outputs(r, _):
        for c in chunks(r):
            rows = chunk_rows(c)
            for hd in range(B_HEADS):
                lanes = head_lanes(hd)
                emit(rows, lanes, _dot_tb(x_s[rows, lanes], u_s[c, hd])
                     + _dot(a_s[hd, rows, :], v_s[rows, lanes]))

    return scale, products, states, outputs


def _scan_factored_ok(scan_scratch, fwd):
    b_s = scan_scratch[4]
    C = SCAN_CHUNK
    mid, last = _scan_reference_rows(fwd)
    worst = jnp.zeros((1, B_WIDTH), _F32)
    for c in range(b_s.shape[0] // C):
        b_mid = b_s[c * C + mid:c * C + mid + 1, :]
        b_tot = b_s[c * C + last:c * C + last + 1, :]
        worst = jnp.maximum(worst, jnp.maximum(-b_mid, b_mid - b_tot))
    return jnp.max(worst) < FACTORED_DECAY_LIMIT


def _begin_tile(st_s, st0_s):
    @pl.when(pl.program_id(1) == 0)
    def _():
        st_s[...] = jnp.zeros_like(st_s)
    st0_s[...] = st_s[...]


def _bwd_kernel(layer, x_ref, gpre_ref, w_ref, lbraw_ref, o_ref, q_ref, v_ref, *scratch):
    (k_s, g_s), rest, (st_s, st0_s) = scratch[:2], scratch[2:-2], scratch[-2:]
    scan_scratch = (q_ref.at[0], k_s, v_ref.at[0], g_s) + tuple(rest)
    _begin_tile(st_s, st0_s)
    lb = _lower_bound(lbraw_ref, layer) if layer > 0 else None

    def project(r, _):
        h = _rms(x_ref[0, _row_block(r), :], gpre_ref[...]).astype(_BF)
        return jnp.dot(h, w_ref[...], preferred_element_type=_F32)

    def gates(r, p):
        rows = _row_block(r)
        q_ref[0, rows, :] = _silu(p[:, 0:512]) * (B_DK ** -0.5)
        v_ref[0, rows, :] = p[:, 1024:1536].astype(_BF)
        return _hgrn_gates(p[:, 512:1024], rows, lb, scan_scratch)

    def cumulate(r, g):
        _hgrn_cumulative(g, _row_block(r), scan_scratch, fwd=False)

    def emit(rows, lanes, o):
        o_ref[0, rows, lanes] = o

    blocks = reversed(range(x_ref.shape[1] // ROW_BLOCK))
    _skewed(blocks, project, gates, cumulate, *_scan_factored_stages(scan_scratch, st_s, False, emit))

    @pl.when(jnp.logical_not(_scan_factored_ok(scan_scratch, False)))
    def _():
        st_s[...] = st0_s[...]
        _scan_exact_loop(scan_scratch, st_s, False, emit)


def _mix_kernel(layer, tiles_per_seq, xcur_ref, xprev_ref, obwd_ref, q_ref, v_ref, gpre_ref, w_ref,
                vlng_ref, vlnb_ref, wsp_ref, bsp_ref, lbraw_ref, onorm_ref, wout_ref, gpost_ref, out_ref,
                k2, g2, b2, kc_s, bc_s, x_s, y_s, a_s, u_s, gate2, st_s, st0_s, cat2):
    step = pl.program_id(0)
    cur = lax.rem(step, 2)
    prev = 1 - cur

    @pl.when(step == 0)
    def _():
        for ref in (k2, g2, b2, gate2, cat2):
            ref[...] = jnp.zeros_like(ref)

    @pl.when(lax.rem(step - 1, tiles_per_seq) == 0)
    def _():
        st_s[...] = jnp.zeros_like(st_s)

    st0_s[...] = st_s[...]
    stage_scratch = (None, k2.at[cur], None, g2.at[cur], b2.at[cur])
    scan_scratch = (q_ref.at[0], k2.at[prev], v_ref.at[0], g2.at[prev], b2.at[prev],
                    kc_s, bc_s, x_s, y_s, a_s, u_s)
    gate_cur, gate_prev, cat_cur, cat_prev = gate2.at[cur], gate2.at[prev], cat2.at[cur], cat2.at[prev]
    lb = _lower_bound(lbraw_ref, layer) if layer > 0 else None
    blocks = range(xcur_ref.shape[1] // ROW_BLOCK)

    def project(r, _):
        h = _rms(xcur_ref[0, _row_block(r), :], gpre_ref[...]).astype(_BF)
        return jnp.dot(h, w_ref[...], preferred_element_type=_F32)

    def elementwise(r, p):
        rows = _row_block(r)
        u = _gelu(p[:, 0:512])
        v = _gelu(p[:, 512:1024])
        vc = v - jnp.mean(v, axis=-1, keepdims=True)
        v = vc * lax.rsqrt(jnp.mean(vc * vc, axis=-1, keepdims=True) + EPS) * vlng_ref[...] + vlnb_ref[...]
        g = _hgrn_gates(p[:, 1024:1536], rows, lb, stage_scratch)
        gate_cur[rows, :] = _silu(p[:, 1536:2048])
        return u, v.astype(_BF), g

    def dependent_matmuls(r, uvg):
        rows = _row_block(r)
        u, v, g = uvg
        chunks = [slice(n * A_CHUNK, (n + 1) * A_CHUNK) for n in range(ROW_BLOCK // A_CHUNK)]
        for gi in range(A_GROUPS):
            lanes = slice(gi * A_GROUP_DIM, (gi + 1) * A_GROUP_DIM)
            sv = jnp.dot(wsp_ref[gi], jnp.concatenate([v[c, lanes] for c in chunks], axis=1),
                         preferred_element_type=_F32)
            for n, c in enumerate(chunks):
                cat_cur[rows.start + c.start:rows.start + c.stop, lanes] = (
                    u[c, lanes] * (sv[:, n * A_GROUP_DIM:(n + 1) * A_GROUP_DIM] + bsp_ref[gi])).astype(_BF)
        _hgrn_cumulative(g, rows, stage_scratch, fwd=True)

    def emit(rows, lanes, o):
        o_sum = o + obwd_ref[0, rows, lanes]
        cat_prev[rows, A_WIDTH + lanes.start:A_WIDTH + lanes.stop] = (
            _rms(o_sum, onorm_ref[...]) * gate_prev[rows, lanes]).astype(_BF)

    def out_proj(r, _):
        return jnp.dot(cat_prev[_row_block(r), :], wout_ref[...], preferred_element_type=_F32)

    def residual(r, y):
        rows = _row_block(r)
        out_ref[0, rows, :] = xprev_ref[0, rows, :] + _rms(y, gpost_ref[...])

    _interleaved(
        _skewed_steps(blocks, project, elementwise, dependent_matmuls),
        _skewed_steps(blocks, *_scan_factored_stages(scan_scratch, st_s, True, emit), out_proj, residual))

    @pl.when(jnp.logical_not(_scan_factored_ok(scan_scratch, True)))
    def _():
        st_s[...] = st0_s[...]
        _scan_exact_loop(scan_scratch, st_s, True, emit)
        _skewed(blocks, out_proj, residual)


def _kv_kernel(mem_ref, g_ref, w_ref, k_ref, v_ref):
    m = _rms(mem_ref[0], g_ref[...])
    kv = _dot(m, w_ref[...])
    k_ref[0] = kv[:, :D_MODEL].astype(_BF)
    v_ref[0] = kv[:, D_MODEL:].astype(_BF)


def _xattn_kernel(x_ref, k_ref, v_ref, gpre_ref, wq_ref, wo_ref, gpost_ref, out_ref, cat_s):
    head_lanes = lambda hd: slice(hd * X_HEAD_DIM, (hd + 1) * X_HEAD_DIM)

    def query(r, _):
        h = _rms(x_ref[0, _row_block(r), :], gpre_ref[...])
        return _dot(h, wq_ref[...]).astype(_BF)

    def scores(r, q):
        return [_dot_tb(q[:, head_lanes(hd)], k_ref[0, :, head_lanes(hd)]) for hd in range(X_HEADS)]

    def softmax(r, s_heads):
        p_heads = []
        for s in s_heads:
            s = s * (X_HEAD_DIM ** -0.5)
            e = jnp.exp(s - jnp.max(s, axis=-1, keepdims=True))
            p_heads.append((e / jnp.sum(e, axis=-1, keepdims=True)).astype(_BF))
        return p_heads

    def values(r, p_heads):
        for hd, p in enumerate(p_heads):
            cat_s[_row_block(r), head_lanes(hd)] = _dot(p, v_ref[0, :, head_lanes(hd)]).astype(_BF)

    def out_proj(r, _):
        return jnp.dot(cat_s[_row_block(r), :], wo_ref[...], preferred_element_type=_F32)

    def residual(r, y):
        rows = _row_block(r)
        out_ref[0, rows, :] = x_ref[0, rows, :] + _rms(y, gpost_ref[...])

    _skewed(range(x_ref.shape[1] // ROW_BLOCK), query, scores, softmax, values, out_proj, residual)


def _ffn_kernel(x_ref, gpre_ref, wgu_ref, wdown_ref, gpost_ref, out_ref):
    def gate_up(r, _):
        h = _rms(x_ref[0, _row_block(r), :], gpre_ref[...]).astype(_BF)
        return jnp.dot(h, wgu_ref[...], preferred_element_type=_F32)

    def activate(r, p):
        return (_silu(p[:, :D_FF]) * p[:, D_FF:]).astype(_BF)

    def down(r, act):
        return jnp.dot(act, wdown_ref[...], preferred_element_type=_F32)

    def residual(r, y):
        rows = _row_block(r)
        out_ref[0, rows, :] = x_ref[0, rows, :] + _rms(y, gpost_ref[...])

    _skewed(range(x_ref.shape[1] // ROW_BLOCK), gate_up, activate, down, residual)


def _params(grid_rank=2):
    return pltpu.CompilerParams(dimension_semantics=("arbitrary",) * grid_rank,
                                vmem_limit_bytes=VMEM_LIMIT_BYTES)


def _const_spec(shape):
    return pl.BlockSpec(shape, lambda *_: (0,) * len(shape), pipeline_mode=pl.Buffered(1))


def _tile_spec(width, rev_tiles=None, tile=SEQ_TILE):
    if rev_tiles is None:
        return pl.BlockSpec((1, tile, width), lambda b, t: (b, t, 0))
    return pl.BlockSpec((1, tile, width), lambda b, t: (b, rev_tiles - 1 - t, 0))


def _scan_scratch():
    tile = lambda dt: pltpu.VMEM((SEQ_TILE, B_WIDTH), dt)
    chunk = lambda: pltpu.VMEM((SCAN_CHUNK, B_WIDTH), _F32)
    return [tile(_F32), tile(_F32), tile(_F32), chunk(), chunk(), tile(_BF), tile(_BF),
            pltpu.VMEM((B_HEADS, SEQ_TILE, SCAN_CHUNK), _BF),
            pltpu.VMEM((SEQ_TILE // SCAN_CHUNK, B_HEADS, B_DK, B_DK), _F32)]


def _state_scratch():
    return pltpu.VMEM((B_HEADS, B_DK, B_DK), _F32)


def _bwd_call(x, layer, gpre, w_bwd, lbraw):
    bn, ln, _ = x.shape
    nt = ln // SEQ_TILE
    return pl.pallas_call(
        functools.partial(_bwd_kernel, layer),
        grid=(bn, nt),
        in_specs=[_tile_spec(D_MODEL, nt), _const_spec(gpre.shape), _const_spec(w_bwd.shape),
                  _const_spec(lbraw.shape)],
        out_specs=[_tile_spec(B_WIDTH, nt)] * 3,
        out_shape=[jax.ShapeDtypeStruct((bn, ln, B_WIDTH), _F32),
                   jax.ShapeDtypeStruct((bn, ln, B_WIDTH), _F32),
                   jax.ShapeDtypeStruct((bn, ln, B_WIDTH), _BF)],
        scratch_shapes=_scan_scratch() + [_state_scratch(), _state_scratch()],
        compiler_params=_params(),
        name=f"hgrn_bwd_l{layer}",
    )(x, gpre, w_bwd, lbraw)


def _mix_call(x, obwd, q, v, layer, gpre, w_mix, vlng, vlnb, wsp, bsp, lbraw, onorm, wout, gpost):
    bn, ln, _ = x.shape
    nt = ln // MIX_TILE
    n_tiles = bn * nt
    consts = (gpre, w_mix, vlng, vlnb, wsp, bsp, lbraw, onorm, wout, gpost)

    def tile_spec(width, lag):
        def index_map(i):
            tile = jnp.clip(i - lag, 0, n_tiles - 1)
            return (tile // nt, tile % nt, 0)
        return pl.BlockSpec((1, MIX_TILE, width), index_map)

    staged = lambda width, dt: pltpu.VMEM((2, MIX_TILE, width), dt)
    tile = lambda dt: pltpu.VMEM((MIX_TILE, B_WIDTH), dt)
    chunk = lambda: pltpu.VMEM((SCAN_CHUNK, B_WIDTH), _F32)
    return pl.pallas_call(
        functools.partial(_mix_kernel, layer, nt),
        grid=(n_tiles + 1,),
        in_specs=[tile_spec(D_MODEL, 0), tile_spec(D_MODEL, 1)] + [tile_spec(B_WIDTH, 1)] * 3
                 + [_const_spec(c.shape) for c in consts],
        out_specs=tile_spec(D_MODEL, 1),
        out_shape=jax.ShapeDtypeStruct(x.shape, _F32),
        scratch_shapes=[staged(B_WIDTH, _F32), staged(B_WIDTH, _F32), staged(B_WIDTH, _F32),
                        chunk(), chunk(), tile(_BF), tile(_BF),
                        pltpu.VMEM((B_HEADS, MIX_TILE, SCAN_CHUNK), _BF),
                        pltpu.VMEM((MIX_TILE // SCAN_CHUNK, B_HEADS, B_DK, B_DK), _F32),
                        staged(B_WIDTH, _F32), _state_scratch(), _state_scratch(), staged(D_MODEL, _BF)],
        compiler_params=_params(grid_rank=1),
        name=f"token_mix_l{layer}",
    )(x, x, obwd, q, v, *consts)


def _kv_call(mem, layer, g, w):
    bn = mem.shape[0]
    spec = pl.BlockSpec((1, MEM_TOKENS, D_MODEL), lambda b, t: (b, 0, 0))
    return pl.pallas_call(
        _kv_kernel,
        grid=(bn, 1),
        in_specs=[spec, _const_spec(g.shape), _const_spec(w.shape)],
        out_specs=[spec, spec],
        out_shape=[jax.ShapeDtypeStruct(mem.shape, _BF)] * 2,
        compiler_params=_params(),
        name=f"mem_kv_l{layer}",
    )(mem, g, w)


def _xattn_call(x, k, v, layer, gpre, wq, wo, gpost):
    bn, ln, _ = x.shape
    nt = ln // XATTN_TILE
    mem_spec = pl.BlockSpec((1, MEM_TOKENS, D_MODEL), lambda b, t: (b, 0, 0))
    consts = (gpre, wq, wo, gpost)
    x_spec = _tile_spec(D_MODEL, tile=XATTN_TILE)
    return pl.pallas_call(
        _xattn_kernel,
        grid=(bn, nt),
        in_specs=[x_spec, mem_spec, mem_spec] + [_const_spec(c.shape) for c in consts],
        out_specs=x_spec,
        out_shape=jax.ShapeDtypeStruct(x.shape, _F32),
        scratch_shapes=[pltpu.VMEM((XATTN_TILE, D_MODEL), _BF)],
        compiler_params=_params(),
        name=f"mem_xattn_l{layer}",
    )(x, k, v, *consts)


def _ffn_call(x, layer, gpre, wgu, wdown, gpost):
    bn, ln, _ = x.shape
    nt = ln // SEQ_TILE
    consts = (gpre, wgu, wdown, gpost)
    return pl.pallas_call(
        _ffn_kernel,
        grid=(bn, nt),
        in_specs=[_tile_spec(D_MODEL)] + [_const_spec(c.shape) for c in consts],
        out_specs=_tile_spec(D_MODEL),
        out_shape=jax.ShapeDtypeStruct(x.shape, _F32),
        compiler_params=_params(),
        name=f"swiglu_ffn_l{layer}",
    )(x, *consts)


def _layer_weights(l, p):
    w_in = p['w_in'][l].astype(_BF)
    cols = lambda a, b: w_in[:, a:b]
    row = lambda a: a[l][None, :]
    return dict(
        w_mix=jnp.concatenate([cols(0, 1024), cols(1536, 2048), cols(3072, 3584)], axis=1),
        w_bwd=jnp.concatenate([cols(1024, 1536), cols(2048, 3072)], axis=1),
        pre_mix_g=row(p['pre_mix_g']), v_ln_g=row(p['v_ln_g']), v_ln_b=row(p['v_ln_b']),
        w_spatial=p['w_spatial'][l].astype(_BF),
        b_spatial=jnp.broadcast_to(p['b_spatial'][l][:, :, None], (A_GROUPS, A_CHUNK, A_GROUP_DIM)),
        lb_fwd=p['lb_raw'][0], lb_bwd=p['lb_raw'][1],
        onorm_g=row(p['onorm_g']), w_out=p['w_out'][l].astype(_BF), post_mix_g=row(p['post_mix_g']),
        pre_x_g=row(p['pre_x_g']), mem_norm_g=row(p['mem_norm_g']),
        w_xq=p['w_xq'][l].astype(_BF), w_xkv=p['w_xkv'][l].astype(_BF), w_xo=p['w_xo'][l].astype(_BF),
        post_x_g=row(p['post_x_g']), pre_ffn_g=row(p['pre_ffn_g']),
        w_gu=p['w_gu'][l].astype(_BF), w_down=p['w_down'][l].astype(_BF), post_ffn_g=row(p['post_ffn_g']),
    )


def _trunk(x, mem, layers):
    assert x.shape[1] % SEQ_TILE == 0 and x.shape[1] % XATTN_TILE == 0 and x.shape[2] == D_MODEL
    assert x.shape[1] % MIX_TILE == 0 and MIX_TILE % ROW_BLOCK == 0
    assert SEQ_TILE % ROW_BLOCK == 0 and ROW_BLOCK % SCAN_CHUNK == 0 and ROW_BLOCK % A_CHUNK == 0
    for l, w in enumerate(layers):
        obwd, q, v = _bwd_call(x, l, w['pre_mix_g'], w['w_bwd'], w['lb_bwd'])
        x = _mix_call(x, obwd, q, v, l, w['pre_mix_g'], w['w_mix'], w['v_ln_g'], w['v_ln_b'], w['w_spatial'],
                      w['b_spatial'], w['lb_fwd'], w['onorm_g'], w['w_out'], w['post_mix_g'])
        k, v = _kv_call(mem, l, w['mem_norm_g'], w['w_xkv'])
        x = _xattn_call(x, k, v, l, w['pre_x_g'], w['w_xq'], w['w_xo'], w['post_x_g'])
        x = _ffn_call(x, l, w['pre_ffn_g'], w['w_gu'], w['w_down'], w['post_ffn_g'])
    return x


def kernel(x_prompt, x_sample, mem_prompt, mem_sample, pre_mix_g, w_in, v_ln_g, v_ln_b, w_spatial, b_spatial, lb_raw, onorm_g, w_out, post_mix_g, pre_x_g, mem_norm_g, w_xq, w_xkv, w_xo, post_x_g, pre_ffn_g, w_gu, w_down, post_ffn_g):
    p = dict(pre_mix_g=pre_mix_g, w_in=w_in, v_ln_g=v_ln_g, v_ln_b=v_ln_b,
             w_spatial=w_spatial, b_spatial=b_spatial, lb_raw=lb_raw, onorm_g=onorm_g,
             w_out=w_out, post_mix_g=post_mix_g, pre_x_g=pre_x_g, mem_norm_g=mem_norm_g,
             w_xq=w_xq, w_xkv=w_xkv, w_xo=w_xo, post_x_g=post_x_g, pre_ffn_g=pre_ffn_g,
             w_gu=w_gu, w_down=w_down, post_ffn_g=post_ffn_g)
    layers = [_layer_weights(l, p) for l in range(w_in.shape[0])]
    return (_trunk(x_prompt, mem_prompt, layers), _trunk(x_sample, mem_sample, layers))
```

```python
import functools

import jax
import jax.numpy as jnp
from jax import lax
from jax.experimental import pallas as pl
from jax.experimental.pallas import tpu as pltpu

D_MODEL = 1024
A_WIDTH = 512
A_GROUPS = 4
A_GROUP_DIM = 128
A_CHUNK = 128
B_WIDTH = 512
B_HEADS = 4
B_DK = 128
MEM_TOKENS = 256
X_HEADS = 4
X_HEAD_DIM = 256
D_FF = 2816
EPS = 1e-6

SEQ_TILE = 1024
XATTN_TILE = 2048
ROW_BLOCK = 512
SCAN_CHUNK = 64
SUB_BLOCK = 16
FACTORED_DECAY_LIMIT = 55.0
VMEM_LIMIT_BYTES = 60 * 1024 * 1024

_BF = jnp.bfloat16
_F32 = jnp.float32


def _dot(a, b):
    return jnp.dot(a.astype(_BF), b.astype(_BF), preferred_element_type=_F32)


def _dot_tb(a, b):
    return lax.dot_general(a.astype(_BF), b.astype(_BF), (((1,), (1,)), ((), ())),
                           preferred_element_type=_F32)


def _dot_ta(a, b):
    return lax.dot_general(a.astype(_BF), b.astype(_BF), (((0,), (0,)), ((), ())),
                           preferred_element_type=_F32)


def _rms(x, g):
    return x * lax.rsqrt(jnp.mean(x * x, axis=-1, keepdims=True) + EPS) * g


def _gelu(x):
    return 0.5 * x * (1.0 + lax.erf(x * (2.0 ** -0.5)))


def _silu(x):
    return x * (0.5 * jnp.tanh(0.5 * x) + 0.5)


def _lower_bound(lbraw_ref, layer):
    depth = lbraw_ref.shape[0]
    rows = [lbraw_ref[j:j + 1, :] for j in range(depth)]
    m = functools.reduce(jnp.maximum, rows)
    e = [jnp.exp(r - m) for r in rows]
    denom = functools.reduce(lambda a, b: a + b, e)
    return functools.reduce(lambda a, b: a + b, e[1:layer + 1]) / denom


def _skewed(order, *stages):
    order = list(order)
    vals = {}
    for i in range(len(order) + len(stages) - 1):
        for k, stage in enumerate(stages):
            if 0 <= i - k < len(order):
                r = order[i - k]
                vals[(k, r)] = stage(r, vals.pop((k - 1, r), None))


def _row_block(r):
    return slice(r * ROW_BLOCK, (r + 1) * ROW_BLOCK)


def _split3(x):
    hi = x.astype(_BF)
    r1 = x - hi.astype(_F32)
    mid = r1.astype(_BF)
    lo = (r1 - mid.astype(_F32)).astype(_BF)
    return hi, mid, lo


def _cumulative(g, fwd):
    n = g.shape[0]
    row = lax.broadcasted_iota(jnp.int32, (n, n), 0)
    col = lax.broadcasted_iota(jnp.int32, (n, n), 1)
    tri = jnp.where(col <= row if fwd else col >= row, 1.0, 0.0).astype(_BF)
    g_hi, g_mid, g_lo = _split3(g)
    return (jnp.dot(tri, g_hi, preferred_element_type=_F32)
            + jnp.dot(tri, g_mid, preferred_element_type=_F32)
            + jnp.dot(tri, g_lo, preferred_element_type=_F32))


def _hgrn_gates(f, rows, lb, scan_scratch):
    k_s, g_s = scan_scratch[1], scan_scratch[3]
    if lb is None:
        e = jnp.exp(-jnp.abs(f))
        g = jnp.minimum(f, 0.0) - jnp.log(1.0 + e)
        k_s[rows, :] = jnp.where(f >= 0.0, e, 1.0) / (1.0 + e)
    else:
        forget = lb + (1.0 - lb) * (0.5 * jnp.tanh(0.5 * f) + 0.5)
        g = jnp.log(forget)
        k_s[rows, :] = 1.0 - forget
    g_s[rows, :] = g
    return g


def _hgrn_cumulative(g, rows, scan_scratch, fwd):
    b_s = scan_scratch[4]
    for c in range(ROW_BLOCK // SCAN_CHUNK):
        lo = c * SCAN_CHUNK
        b_s[rows.start + lo:rows.start + lo + SCAN_CHUNK, :] = _cumulative(g[lo:lo + SCAN_CHUNK], fwd)


def _scan_exact_loop(scan_scratch, st_s, fwd, emit):
    q_s, k_s, v_s, g_s, b_s, kc_s, bc_s = scan_scratch[:7]
    C = SCAN_CHUNK
    n_chunks = q_s.shape[0] // C
    n_sub = C // SUB_BLOCK
    sub_row = lax.broadcasted_iota(jnp.int32, (SUB_BLOCK, C), 0)
    sub_col = lax.broadcasted_iota(jnp.int32, (SUB_BLOCK, C), 1)
    last = C - 1 if fwd else 0

    def chunk_body(c, carry):
        idx = c if fwd else n_chunks - 1 - c
        base = pl.multiple_of(idx * C, C)
        rows = pl.ds(base, C)
        bc_s[...] = b_s[rows, :]
        kc_s[...] = k_s[rows, :]
        for hd in range(B_HEADS):
            lanes = slice(hd * B_DK, (hd + 1) * B_DK)
            q = q_s[rows, lanes]
            k = kc_s[:, lanes]
            v = v_s[rows, lanes]
            b = bc_s[:, lanes]
            st = st_s[hd]
            b_tot = b[last:last + 1]
            o = _dot_tb(q * jnp.exp(b), st)
            kd = k * jnp.exp(b_tot - b)
            st_s[hd] = st * jnp.exp(b_tot) + _dot_ta(v, kd)

            a_rows = []
            for i in range(n_sub):
                lo, hi = i * SUB_BLOCK, (i + 1) * SUB_BLOCK
                qi, bi = q[lo:hi], b[lo:hi]
                if fwd and i > 0:
                    r = b[lo - 1:lo]
                    ke = k[:lo] * jnp.exp(r - b[:lo])
                    ke = jnp.concatenate([ke, jnp.zeros((C - lo, B_DK), _F32)], axis=0)
                    a_i = _dot_tb(qi * jnp.exp(bi - r), ke)
                elif (not fwd) and i < n_sub - 1:
                    r = b[hi:hi + 1]
                    ke = k[hi:] * jnp.exp(r - b[hi:])
                    ke = jnp.concatenate([jnp.zeros((hi, B_DK), _F32), ke], axis=0)
                    a_i = _dot_tb(qi * jnp.exp(bi - r), ke)
                else:
                    a_i = jnp.zeros((SUB_BLOCK, C), _F32)
                for s in range(SUB_BLOCK):
                    j = lo + s
                    k_row = kc_s[j:j + 1, lanes]
                    b_row = bc_s[j:j + 1, lanes]
                    p = qi * k_row * jnp.exp(jnp.minimum(bi - b_row, 0.0))
                    a_i = jnp.where(sub_col == j, jnp.sum(p, axis=-1, keepdims=True), a_i)
                local = sub_col - lo
                if fwd:
                    keep = (sub_col < lo) | ((sub_col < hi) & (sub_row >= local))
                else:
                    keep = (sub_col >= hi) | ((sub_col >= lo) & (sub_row <= local))
                a_rows.append(jnp.where(keep, a_i, 0.0))
            a = jnp.concatenate(a_rows, axis=0)
            emit(rows, lanes, o + _dot(a, v))
        return carry

    lax.fori_loop(0, n_chunks, chunk_body, 0)


def _scan_reference_rows(fwd):
    return (SCAN_CHUNK // 2 - 1, SCAN_CHUNK - 1) if fwd else (SCAN_CHUNK // 2, 0)


def _scan_factored_stages(scan_scratch, st_s, fwd, emit):
    q_s, k_s, v_s, g_s, b_s, kc_s, bc_s, x_s, y_s, a_s, u_s = scan_scratch
    C = SCAN_CHUNK
    mid, last = _scan_reference_rows(fwd)
    row = lax.broadcasted_iota(jnp.int32, (C, C), 0)
    col = lax.broadcasted_iota(jnp.int32, (C, C), 1)
    causal = (col <= row) if fwd else (col >= row)
    chunk_rows = lambda c: slice(c * C, (c + 1) * C)
    head_lanes = lambda hd: slice(hd * B_DK, (hd + 1) * B_DK)

    def chunks(r):
        per_block = ROW_BLOCK // C
        ids = list(range(r * per_block, (r + 1) * per_block))
        return ids if fwd else ids[::-1]

    def scale(r, _):
        for c in chunks(r):
            rows = chunk_rows(c)
            b = b_s[rows, :]
            e = jnp.exp(b - b[mid:mid + 1])
            x_s[rows, :] = (q_s[rows, :] * e).astype(_BF)
            y_s[rows, :] = (k_s[rows, :] / e).astype(_BF)

    def products(r, _):
        for c in chunks(r):
            rows = chunk_rows(c)
            for hd in range(B_HEADS):
                lanes = head_lanes(hd)
                yh = y_s[rows, lanes]
                a_s[hd, rows, :] = jnp.where(causal, _dot_tb(x_s[rows, lanes], yh), 0.0).astype(_BF)
                u_s[c, hd] = _dot_ta(v_s[rows, lanes], yh)

    def states(r, _):
        for hd in range(B_HEADS):
            lanes = head_lanes(hd)
            st = st_s[hd]
            for c in chunks(r):
                rho = b_s[c * C + mid:c * C + mid + 1, lanes]
                b_tot = b_s[c * C + last:c * C + last + 1, lanes]
                u = u_s[c, hd]
                u_s[c, hd] = st * jnp.exp(rho)
                st = st * jnp.exp(b_tot) + u * jnp.exp(b_tot - rho)
            st_s[hd] = st

    def outputs(r, _):
        for c in chunks(r):
            rows = chunk_rows(c)
            for hd in range(B_HEADS):
                lanes = head_lanes(hd)
                emit(rows, lanes, _dot_tb(x_s[rows, lanes], u_s[c, hd])
                     + _dot(a_s[hd, rows, :], v_s[rows, lanes]))

    return scale, products, states, outputs


def _scan_factored_ok(scan_scratch, fwd):
    b_s = scan_scratch[4]
    C = SCAN_CHUNK
    mid, last = _scan_reference_rows(fwd)
    worst = jnp.zeros((1, B_WIDTH), _F32)
    for c in range(b_s.shape[0] // C):
        b_mid = b_s[c * C + mid:c * C + mid + 1, :]
        b_tot = b_s[c * C + last:c * C + last + 1, :]
        worst = jnp.maximum(worst, jnp.maximum(-b_mid, b_mid - b_tot))
    return jnp.max(worst) < FACTORED_DECAY_LIMIT


def _begin_tile(st_s, st0_s):
    @pl.when(pl.program_id(1) == 0)
    def _():
        st_s[...] = jnp.zeros_like(st_s)
    st0_s[...] = st_s[...]


def _bwd_kernel(layer, x_ref, gpre_ref, w_ref, lbraw_ref, o_ref, q_ref, v_ref, *scratch):
    (k_s, g_s), rest, (st_s, st0_s) = scratch[:2], scratch[2:-2], scratch[-2:]
    scan_scratch = (q_ref.at[0], k_s, v_ref.at[0], g_s) + tuple(rest)
    _begin_tile(st_s, st0_s)
    lb = _lower_bound(lbraw_ref, layer) if layer > 0 else None

    def project(r, _):
        h = _rms(x_ref[0, _row_block(r), :], gpre_ref[...]).astype(_BF)
        return jnp.dot(h, w_ref[...], preferred_element_type=_F32)

    def gates(r, p):
        rows = _row_block(r)
        q_ref[0, rows, :] = _silu(p[:, 0:512]) * (B_DK ** -0.5)
        v_ref[0, rows, :] = p[:, 1024:1536].astype(_BF)
        return _hgrn_gates(p[:, 512:1024], rows, lb, scan_scratch)

    def cumulate(r, g):
        _hgrn_cumulative(g, _row_block(r), scan_scratch, fwd=False)

    def emit(rows, lanes, o):
        o_ref[0, rows, lanes] = o

    blocks = reversed(range(x_ref.shape[1] // ROW_BLOCK))
    _skewed(blocks, project, gates, cumulate, *_scan_factored_stages(scan_scratch, st_s, False, emit))

    @pl.when(jnp.logical_not(_scan_factored_ok(scan_scratch, False)))
    def _():
        st_s[...] = st0_s[...]
        _scan_exact_loop(scan_scratch, st_s, False, emit)


def _mix_kernel(layer, x_ref, obwd_ref, q_ref, v_ref, gpre_ref, w_ref, vlng_ref, vlnb_ref, wsp_ref,
                bsp_ref, lbraw_ref, onorm_ref, wout_ref, gpost_ref, out_ref, *scratch):
    (k_s, g_s), rest, (gate_s, st_s, st0_s, cat_s) = scratch[:2], scratch[2:-4], scratch[-4:]
    scan_scratch = (q_ref.at[0], k_s, v_ref.at[0], g_s) + tuple(rest)
    _begin_tile(st_s, st0_s)
    lb = _lower_bound(lbraw_ref, layer) if layer > 0 else None
    blocks = range(x_ref.shape[1] // ROW_BLOCK)

    def project(r, _):
        h = _rms(x_ref[0, _row_block(r), :], gpre_ref[...]).astype(_BF)
        return jnp.dot(h, w_ref[...], preferred_element_type=_F32)

    def elementwise(r, p):
        rows = _row_block(r)
        u = _gelu(p[:, 0:512])
        v = _gelu(p[:, 512:1024])
        vc = v - jnp.mean(v, axis=-1, keepdims=True)
        v = vc * lax.rsqrt(jnp.mean(vc * vc, axis=-1, keepdims=True) + EPS) * vlng_ref[...] + vlnb_ref[...]
        g = _hgrn_gates(p[:, 1024:1536], rows, lb, scan_scratch)
        gate_s[rows, :] = _silu(p[:, 1536:2048])
        return u, v.astype(_BF), g

    def dependent_matmuls(r, uvg):
        rows = _row_block(r)
        u, v, g = uvg
        chunks = [slice(n * A_CHUNK, (n + 1) * A_CHUNK) for n in range(ROW_BLOCK // A_CHUNK)]
        for gi in range(A_GROUPS):
            lanes = slice(gi * A_GROUP_DIM, (gi + 1) * A_GROUP_DIM)
            sv = jnp.dot(wsp_ref[gi], jnp.concatenate([v[c, lanes] for c in chunks], axis=1),
                         preferred_element_type=_F32)
            for n, c in enumerate(chunks):
                cat_s[rows.start + c.start:rows.start + c.stop, lanes] = (
                    u[c, lanes] * (sv[:, n * A_GROUP_DIM:(n + 1) * A_GROUP_DIM] + bsp_ref[gi])).astype(_BF)
        _hgrn_cumulative(g, rows, scan_scratch, fwd=True)

    def emit(rows, lanes, o):
        o_sum = o + obwd_ref[0, rows, lanes]
        cat_s[rows, A_WIDTH + lanes.start:A_WIDTH + lanes.stop] = (
            _rms(o_sum, onorm_ref[...]) * gate_s[rows, lanes]).astype(_BF)

    def out_proj(r, _):
        return jnp.dot(cat_s[_row_block(r), :], wout_ref[...], preferred_element_type=_F32)

    def residual(r, y):
        rows = _row_block(r)
        out_ref[0, rows, :] = x_ref[0, rows, :] + _rms(y, gpost_ref[...])

    _skewed(blocks, project, elementwise, dependent_matmuls,
            *_scan_factored_stages(scan_scratch, st_s, True, emit), out_proj, residual)

    @pl.when(jnp.logical_not(_scan_factored_ok(scan_scratch, True)))
    def _():
        st_s[...] = st0_s[...]
        _scan_exact_loop(scan_scratch, st_s, True, emit)
        _skewed(blocks, out_proj, residual)


def _kv_kernel(mem_ref, g_ref, w_ref, k_ref, v_ref):
    m = _rms(mem_ref[0], g_ref[...])
    kv = _dot(m, w_ref[...])
    k_ref[0] = kv[:, :D_MODEL].astype(_BF)
    v_ref[0] = kv[:, D_MODEL:].astype(_BF)


def _xattn_kernel(x_ref, k_ref, v_ref, gpre_ref, wq_ref, wo_ref, gpost_ref, out_ref, cat_s):
    head_lanes = lambda hd: slice(hd * X_HEAD_DIM, (hd + 1) * X_HEAD_DIM)

    def query(r, _):
        h = _rms(x_ref[0, _row_block(r), :], gpre_ref[...])
        return _dot(h, wq_ref[...]).astype(_BF)

    def scores(r, q):
        return [_dot_tb(q[:, head_lanes(hd)], k_ref[0, :, head_lanes(hd)]) for hd in range(X_HEADS)]

    def softmax(r, s_heads):
        p_heads = []
        for s in s_heads:
            s = s * (X_HEAD_DIM ** -0.5)
            e = jnp.exp(s - jnp.max(s, axis=-1, keepdims=True))
            p_heads.append((e / jnp.sum(e, axis=-1, keepdims=True)).astype(_BF))
        return p_heads

    def values(r, p_heads):
        for hd, p in enumerate(p_heads):
            cat_s[_row_block(r), head_lanes(hd)] = _dot(p, v_ref[0, :, head_lanes(hd)]).astype(_BF)

    def out_proj(r, _):
        return jnp.dot(cat_s[_row_block(r), :], wo_ref[...], preferred_element_type=_F32)

    def residual(r, y):
        rows = _row_block(r)
        out_ref[0, rows, :] = x_ref[0, rows, :] + _rms(y, gpost_ref[...])

    _skewed(range(x_ref.shape[1] // ROW_BLOCK), query, scores, softmax, values, out_proj, residual)


def _ffn_kernel(x_ref, gpre_ref, wgu_ref, wdown_ref, gpost_ref, out_ref):
    def gate_up(r, _):
        h = _rms(x_ref[0, _row_block(r), :], gpre_ref[...]).astype(_BF)
        return jnp.dot(h, wgu_ref[...], preferred_element_type=_F32)

    def activate(r, p):
        return (_silu(p[:, :D_FF]) * p[:, D_FF:]).astype(_BF)

    def down(r, act):
        return jnp.dot(act, wdown_ref[...], preferred_element_type=_F32)

    def residual(r, y):
        rows = _row_block(r)
        out_ref[0, rows, :] = x_ref[0, rows, :] + _rms(y, gpost_ref[...])

    _skewed(range(x_ref.shape[1] // ROW_BLOCK), gate_up, activate, down, residual)


def _params():
    return pltpu.CompilerParams(dimension_semantics=("arbitrary", "arbitrary"),
                                vmem_limit_bytes=VMEM_LIMIT_BYTES)


def _const_spec(shape):
    return pl.BlockSpec(shape, lambda b, t: (0,) * len(shape), pipeline_mode=pl.Buffered(1))


def _tile_spec(width, rev_tiles=None, tile=SEQ_TILE):
    if rev_tiles is None:
        return pl.BlockSpec((1, tile, width), lambda b, t: (b, t, 0))
    return pl.BlockSpec((1, tile, width), lambda b, t: (b, rev_tiles - 1 - t, 0))


def _scan_scratch():
    tile = lambda dt: pltpu.VMEM((SEQ_TILE, B_WIDTH), dt)
    chunk = lambda: pltpu.VMEM((SCAN_CHUNK, B_WIDTH), _F32)
    return [tile(_F32), tile(_F32), tile(_F32), chunk(), chunk(), tile(_BF), tile(_BF),
            pltpu.VMEM((B_HEADS, SEQ_TILE, SCAN_CHUNK), _BF),
            pltpu.VMEM((SEQ_TILE // SCAN_CHUNK, B_HEADS, B_DK, B_DK), _F32)]


def _state_scratch():
    return pltpu.VMEM((B_HEADS, B_DK, B_DK), _F32)


def _bwd_call(x, layer, gpre, w_bwd, lbraw):
    bn, ln, _ = x.shape
    nt = ln // SEQ_TILE
    return pl.pallas_call(
        functools.partial(_bwd_kernel, layer),
        grid=(bn, nt),
        in_specs=[_tile_spec(D_MODEL, nt), _const_spec(gpre.shape), _const_spec(w_bwd.shape),
                  _const_spec(lbraw.shape)],
        out_specs=[_tile_spec(B_WIDTH, nt)] * 3,
        out_shape=[jax.ShapeDtypeStruct((bn, ln, B_WIDTH), _F32),
                   jax.ShapeDtypeStruct((bn, ln, B_WIDTH), _F32),
                   jax.ShapeDtypeStruct((bn, ln, B_WIDTH), _BF)],
        scratch_shapes=_scan_scratch() + [_state_scratch(), _state_scratch()],
        compiler_params=_params(),
        name=f"hgrn_bwd_l{layer}",
    )(x, gpre, w_bwd, lbraw)


def _mix_call(x, obwd, q, v, layer, gpre, w_mix, vlng, vlnb, wsp, bsp, lbraw, onorm, wout, gpost):
    bn, ln, _ = x.shape
    nt = ln // SEQ_TILE
    consts = (gpre, w_mix, vlng, vlnb, wsp, bsp, lbraw, onorm, wout, gpost)
    return pl.pallas_call(
        functools.partial(_mix_kernel, layer),
        grid=(bn, nt),
        in_specs=[_tile_spec(D_MODEL)] + [_tile_spec(B_WIDTH)] * 3 + [_const_spec(c.shape) for c in consts],
        out_specs=_tile_spec(D_MODEL),
        out_shape=jax.ShapeDtypeStruct(x.shape, _F32),
        scratch_shapes=_scan_scratch() + [pltpu.VMEM((SEQ_TILE, B_WIDTH), _F32), _state_scratch(),
                                          _state_scratch(),
                                          pltpu.VMEM((SEQ_TILE, D_MODEL), _BF)],
        compiler_params=_params(),
        name=f"token_mix_l{layer}",
    )(x, obwd, q, v, *consts)


def _kv_call(mem, layer, g, w):
    bn = mem.shape[0]
    spec = pl.BlockSpec((1, MEM_TOKENS, D_MODEL), lambda b, t: (b, 0, 0))
    return pl.pallas_call(
        _kv_kernel,
        grid=(bn, 1),
        in_specs=[spec, _const_spec(g.shape), _const_spec(w.shape)],
        out_specs=[spec, spec],
        out_shape=[jax.ShapeDtypeStruct(mem.shape, _BF)] * 2,
        compiler_params=_params(),
        name=f"mem_kv_l{layer}",
    )(mem, g, w)


def _xattn_call(x, k, v, layer, gpre, wq, wo, gpost):
    bn, ln, _ = x.shape
    nt = ln // XATTN_TILE
    mem_spec = pl.BlockSpec((1, MEM_TOKENS, D_MODEL), lambda b, t: (b, 0, 0))
    consts = (gpre, wq, wo, gpost)
    x_spec = _tile_spec(D_MODEL, tile=XATTN_TILE)
    return pl.pallas_call(
        _xattn_kernel,
        grid=(bn, nt),
        in_specs=[x_spec, mem_spec, mem_spec] + [_const_spec(c.shape) for c in consts],
        out_specs=x_spec,
        out_shape=jax.ShapeDtypeStruct(x.shape, _F32),
        scratch_shapes=[pltpu.VMEM((XATTN_TILE, D_MODEL), _BF)],
        compiler_params=_params(),
        name=f"mem_xattn_l{layer}",
    )(x, k, v, *consts)


def _ffn_call(x, layer, gpre, wgu, wdown, gpost):
    bn, ln, _ = x.shape
    nt = ln // SEQ_TILE
    consts = (gpre, wgu, wdown, gpost)
    return pl.pallas_call(
        _ffn_kernel,
        grid=(bn, nt),
        in_specs=[_tile_spec(D_MODEL)] + [_const_spec(c.shape) for c in consts],
        out_specs=_tile_spec(D_MODEL),
        out_shape=jax.ShapeDtypeStruct(x.shape, _F32),
        compiler_params=_params(),
        name=f"swiglu_ffn_l{layer}",
    )(x, *consts)


def _layer_weights(l, p):
    w_in = p['w_in'][l].astype(_BF)
    cols = lambda a, b: w_in[:, a:b]
    row = lambda a: a[l][None, :]
    return dict(
        w_mix=jnp.concatenate([cols(0, 1024), cols(1536, 2048), cols(3072, 3584)], axis=1),
        w_bwd=jnp.concatenate([cols(1024, 1536), cols(2048, 3072)], axis=1),
        pre_mix_g=row(p['pre_mix_g']), v_ln_g=row(p['v_ln_g']), v_ln_b=row(p['v_ln_b']),
        w_spatial=p['w_spatial'][l].astype(_BF),
        b_spatial=jnp.broadcast_to(p['b_spatial'][l][:, :, None], (A_GROUPS, A_CHUNK, A_GROUP_DIM)),
        lb_fwd=p['lb_raw'][0], lb_bwd=p['lb_raw'][1],
        onorm_g=row(p['onorm_g']), w_out=p['w_out'][l].astype(_BF), post_mix_g=row(p['post_mix_g']),
        pre_x_g=row(p['pre_x_g']), mem_norm_g=row(p['mem_norm_g']),
        w_xq=p['w_xq'][l].astype(_BF), w_xkv=p['w_xkv'][l].astype(_BF), w_xo=p['w_xo'][l].astype(_BF),
        post_x_g=row(p['post_x_g']), pre_ffn_g=row(p['pre_ffn_g']),
        w_gu=p['w_gu'][l].astype(_BF), w_down=p['w_down'][l].astype(_BF), post_ffn_g=row(p['post_ffn_g']),
    )


def _trunk(x, mem, layers):
    assert x.shape[1] % SEQ_TILE == 0 and x.shape[1] % XATTN_TILE == 0 and x.shape[2] == D_MODEL
    assert SEQ_TILE % ROW_BLOCK == 0 and ROW_BLOCK % SCAN_CHUNK == 0 and ROW_BLOCK % A_CHUNK == 0
    for l, w in enumerate(layers):
        obwd, q, v = _bwd_call(x, l, w['pre_mix_g'], w['w_bwd'], w['lb_bwd'])
        x = _mix_call(x, obwd, q, v, l, w['pre_mix_g'], w['w_mix'], w['v_ln_g'], w['v_ln_b'], w['w_spatial'],
                      w['b_spatial'], w['lb_fwd'], w['onorm_g'], w['w_out'], w['post_mix_g'])
        k, v = _kv_call(mem, l, w['mem_norm_g'], w['w_xkv'])
        x = _xattn_call(x, k, v, l, w['pre_x_g'], w['w_xq'], w['w_xo'], w['post_x_g'])
        x = _ffn_call(x, l, w['pre_ffn_g'], w['w_gu'], w['w_down'], w['post_ffn_g'])
    return x


def kernel(x_prompt, x_sample, mem_prompt, mem_sample, pre_mix_g, w_in, v_ln_g, v_ln_b, w_spatial, b_spatial, lb_raw, onorm_g, w_out, post_mix_g, pre_x_g, mem_norm_g, w_xq, w_xkv, w_xo, post_x_g, pre_ffn_g, w_gu, w_down, post_ffn_g):
    p = dict(pre_mix_g=pre_mix_g, w_in=w_in, v_ln_g=v_ln_g, v_ln_b=v_ln_b,
             w_spatial=w_spatial, b_spatial=b_spatial, lb_raw=lb_raw, onorm_g=onorm_g,
             w_out=w_out, post_mix_g=post_mix_g, pre_x_g=pre_x_g, mem_norm_g=mem_norm_g,
             w_xq=w_xq, w_xkv=w_xkv, w_xo=w_xo, post_x_g=post_x_g, pre_ffn_g=pre_ffn_g,
             w_gu=w_gu, w_down=w_down, post_ffn_g=post_ffn_g)
    layers = [_layer_weights(l, p) for l in range(w_in.shape[0])]
    return (_trunk(x_prompt, mem_prompt, layers), _trunk(x_sample, mem_sample, layers))
```

```python
import functools

import jax
import jax.numpy as jnp
from jax import lax
from jax.experimental import pallas as pl
from jax.experimental.pallas import tpu as pltpu

D_MODEL = 1024
A_WIDTH = 512
A_GROUPS = 4
A_GROUP_DIM = 128
A_CHUNK = 128
B_WIDTH = 512
B_HEADS = 4
B_DK = 128
MEM_TOKENS = 256
X_HEADS = 4
X_HEAD_DIM = 256
D_FF = 2816
EPS = 1e-6

SEQ_TILE = 1024
XATTN_TILE = 2048
ROW_BLOCK = 512
FFN_ROW_BLOCK = 256
SCAN_CHUNK = 64
SUB_BLOCK = 16
FACTORED_DECAY_LIMIT = 55.0
VMEM_LIMIT_BYTES = 60 * 1024 * 1024

_BF = jnp.bfloat16
_F32 = jnp.float32


def _dot(a, b):
    return jnp.dot(a.astype(_BF), b.astype(_BF), preferred_element_type=_F32)


def _dot_tb(a, b):
    return lax.dot_general(a.astype(_BF), b.astype(_BF), (((1,), (1,)), ((), ())),
                           preferred_element_type=_F32)


def _dot_ta(a, b):
    return lax.dot_general(a.astype(_BF), b.astype(_BF), (((0,), (0,)), ((), ())),
                           preferred_element_type=_F32)


def _rms(x, g):
    return x * lax.rsqrt(jnp.mean(x * x, axis=-1, keepdims=True) + EPS) * g


def _gelu(x):
    return 0.5 * x * (1.0 + lax.erf(x * (2.0 ** -0.5)))


def _silu(x):
    return x * (0.5 * jnp.tanh(0.5 * x) + 0.5)


def _lower_bound(lbraw_ref, layer):
    depth = lbraw_ref.shape[0]
    rows = [lbraw_ref[j:j + 1, :] for j in range(depth)]
    m = functools.reduce(jnp.maximum, rows)
    e = [jnp.exp(r - m) for r in rows]
    denom = functools.reduce(lambda a, b: a + b, e)
    return functools.reduce(lambda a, b: a + b, e[1:layer + 1]) / denom


def _skewed(order, *stages):
    order = list(order)
    vals = {}
    for i in range(len(order) + len(stages) - 1):
        for k, stage in enumerate(stages):
            if 0 <= i - k < len(order):
                r = order[i - k]
                vals[(k, r)] = stage(r, vals.pop((k - 1, r), None))


def _row_block(r, size=ROW_BLOCK):
    return slice(r * size, (r + 1) * size)


def _split3(x):
    hi = x.astype(_BF)
    r1 = x - hi.astype(_F32)
    mid = r1.astype(_BF)
    lo = (r1 - mid.astype(_F32)).astype(_BF)
    return hi, mid, lo


def _cumulative(g, fwd):
    n = g.shape[0]
    row = lax.broadcasted_iota(jnp.int32, (n, n), 0)
    col = lax.broadcasted_iota(jnp.int32, (n, n), 1)
    tri = jnp.where(col <= row if fwd else col >= row, 1.0, 0.0).astype(_BF)
    g_hi, g_mid, g_lo = _split3(g)
    return (jnp.dot(tri, g_hi, preferred_element_type=_F32)
            + jnp.dot(tri, g_mid, preferred_element_type=_F32)
            + jnp.dot(tri, g_lo, preferred_element_type=_F32))


def _hgrn_gates(f, rows, lb, scan_scratch):
    k_s, g_s = scan_scratch[1], scan_scratch[3]
    if lb is None:
        e = jnp.exp(-jnp.abs(f))
        g = jnp.minimum(f, 0.0) - jnp.log(1.0 + e)
        k_s[rows, :] = jnp.where(f >= 0.0, e, 1.0) / (1.0 + e)
    else:
        forget = lb + (1.0 - lb) * (0.5 * jnp.tanh(0.5 * f) + 0.5)
        g = jnp.log(forget)
        k_s[rows, :] = 1.0 - forget
    g_s[rows, :] = g
    return g


def _hgrn_cumulative(g, rows, scan_scratch, fwd):
    b_s = scan_scratch[4]
    for c in range(ROW_BLOCK // SCAN_CHUNK):
        lo = c * SCAN_CHUNK
        b_s[rows.start + lo:rows.start + lo + SCAN_CHUNK, :] = _cumulative(g[lo:lo + SCAN_CHUNK], fwd)


def _scan_exact_loop(scan_scratch, st_s, fwd, emit):
    q_s, k_s, v_s, g_s, b_s, kc_s, bc_s = scan_scratch[:7]
    C = SCAN_CHUNK
    n_chunks = q_s.shape[0] // C
    n_sub = C // SUB_BLOCK
    sub_row = lax.broadcasted_iota(jnp.int32, (SUB_BLOCK, C), 0)
    sub_col = lax.broadcasted_iota(jnp.int32, (SUB_BLOCK, C), 1)
    last = C - 1 if fwd else 0

    def chunk_body(c, carry):
        idx = c if fwd else n_chunks - 1 - c
        base = pl.multiple_of(idx * C, C)
        rows = pl.ds(base, C)
        bc_s[...] = b_s[rows, :]
        kc_s[...] = k_s[rows, :]
        for hd in range(B_HEADS):
            lanes = slice(hd * B_DK, (hd + 1) * B_DK)
            q = q_s[rows, lanes]
            k = kc_s[:, lanes]
            v = v_s[rows, lanes]
            b = bc_s[:, lanes]
            st = st_s[hd]
            b_tot = b[last:last + 1]
            o = _dot_tb(q * jnp.exp(b), st)
            kd = k * jnp.exp(b_tot - b)
            st_s[hd] = st * jnp.exp(b_tot) + _dot_ta(v, kd)

            a_rows = []
            for i in range(n_sub):
                lo, hi = i * SUB_BLOCK, (i + 1) * SUB_BLOCK
                qi, bi = q[lo:hi], b[lo:hi]
                if fwd and i > 0:
                    r = b[lo - 1:lo]
                    ke = k[:lo] * jnp.exp(r - b[:lo])
                    ke = jnp.concatenate([ke, jnp.zeros((C - lo, B_DK), _F32)], axis=0)
                    a_i = _dot_tb(qi * jnp.exp(bi - r), ke)
                elif (not fwd) and i < n_sub - 1:
                    r = b[hi:hi + 1]
                    ke = k[hi:] * jnp.exp(r - b[hi:])
                    ke = jnp.concatenate([jnp.zeros((hi, B_DK), _F32), ke], axis=0)
                    a_i = _dot_tb(qi * jnp.exp(bi - r), ke)
                else:
                    a_i = jnp.zeros((SUB_BLOCK, C), _F32)
                for s in range(SUB_BLOCK):
                    j = lo + s
                    k_row = kc_s[j:j + 1, lanes]
                    b_row = bc_s[j:j + 1, lanes]
                    p = qi * k_row * jnp.exp(jnp.minimum(bi - b_row, 0.0))
                    a_i = jnp.where(sub_col == j, jnp.sum(p, axis=-1, keepdims=True), a_i)
                local = sub_col - lo
                if fwd:
                    keep = (sub_col < lo) | ((sub_col < hi) & (sub_row >= local))
                else:
                    keep = (sub_col >= hi) | ((sub_col >= lo) & (sub_row <= local))
                a_rows.append(jnp.where(keep, a_i, 0.0))
            a = jnp.concatenate(a_rows, axis=0)
            emit(rows, lanes, o + _dot(a, v))
        return carry

    lax.fori_loop(0, n_chunks, chunk_body, 0)


def _scan_reference_rows(fwd):
    return (SCAN_CHUNK // 2 - 1, SCAN_CHUNK - 1) if fwd else (SCAN_CHUNK // 2, 0)


def _scan_factored_stages(scan_scratch, st_s, fwd, emit):
    q_s, k_s, v_s, g_s, b_s, kc_s, bc_s, x_s, y_s, a_s, u_s = scan_scratch
    C = SCAN_CHUNK
    mid, last = _scan_reference_rows(fwd)
    row = lax.broadcasted_iota(jnp.int32, (C, C), 0)
    col = lax.broadcasted_iota(jnp.int32, (C, C), 1)
    causal = (col <= row) if fwd else (col >= row)
    chunk_rows = lambda c: slice(c * C, (c + 1) * C)
    head_lanes = lambda hd: slice(hd * B_DK, (hd + 1) * B_DK)

    def chunks(r):
        per_block = ROW_BLOCK // C
        ids = list(range(r * per_block, (r + 1) * per_block))
        return ids if fwd else ids[::-1]

    def scale(r, _):
        for c in chunks(r):
            rows = chunk_rows(c)
            b = b_s[rows, :]
            e = jnp.exp(b - b[mid:mid + 1])
            x_s[rows, :] = (q_s[rows, :] * e).astype(_BF)
            y_s[rows, :] = (k_s[rows, :] / e).astype(_BF)

    def products(r, _):
        for c in chunks(r):
            rows = chunk_rows(c)
            for hd in range(B_HEADS):
                lanes = head_lanes(hd)
                yh = y_s[rows, lanes]
                a_s[hd, rows, :] = jnp.where(causal, _dot_tb(x_s[rows, lanes], yh), 0.0).astype(_BF)
                u_s[c, hd] = _dot_ta(v_s[rows, lanes], yh)

    def states(r, _):
        for hd in range(B_HEADS):
            lanes = head_lanes(hd)
            st = st_s[hd]
            for c in chunks(r):
                rho = b_s[c * C + mid:c * C + mid + 1, lanes]
                b_tot = b_s[c * C + last:c * C + last + 1, lanes]
                u = u_s[c, hd]
                u_s[c, hd] = st * jnp.exp(rho)
                st = st * jnp.exp(b_tot) + u * jnp.exp(b_tot - rho)
            st_s[hd] = st

    def outputs(r, _):
        for c in chunks(r):
            rows = chunk_rows(c)
            for hd in range(B_HEADS):
                lanes = head_lanes(hd)
                emit(rows, lanes, _dot_tb(x_s[rows, lanes], u_s[c, hd])
                     + _dot(a_s[hd, rows, :], v_s[rows, lanes]))

    return scale, products, states, outputs


def _scan_factored_ok(scan_scratch, fwd):
    b_s = scan_scratch[4]
    C = SCAN_CHUNK
    mid, last = _scan_reference_rows(fwd)
    worst = jnp.zeros((1, B_WIDTH), _F32)
    for c in range(b_s.shape[0] // C):
        b_mid = b_s[c * C + mid:c * C + mid + 1, :]
        b_tot = b_s[c * C + last:c * C + last + 1, :]
        worst = jnp.maximum(worst, jnp.maximum(-b_mid, b_mid - b_tot))
    return jnp.max(worst) < FACTORED_DECAY_LIMIT


def _begin_tile(st_s, st0_s):
    @pl.when(pl.program_id(1) == 0)
    def _():
        st_s[...] = jnp.zeros_like(st_s)
    st0_s[...] = st_s[...]


def _bwd_kernel(layer, x_ref, gpre_ref, w_ref, lbraw_ref, o_ref, q_ref, v_ref, *scratch):
    (k_s, g_s), rest, (st_s, st0_s) = scratch[:2], scratch[2:-2], scratch[-2:]
    scan_scratch = (q_ref.at[0], k_s, v_ref.at[0], g_s) + tuple(rest)
    _begin_tile(st_s, st0_s)
    lb = _lower_bound(lbraw_ref, layer) if layer > 0 else None

    def project(r, _):
        h = _rms(x_ref[0, _row_block(r), :], gpre_ref[...]).astype(_BF)
        return jnp.dot(h, w_ref[...], preferred_element_type=_F32)

    def gates(r, p):
        rows = _row_block(r)
        q_ref[0, rows, :] = _silu(p[:, 0:512]) * (B_DK ** -0.5)
        v_ref[0, rows, :] = p[:, 1024:1536].astype(_BF)
        return _hgrn_gates(p[:, 512:1024], rows, lb, scan_scratch)

    def cumulate(r, g):
        _hgrn_cumulative(g, _row_block(r), scan_scratch, fwd=False)

    def emit(rows, lanes, o):
        o_ref[0, rows, lanes] = o

    blocks = reversed(range(x_ref.shape[1] // ROW_BLOCK))
    _skewed(blocks, project, gates, cumulate, *_scan_factored_stages(scan_scratch, st_s, False, emit))

    @pl.when(jnp.logical_not(_scan_factored_ok(scan_scratch, False)))
    def _():
        st_s[...] = st0_s[...]
        _scan_exact_loop(scan_scratch, st_s, False, emit)


def _mix_kernel(layer, x_ref, obwd_ref, q_ref, v_ref, gpre_ref, w_ref, vlng_ref, vlnb_ref, wsp_ref,
                bsp_ref, lbraw_ref, onorm_ref, wout_ref, gpost_ref, out_ref, *scratch):
    (k_s, g_s), rest, (gate_s, st_s, st0_s, cat_s) = scratch[:2], scratch[2:-4], scratch[-4:]
    scan_scratch = (q_ref.at[0], k_s, v_ref.at[0], g_s) + tuple(rest)
    _begin_tile(st_s, st0_s)
    lb = _lower_bound(lbraw_ref, layer) if layer > 0 else None
    blocks = range(x_ref.shape[1] // ROW_BLOCK)

    def project(r, _):
        h = _rms(x_ref[0, _row_block(r), :], gpre_ref[...]).astype(_BF)
        return jnp.dot(h, w_ref[...], preferred_element_type=_F32)

    def elementwise(r, p):
        rows = _row_block(r)
        u = _gelu(p[:, 0:512])
        v = _gelu(p[:, 512:1024])
        vc = v - jnp.mean(v, axis=-1, keepdims=True)
        v = vc * lax.rsqrt(jnp.mean(vc * vc, axis=-1, keepdims=True) + EPS) * vlng_ref[...] + vlnb_ref[...]
        g = _hgrn_gates(p[:, 1024:1536], rows, lb, scan_scratch)
        gate_s[rows, :] = _silu(p[:, 1536:2048])
        return u, v.astype(_BF), g

    def dependent_matmuls(r, uvg):
        rows = _row_block(r)
        u, v, g = uvg
        chunks = [slice(n * A_CHUNK, (n + 1) * A_CHUNK) for n in range(ROW_BLOCK // A_CHUNK)]
        for gi in range(A_GROUPS):
            lanes = slice(gi * A_GROUP_DIM, (gi + 1) * A_GROUP_DIM)
            sv = jnp.dot(wsp_ref[gi], jnp.concatenate([v[c, lanes] for c in chunks], axis=1),
                         preferred_element_type=_F32)
            for n, c in enumerate(chunks):
                cat_s[rows.start + c.start:rows.start + c.stop, lanes] = (
                    u[c, lanes] * (sv[:, n * A_GROUP_DIM:(n + 1) * A_GROUP_DIM] + bsp_ref[gi])).astype(_BF)
        _hgrn_cumulative(g, rows, scan_scratch, fwd=True)

    def emit(rows, lanes, o):
        o_sum = o + obwd_ref[0, rows, lanes]
        cat_s[rows, A_WIDTH + lanes.start:A_WIDTH + lanes.stop] = (
            _rms(o_sum, onorm_ref[...]) * gate_s[rows, lanes]).astype(_BF)

    def out_proj(r, _):
        return jnp.dot(cat_s[_row_block(r), :], wout_ref[...], preferred_element_type=_F32)

    def residual(r, y):
        rows = _row_block(r)
        out_ref[0, rows, :] = x_ref[0, rows, :] + _rms(y, gpost_ref[...])

    _skewed(blocks, project, elementwise, dependent_matmuls,
            *_scan_factored_stages(scan_scratch, st_s, True, emit), out_proj, residual)

    @pl.when(jnp.logical_not(_scan_factored_ok(scan_scratch, True)))
    def _():
        st_s[...] = st0_s[...]
        _scan_exact_loop(scan_scratch, st_s, True, emit)
        _skewed(blocks, out_proj, residual)


def _kv_kernel(mem_ref, g_ref, w_ref, k_ref, v_ref):
    m = _rms(mem_ref[0], g_ref[...])
    kv = _dot(m, w_ref[...])
    k_ref[0] = kv[:, :D_MODEL].astype(_BF)
    v_ref[0] = kv[:, D_MODEL:].astype(_BF)


def _xattn_kernel(x_ref, k_ref, v_ref, gpre_ref, wq_ref, wo_ref, gpost_ref, out_ref, cat_s):
    head_lanes = lambda hd: slice(hd * X_HEAD_DIM, (hd + 1) * X_HEAD_DIM)

    def query(r, _):
        h = _rms(x_ref[0, _row_block(r), :], gpre_ref[...])
        return _dot(h, wq_ref[...]).astype(_BF)

    def scores(r, q):
        return [_dot_tb(q[:, head_lanes(hd)], k_ref[0, :, head_lanes(hd)]) for hd in range(X_HEADS)]

    def softmax(r, s_heads):
        p_heads = []
        for s in s_heads:
            s = s * (X_HEAD_DIM ** -0.5)
            e = jnp.exp(s - jnp.max(s, axis=-1, keepdims=True))
            p_heads.append((e / jnp.sum(e, axis=-1, keepdims=True)).astype(_BF))
        return p_heads

    def values(r, p_heads):
        for hd, p in enumerate(p_heads):
            cat_s[_row_block(r), head_lanes(hd)] = _dot(p, v_ref[0, :, head_lanes(hd)]).astype(_BF)

    def out_proj(r, _):
        return jnp.dot(cat_s[_row_block(r), :], wo_ref[...], preferred_element_type=_F32)

    def residual(r, y):
        rows = _row_block(r)
        out_ref[0, rows, :] = x_ref[0, rows, :] + _rms(y, gpost_ref[...])

    _skewed(range(x_ref.shape[1] // ROW_BLOCK), query, scores, softmax, values, out_proj, residual)


def _ffn_kernel(x_ref, gpre_ref, wgu_ref, wdown_ref, gpost_ref, out_ref):
    def gate_up(r, _):
        h = _rms(x_ref[0, _row_block(r, FFN_ROW_BLOCK), :], gpre_ref[...]).astype(_BF)
        return jnp.dot(h, wgu_ref[...], preferred_element_type=_F32)

    def activate(r, p):
        return (_silu(p[:, :D_FF]) * p[:, D_FF:]).astype(_BF)

    def down(r, act):
        return jnp.dot(act, wdown_ref[...], preferred_element_type=_F32)

    def residual(r, y):
        rows = _row_block(r, FFN_ROW_BLOCK)
        out_ref[0, rows, :] = x_ref[0, rows, :] + _rms(y, gpost_ref[...])

    _skewed(range(x_ref.shape[1] // FFN_ROW_BLOCK), gate_up, activate, down, residual)


def _params():
    return pltpu.CompilerParams(dimension_semantics=("arbitrary", "arbitrary"),
                                vmem_limit_bytes=VMEM_LIMIT_BYTES)


def _const_spec(shape):
    return pl.BlockSpec(shape, lambda b, t: (0,) * len(shape), pipeline_mode=pl.Buffered(1))


def _tile_spec(width, rev_tiles=None, tile=SEQ_TILE):
    if rev_tiles is None:
        return pl.BlockSpec((1, tile, width), lambda b, t: (b, t, 0))
    return pl.BlockSpec((1, tile, width), lambda b, t: (b, rev_tiles - 1 - t, 0))


def _scan_scratch():
    tile = lambda dt: pltpu.VMEM((SEQ_TILE, B_WIDTH), dt)
    chunk = lambda: pltpu.VMEM((SCAN_CHUNK, B_WIDTH), _F32)
    return [tile(_F32), tile(_F32), tile(_F32), chunk(), chunk(), tile(_BF), tile(_BF),
            pltpu.VMEM((B_HEADS, SEQ_TILE, SCAN_CHUNK), _BF),
            pltpu.VMEM((SEQ_TILE // SCAN_CHUNK, B_HEADS, B_DK, B_DK), _F32)]


def _state_scratch():
    return pltpu.VMEM((B_HEADS, B_DK, B_DK), _F32)


def _bwd_call(x, layer, gpre, w_bwd, lbraw):
    bn, ln, _ = x.shape
    nt = ln // SEQ_TILE
    return pl.pallas_call(
        functools.partial(_bwd_kernel, layer),
        grid=(bn, nt),
        in_specs=[_tile_spec(D_MODEL, nt), _const_spec(gpre.shape), _const_spec(w_bwd.shape),
                  _const_spec(lbraw.shape)],
        out_specs=[_tile_spec(B_WIDTH, nt)] * 3,
        out_shape=[jax.ShapeDtypeStruct((bn, ln, B_WIDTH), _F32),
                   jax.ShapeDtypeStruct((bn, ln, B_WIDTH), _F32),
                   jax.ShapeDtypeStruct((bn, ln, B_WIDTH), _BF)],
        scratch_shapes=_scan_scratch() + [_state_scratch(), _state_scratch()],
        compiler_params=_params(),
        name=f"hgrn_bwd_l{layer}",
    )(x, gpre, w_bwd, lbraw)


def _mix_call(x, obwd, q, v, layer, gpre, w_mix, vlng, vlnb, wsp, bsp, lbraw, onorm, wout, gpost):
    bn, ln, _ = x.shape
    nt = ln // SEQ_TILE
    consts = (gpre, w_mix, vlng, vlnb, wsp, bsp, lbraw, onorm, wout, gpost)
    return pl.pallas_call(
        functools.partial(_mix_kernel, layer),
        grid=(bn, nt),
        in_specs=[_tile_spec(D_MODEL)] + [_tile_spec(B_WIDTH)] * 3 + [_const_spec(c.shape) for c in consts],
        out_specs=_tile_spec(D_MODEL),
        out_shape=jax.ShapeDtypeStruct(x.shape, _F32),
        scratch_shapes=_scan_scratch() + [pltpu.VMEM((SEQ_TILE, B_WIDTH), _F32), _state_scratch(),
                                          _state_scratch(),
                                          pltpu.VMEM((SEQ_TILE, D_MODEL), _BF)],
        compiler_params=_params(),
        name=f"token_mix_l{layer}",
    )(x, obwd, q, v, *consts)


def _kv_call(mem, layer, g, w):
    bn = mem.shape[0]
    spec = pl.BlockSpec((1, MEM_TOKENS, D_MODEL), lambda b, t: (b, 0, 0))
    return pl.pallas_call(
        _kv_kernel,
        grid=(bn, 1),
        in_specs=[spec, _const_spec(g.shape), _const_spec(w.shape)],
        out_specs=[spec, spec],
        out_shape=[jax.ShapeDtypeStruct(mem.shape, _BF)] * 2,
        compiler_params=_params(),
        name=f"mem_kv_l{layer}",
    )(mem, g, w)


def _xattn_call(x, k, v, layer, gpre, wq, wo, gpost):
    bn, ln, _ = x.shape
    nt = ln // XATTN_TILE
    mem_spec = pl.BlockSpec((1, MEM_TOKENS, D_MODEL), lambda b, t: (b, 0, 0))
    consts = (gpre, wq, wo, gpost)
    x_spec = _tile_spec(D_MODEL, tile=XATTN_TILE)
    return pl.pallas_call(
        _xattn_kernel,
        grid=(bn, nt),
        in_specs=[x_spec, mem_spec, mem_spec] + [_const_spec(c.shape) for c in consts],
        out_specs=x_spec,
        out_shape=jax.ShapeDtypeStruct(x.shape, _F32),
        scratch_shapes=[pltpu.VMEM((XATTN_TILE, D_MODEL), _BF)],
        compiler_params=_params(),
        name=f"mem_xattn_l{layer}",
    )(x, k, v, *consts)


def _ffn_call(x, layer, gpre, wgu, wdown, gpost):
    bn, ln, _ = x.shape
    nt = ln // SEQ_TILE
    consts = (gpre, wgu, wdown, gpost)
    return pl.pallas_call(
        _ffn_kernel,
        grid=(bn, nt),
        in_specs=[_tile_spec(D_MODEL)] + [_const_spec(c.shape) for c in consts],
        out_specs=_tile_spec(D_MODEL),
        out_shape=jax.ShapeDtypeStruct(x.shape, _F32),
        compiler_params=_params(),
        name=f"swiglu_ffn_l{layer}",
    )(x, *consts)


def _layer_weights(l, p):
    w_in = p['w_in'][l].astype(_BF)
    cols = lambda a, b: w_in[:, a:b]
    row = lambda a: a[l][None, :]
    return dict(
        w_mix=jnp.concatenate([cols(0, 1024), cols(1536, 2048), cols(3072, 3584)], axis=1),
        w_bwd=jnp.concatenate([cols(1024, 1536), cols(2048, 3072)], axis=1),
        pre_mix_g=row(p['pre_mix_g']), v_ln_g=row(p['v_ln_g']), v_ln_b=row(p['v_ln_b']),
        w_spatial=p['w_spatial'][l].astype(_BF),
        b_spatial=jnp.broadcast_to(p['b_spatial'][l][:, :, None], (A_GROUPS, A_CHUNK, A_GROUP_DIM)),
        lb_fwd=p['lb_raw'][0], lb_bwd=p['lb_raw'][1],
        onorm_g=row(p['onorm_g']), w_out=p['w_out'][l].astype(_BF), post_mix_g=row(p['post_mix_g']),
        pre_x_g=row(p['pre_x_g']), mem_norm_g=row(p['mem_norm_g']),
        w_xq=p['w_xq'][l].astype(_BF), w_xkv=p['w_xkv'][l].astype(_BF), w_xo=p['w_xo'][l].astype(_BF),
        post_x_g=row(p['post_x_g']), pre_ffn_g=row(p['pre_ffn_g']),
        w_gu=p['w_gu'][l].astype(_BF), w_down=p['w_down'][l].astype(_BF), post_ffn_g=row(p['post_ffn_g']),
    )


def _trunk(x, mem, layers):
    assert x.shape[1] % SEQ_TILE == 0 and x.shape[1] % XATTN_TILE == 0 and x.shape[2] == D_MODEL
    assert SEQ_TILE % ROW_BLOCK == 0 and ROW_BLOCK % SCAN_CHUNK == 0 and ROW_BLOCK % A_CHUNK == 0
    assert SEQ_TILE % FFN_ROW_BLOCK == 0
    for l, w in enumerate(layers):
        obwd, q, v = _bwd_call(x, l, w['pre_mix_g'], w['w_bwd'], w['lb_bwd'])
        x = _mix_call(x, obwd, q, v, l, w['pre_mix_g'], w['w_mix'], w['v_ln_g'], w['v_ln_b'], w['w_spatial'],
                      w['b_spatial'], w['lb_fwd'], w['onorm_g'], w['w_out'], w['post_mix_g'])
        k, v = _kv_call(mem, l, w['mem_norm_g'], w['w_xkv'])
        x = _xattn_call(x, k, v, l, w['pre_x_g'], w['w_xq'], w['w_xo'], w['post_x_g'])
        x = _ffn_call(x, l, w['pre_ffn_g'], w['w_gu'], w['w_down'], w['post_ffn_g'])
    return x


def kernel(x_prompt, x_sample, mem_prompt, mem_sample, pre_mix_g, w_in, v_ln_g, v_ln_b, w_spatial, b_spatial, lb_raw, onorm_g, w_out, post_mix_g, pre_x_g, mem_norm_g, w_xq, w_xkv, w_xo, post_x_g, pre_ffn_g, w_gu, w_down, post_ffn_g):
    p = dict(pre_mix_g=pre_mix_g, w_in=w_in, v_ln_g=v_ln_g, v_ln_b=v_ln_b,
             w_spatial=w_spatial, b_spatial=b_spatial, lb_raw=lb_raw, onorm_g=onorm_g,
             w_out=w_out, post_mix_g=post_mix_g, pre_x_g=pre_x_g, mem_norm_g=mem_norm_g,
             w_xq=w_xq, w_xkv=w_xkv, w_xo=w_xo, post_x_g=post_x_g, pre_ffn_g=pre_ffn_g,
             w_gu=w_gu, w_down=w_down, post_ffn_g=post_ffn_g)
    layers = [_layer_weights(l, p) for l in range(w_in.shape[0])]
    return (_trunk(x_prompt, mem_prompt, layers), _trunk(x_sample, mem_sample, layers))
```

```python
import functools

import jax
import jax.numpy as jnp
from jax import lax
from jax.experimental import pallas as pl
from jax.experimental.pallas import tpu as pltpu

D_MODEL = 1024
A_WIDTH = 512
A_GROUPS = 4
A_GROUP_DIM = 128
A_CHUNK = 128
B_WIDTH = 512
B_HEADS = 4
B_DK = 128
MEM_TOKENS = 256
X_HEADS = 4
X_HEAD_DIM = 256
D_FF = 2816
EPS = 1e-6

SEQ_TILE = 1024
XATTN_TILE = 2048
ROW_BLOCK = 512
FFN_ROW_BLOCK = 256
SCAN_CHUNK = 64
SUB_BLOCK = 16
FACTORED_DECAY_LIMIT = 55.0
VMEM_LIMIT_BYTES = 60 * 1024 * 1024

_BF = jnp.bfloat16
_F32 = jnp.float32


def _dot(a, b):
    return jnp.dot(a.astype(_BF), b.astype(_BF), preferred_element_type=_F32)


def _dot_tb(a, b):
    return lax.dot_general(a.astype(_BF), b.astype(_BF), (((1,), (1,)), ((), ())),
                           preferred_element_type=_F32)


def _dot_ta(a, b):
    return lax.dot_general(a.astype(_BF), b.astype(_BF), (((0,), (0,)), ((), ())),
                           preferred_element_type=_F32)


def _rms(x, g):
    return x * lax.rsqrt(jnp.mean(x * x, axis=-1, keepdims=True) + EPS) * g


def _gelu(x):
    return 0.5 * x * (1.0 + lax.erf(x * (2.0 ** -0.5)))


def _silu(x):
    return x * (0.5 * jnp.tanh(0.5 * x) + 0.5)


def _lower_bound(lbraw_ref, layer):
    depth = lbraw_ref.shape[0]
    rows = [lbraw_ref[j:j + 1, :] for j in range(depth)]
    m = functools.reduce(jnp.maximum, rows)
    e = [jnp.exp(r - m) for r in rows]
    denom = functools.reduce(lambda a, b: a + b, e)
    return functools.reduce(lambda a, b: a + b, e[1:layer + 1]) / denom


def _skewed(order, *stages):
    order = list(order)
    vals = {}
    for i in range(len(order) + len(stages) - 1):
        for k, stage in enumerate(stages):
            if 0 <= i - k < len(order):
                r = order[i - k]
                vals[(k, r)] = stage(r, vals.pop((k - 1, r), None))


def _row_block(r, size=ROW_BLOCK):
    return slice(r * size, (r + 1) * size)


def _split3(x):
    hi = x.astype(_BF)
    r1 = x - hi.astype(_F32)
    mid = r1.astype(_BF)
    lo = (r1 - mid.astype(_F32)).astype(_BF)
    return hi, mid, lo


def _cumulative(g, fwd):
    n = g.shape[0]
    row = lax.broadcasted_iota(jnp.int32, (n, n), 0)
    col = lax.broadcasted_iota(jnp.int32, (n, n), 1)
    tri = jnp.where(col <= row if fwd else col >= row, 1.0, 0.0).astype(_BF)
    g_hi, g_mid, g_lo = _split3(g)
    return (jnp.dot(tri, g_hi, preferred_element_type=_F32)
            + jnp.dot(tri, g_mid, preferred_element_type=_F32)
            + jnp.dot(tri, g_lo, preferred_element_type=_F32))


def _hgrn_gates(f, rows, lb, scan_scratch):
    k_s, g_s = scan_scratch[1], scan_scratch[3]
    if lb is None:
        e = jnp.exp(-jnp.abs(f))
        g = jnp.minimum(f, 0.0) - jnp.log(1.0 + e)
        k_s[rows, :] = jnp.where(f >= 0.0, e, 1.0) / (1.0 + e)
    else:
        forget = lb + (1.0 - lb) * (0.5 * jnp.tanh(0.5 * f) + 0.5)
        g = jnp.log(forget)
        k_s[rows, :] = 1.0 - forget
    g_s[rows, :] = g
    return g


def _hgrn_cumulative(g, rows, scan_scratch, fwd):
    b_s = scan_scratch[4]
    for c in range(ROW_BLOCK // SCAN_CHUNK):
        lo = c * SCAN_CHUNK
        b_s[rows.start + lo:rows.start + lo + SCAN_CHUNK, :] = _cumulative(g[lo:lo + SCAN_CHUNK], fwd)


def _scan_exact_loop(scan_scratch, st_s, fwd, emit):
    q_s, k_s, v_s, g_s, b_s, kc_s, bc_s = scan_scratch[:7]
    C = SCAN_CHUNK
    n_chunks = q_s.shape[0] // C
    n_sub = C // SUB_BLOCK
    sub_row = lax.broadcasted_iota(jnp.int32, (SUB_BLOCK, C), 0)
    sub_col = lax.broadcasted_iota(jnp.int32, (SUB_BLOCK, C), 1)
    last = C - 1 if fwd else 0

    def chunk_body(c, carry):
        idx = c if fwd else n_chunks - 1 - c
        base = pl.multiple_of(idx * C, C)
        rows = pl.ds(base, C)
        bc_s[...] = b_s[rows, :]
        kc_s[...] = k_s[rows, :]
        for hd in range(B_HEADS):
            lanes = slice(hd * B_DK, (hd + 1) * B_DK)
            q = q_s[rows, lanes]
            k = kc_s[:, lanes]
            v = v_s[rows, lanes]
            b = bc_s[:, lanes]
            st = st_s[hd]
            b_tot = b[last:last + 1]
            o = _dot_tb(q * jnp.exp(b), st)
            kd = k * jnp.exp(b_tot - b)
            st_s[hd] = st * jnp.exp(b_tot) + _dot_ta(v, kd)

            a_rows = []
            for i in range(n_sub):
                lo, hi = i * SUB_BLOCK, (i + 1) * SUB_BLOCK
                qi, bi = q[lo:hi], b[lo:hi]
                if fwd and i > 0:
                    r = b[lo - 1:lo]
                    ke = k[:lo] * jnp.exp(r - b[:lo])
                    ke = jnp.concatenate([ke, jnp.zeros((C - lo, B_DK), _F32)], axis=0)
                    a_i = _dot_tb(qi * jnp.exp(bi - r), ke)
                elif (not fwd) and i < n_sub - 1:
                    r = b[hi:hi + 1]
                    ke = k[hi:] * jnp.exp(r - b[hi:])
                    ke = jnp.concatenate([jnp.zeros((hi, B_DK), _F32), ke], axis=0)
                    a_i = _dot_tb(qi * jnp.exp(bi - r), ke)
                else:
                    a_i = jnp.zeros((SUB_BLOCK, C), _F32)
                for s in range(SUB_BLOCK):
                    j = lo + s
                    k_row = kc_s[j:j + 1, lanes]
                    b_row = bc_s[j:j + 1, lanes]
                    p = qi * k_row * jnp.exp(jnp.minimum(bi - b_row, 0.0))
                    a_i = jnp.where(sub_col == j, jnp.sum(p, axis=-1, keepdims=True), a_i)
                local = sub_col - lo
                if fwd:
                    keep = (sub_col < lo) | ((sub_col < hi) & (sub_row >= local))
                else:
                    keep = (sub_col >= hi) | ((sub_col >= lo) & (sub_row <= local))
                a_rows.append(jnp.where(keep, a_i, 0.0))
            a = jnp.concatenate(a_rows, axis=0)
            emit(rows, lanes, o + _dot(a, v))
        return carry

    lax.fori_loop(0, n_chunks, chunk_body, 0)


def _scan_reference_rows(fwd):
    return (SCAN_CHUNK // 2 - 1, SCAN_CHUNK - 1) if fwd else (SCAN_CHUNK // 2, 0)


def _scan_factored_stages(scan_scratch, st_s, fwd, emit):
    q_s, k_s, v_s, g_s, b_s, kc_s, bc_s, x_s, y_s, a_s, u_s = scan_scratch
    C = SCAN_CHUNK
    mid, last = _scan_reference_rows(fwd)
    row = lax.broadcasted_iota(jnp.int32, (C, C), 0)
    col = lax.broadcasted_iota(jnp.int32, (C, C), 1)
    causal = (col <= row) if fwd else (col >= row)
    chunk_rows = lambda c: slice(c * C, (c + 1) * C)
    head_lanes = lambda hd: slice(hd * B_DK, (hd + 1) * B_DK)

    def chunks(r):
        per_block = ROW_BLOCK // C
        ids = list(range(r * per_block, (r + 1) * per_block))
        return ids if fwd else ids[::-1]

    def scale(r, _):
        for c in chunks(r):
            rows = chunk_rows(c)
            b = b_s[rows, :]
            e = jnp.exp(b - b[mid:mid + 1])
            x_s[rows, :] = (q_s[rows, :] * e).astype(_BF)
            y_s[rows, :] = (k_s[rows, :] / e).astype(_BF)

    def products(r, _):
        for c in chunks(r):
            rows = chunk_rows(c)
            for hd in range(B_HEADS):
                lanes = head_lanes(hd)
                yh = y_s[rows, lanes]
                a_s[hd, rows, :] = jnp.where(causal, _dot_tb(x_s[rows, lanes], yh), 0.0).astype(_BF)
                u_s[c, hd] = _dot_ta(v_s[rows, lanes], yh)

    def states(r, _):
        for hd in range(B_HEADS):
            lanes = head_lanes(hd)
            st = st_s[hd]
            for c in chunks(r):
                rho = b_s[c * C + mid:c * C + mid + 1, lanes]
                b_tot = b_s[c * C + last:c * C + last + 1, lanes]
                u = u_s[c, hd]
                u_s[c, hd] = st * jnp.exp(rho)
                st = st * jnp.exp(b_tot) + u * jnp.exp(b_tot - rho)
            st_s[hd] = st

    def outputs(r, _):
        for c in chunks(r):
            rows = chunk_rows(c)
            for hd in range(B_HEADS):
                lanes = head_lanes(hd)
                emit(rows, lanes, _dot_tb(x_s[rows, lanes], u_s[c, hd])
                     + _dot(a_s[hd, rows, :], v_s[rows, lanes]))

    return scale, products, states, outputs


def _scan_factored_ok(scan_scratch, fwd):
    b_s = scan_scratch[4]
    C = SCAN_CHUNK
    mid, last = _scan_reference_rows(fwd)
    worst = jnp.zeros((1, B_WIDTH), _F32)
    for c in range(b_s.shape[0] // C):
        b_mid = b_s[c * C + mid:c * C + mid + 1, :]
        b_tot = b_s[c * C + last:c * C + last + 1, :]
        worst = jnp.maximum(worst, jnp.maximum(-b_mid, b_mid - b_tot))
    return jnp.max(worst) < FACTORED_DECAY_LIMIT


def _begin_tile(st_s, st0_s):
    @pl.when(pl.program_id(1) == 0)
    def _():
        st_s[...] = jnp.zeros_like(st_s)
    st0_s[...] = st_s[...]


def _bwd_kernel(layer, x_ref, gpre_ref, w_ref, lbraw_ref, o_ref, q_ref, v_ref, *scratch):
    (k_s, g_s), rest, (st_s, st0_s) = scratch[:2], scratch[2:-2], scratch[-2:]
    scan_scratch = (q_ref.at[0], k_s, v_ref.at[0], g_s) + tuple(rest)
    _begin_tile(st_s, st0_s)
    lb = _lower_bound(lbraw_ref, layer) if layer > 0 else None

    def project(r, _):
        h = _rms(x_ref[0, _row_block(r), :], gpre_ref[...]).astype(_BF)
        return jnp.dot(h, w_ref[...], preferred_element_type=_F32)

    def gates(r, p):
        rows = _row_block(r)
        q_ref[0, rows, :] = _silu(p[:, 0:512]) * (B_DK ** -0.5)
        v_ref[0, rows, :] = p[:, 1024:1536].astype(_BF)
        return _hgrn_gates(p[:, 512:1024], rows, lb, scan_scratch)

    factored_ok = []

    def cumulate(r, g):
        _hgrn_cumulative(g, _row_block(r), scan_scratch, fwd=False)
        if r == 0:
            factored_ok.append(_scan_factored_ok(scan_scratch, False))

    def emit(rows, lanes, o):
        o_ref[0, rows, lanes] = o

    blocks = reversed(range(x_ref.shape[1] // ROW_BLOCK))
    _skewed(blocks, project, gates, cumulate, *_scan_factored_stages(scan_scratch, st_s, False, emit))

    @pl.when(jnp.logical_not(factored_ok[0]))
    def _():
        st_s[...] = st0_s[...]
        _scan_exact_loop(scan_scratch, st_s, False, emit)


def _mix_kernel(layer, x_ref, obwd_ref, q_ref, v_ref, gpre_ref, w_ref, vlng_ref, vlnb_ref, wsp_ref,
                bsp_ref, lbraw_ref, onorm_ref, wout_ref, gpost_ref, out_ref, *scratch):
    (k_s, g_s), rest, (gate_s, st_s, st0_s, cat_s) = scratch[:2], scratch[2:-4], scratch[-4:]
    scan_scratch = (q_ref.at[0], k_s, v_ref.at[0], g_s) + tuple(rest)
    _begin_tile(st_s, st0_s)
    lb = _lower_bound(lbraw_ref, layer) if layer > 0 else None
    blocks = range(x_ref.shape[1] // ROW_BLOCK)
    factored_ok = []

    def project(r, _):
        h = _rms(x_ref[0, _row_block(r), :], gpre_ref[...]).astype(_BF)
        return jnp.dot(h, w_ref[...], preferred_element_type=_F32)

    def elementwise(r, p):
        rows = _row_block(r)
        u = _gelu(p[:, 0:512])
        v = _gelu(p[:, 512:1024])
        vc = v - jnp.mean(v, axis=-1, keepdims=True)
        v = vc * lax.rsqrt(jnp.mean(vc * vc, axis=-1, keepdims=True) + EPS) * vlng_ref[...] + vlnb_ref[...]
        g = _hgrn_gates(p[:, 1024:1536], rows, lb, scan_scratch)
        gate_s[rows, :] = _silu(p[:, 1536:2048])
        return u, v.astype(_BF), g

    def dependent_matmuls(r, uvg):
        rows = _row_block(r)
        u, v, g = uvg
        chunks = [slice(n * A_CHUNK, (n + 1) * A_CHUNK) for n in range(ROW_BLOCK // A_CHUNK)]
        for gi in range(A_GROUPS):
            lanes = slice(gi * A_GROUP_DIM, (gi + 1) * A_GROUP_DIM)
            sv = jnp.dot(wsp_ref[gi], jnp.concatenate([v[c, lanes] for c in chunks], axis=1),
                         preferred_element_type=_F32)
            for n, c in enumerate(chunks):
                cat_s[rows.start + c.start:rows.start + c.stop, lanes] = (
                    u[c, lanes] * (sv[:, n * A_GROUP_DIM:(n + 1) * A_GROUP_DIM] + bsp_ref[gi])).astype(_BF)
        _hgrn_cumulative(g, rows, scan_scratch, fwd=True)
        if r == blocks[-1]:
            factored_ok.append(_scan_factored_ok(scan_scratch, True))

    def emit(rows, lanes, o):
        o_sum = o + obwd_ref[0, rows, lanes]
        cat_s[rows, A_WIDTH + lanes.start:A_WIDTH + lanes.stop] = (
            _rms(o_sum, onorm_ref[...]) * gate_s[rows, lanes]).astype(_BF)

    def out_proj(r, _):
        return jnp.dot(cat_s[_row_block(r), :], wout_ref[...], preferred_element_type=_F32)

    def residual(r, y):
        rows = _row_block(r)
        out_ref[0, rows, :] = x_ref[0, rows, :] + _rms(y, gpost_ref[...])

    _skewed(blocks, project, elementwise, dependent_matmuls,
            *_scan_factored_stages(scan_scratch, st_s, True, emit), out_proj, residual)

    @pl.when(jnp.logical_not(factored_ok[0]))
    def _():
        st_s[...] = st0_s[...]
        _scan_exact_loop(scan_scratch, st_s, True, emit)
        _skewed(blocks, out_proj, residual)


def _kv_kernel(mem_ref, g_ref, w_ref, k_ref, v_ref):
    m = _rms(mem_ref[0], g_ref[...])
    kv = _dot(m, w_ref[...])
    k_ref[0] = kv[:, :D_MODEL].astype(_BF)
    v_ref[0] = kv[:, D_MODEL:].astype(_BF)


def _xattn_kernel(x_ref, k_ref, v_ref, gpre_ref, wq_ref, wo_ref, gpost_ref, out_ref, cat_s):
    head_lanes = lambda hd: slice(hd * X_HEAD_DIM, (hd + 1) * X_HEAD_DIM)

    def query(r, _):
        h = _rms(x_ref[0, _row_block(r), :], gpre_ref[...])
        return _dot(h, wq_ref[...]).astype(_BF)

    def scores(r, q):
        return [_dot_tb(q[:, head_lanes(hd)], k_ref[0, :, head_lanes(hd)]) for hd in range(X_HEADS)]

    def softmax(r, s_heads):
        p_heads = []
        for s in s_heads:
            s = s * (X_HEAD_DIM ** -0.5)
            e = jnp.exp(s - jnp.max(s, axis=-1, keepdims=True))
            p_heads.append((e / jnp.sum(e, axis=-1, keepdims=True)).astype(_BF))
        return p_heads

    def values(r, p_heads):
        for hd, p in enumerate(p_heads):
            cat_s[_row_block(r), head_lanes(hd)] = _dot(p, v_ref[0, :, head_lanes(hd)]).astype(_BF)

    def out_proj(r, _):
        return jnp.dot(cat_s[_row_block(r), :], wo_ref[...], preferred_element_type=_F32)

    def residual(r, y):
        rows = _row_block(r)
        out_ref[0, rows, :] = x_ref[0, rows, :] + _rms(y, gpost_ref[...])

    _skewed(range(x_ref.shape[1] // ROW_BLOCK), query, scores, softmax, values, out_proj, residual)


def _ffn_kernel(x_ref, gpre_ref, wgu_ref, wdown_ref, gpost_ref, out_ref):
    def gate_up(r, _):
        h = _rms(x_ref[0, _row_block(r, FFN_ROW_BLOCK), :], gpre_ref[...]).astype(_BF)
        return jnp.dot(h, wgu_ref[...], preferred_element_type=_F32)

    def activate(r, p):
        return (_silu(p[:, :D_FF]) * p[:, D_FF:]).astype(_BF)

    def down(r, act):
        return jnp.dot(act, wdown_ref[...], preferred_element_type=_F32)

    def residual(r, y):
        rows = _row_block(r, FFN_ROW_BLOCK)
        out_ref[0, rows, :] = x_ref[0, rows, :] + _rms(y, gpost_ref[...])

    _skewed(range(x_ref.shape[1] // FFN_ROW_BLOCK), gate_up, activate, down, residual)


def _params():
    return pltpu.CompilerParams(dimension_semantics=("arbitrary", "arbitrary"),
                                vmem_limit_bytes=VMEM_LIMIT_BYTES)


def _const_spec(shape):
    return pl.BlockSpec(shape, lambda b, t: (0,) * len(shape), pipeline_mode=pl.Buffered(1))


def _tile_spec(width, rev_tiles=None, tile=SEQ_TILE):
    if rev_tiles is None:
        return pl.BlockSpec((1, tile, width), lambda b, t: (b, t, 0))
    return pl.BlockSpec((1, tile, width), lambda b, t: (b, rev_tiles - 1 - t, 0))


def _scan_scratch():
    tile = lambda dt: pltpu.VMEM((SEQ_TILE, B_WIDTH), dt)
    chunk = lambda: pltpu.VMEM((SCAN_CHUNK, B_WIDTH), _F32)
    return [tile(_F32), tile(_F32), tile(_F32), chunk(), chunk(), tile(_BF), tile(_BF),
            pltpu.VMEM((B_HEADS, SEQ_TILE, SCAN_CHUNK), _BF),
            pltpu.VMEM((SEQ_TILE // SCAN_CHUNK, B_HEADS, B_DK, B_DK), _F32)]


def _state_scratch():
    return pltpu.VMEM((B_HEADS, B_DK, B_DK), _F32)


def _bwd_call(x, layer, gpre, w_bwd, lbraw):
    bn, ln, _ = x.shape
    nt = ln // SEQ_TILE
    return pl.pallas_call(
        functools.partial(_bwd_kernel, layer),
        grid=(bn, nt),
        in_specs=[_tile_spec(D_MODEL, nt), _const_spec(gpre.shape), _const_spec(w_bwd.shape),
                  _const_spec(lbraw.shape)],
        out_specs=[_tile_spec(B_WIDTH, nt)] * 3,
        out_shape=[jax.ShapeDtypeStruct((bn, ln, B_WIDTH), _F32),
                   jax.ShapeDtypeStruct((bn, ln, B_WIDTH), _F32),
                   jax.ShapeDtypeStruct((bn, ln, B_WIDTH), _BF)],
        scratch_shapes=_scan_scratch() + [_state_scratch(), _state_scratch()],
        compiler_params=_params(),
        name=f"hgrn_bwd_l{layer}",
    )(x, gpre, w_bwd, lbraw)


def _mix_call(x, obwd, q, v, layer, gpre, w_mix, vlng, vlnb, wsp, bsp, lbraw, onorm, wout, gpost):
    bn, ln, _ = x.shape
    nt = ln // SEQ_TILE
    consts = (gpre, w_mix, vlng, vlnb, wsp, bsp, lbraw, onorm, wout, gpost)
    return pl.pallas_call(
        functools.partial(_mix_kernel, layer),
        grid=(bn, nt),
        in_specs=[_tile_spec(D_MODEL)] + [_tile_spec(B_WIDTH)] * 3 + [_const_spec(c.shape) for c in consts],
        out_specs=_tile_spec(D_MODEL),
        out_shape=jax.ShapeDtypeStruct(x.shape, _F32),
        scratch_shapes=_scan_scratch() + [pltpu.VMEM((SEQ_TILE, B_WIDTH), _F32), _state_scratch(),
                                          _state_scratch(),
                                          pltpu.VMEM((SEQ_TILE, D_MODEL), _BF)],
        compiler_params=_params(),
        name=f"token_mix_l{layer}",
    )(x, obwd, q, v, *consts)


def _kv_call(mem, layer, g, w):
    bn = mem.shape[0]
    spec = pl.BlockSpec((1, MEM_TOKENS, D_MODEL), lambda b, t: (b, 0, 0))
    return pl.pallas_call(
        _kv_kernel,
        grid=(bn, 1),
        in_specs=[spec, _const_spec(g.shape), _const_spec(w.shape)],
        out_specs=[spec, spec],
        out_shape=[jax.ShapeDtypeStruct(mem.shape, _BF)] * 2,
        compiler_params=_params(),
        name=f"mem_kv_l{layer}",
    )(mem, g, w)


def _xattn_call(x, k, v, layer, gpre, wq, wo, gpost):
    bn, ln, _ = x.shape
    nt = ln // XATTN_TILE
    mem_spec = pl.BlockSpec((1, MEM_TOKENS, D_MODEL), lambda b, t: (b, 0, 0))
    consts = (gpre, wq, wo, gpost)
    x_spec = _tile_spec(D_MODEL, tile=XATTN_TILE)
    return pl.pallas_call(
        _xattn_kernel,
        grid=(bn, nt),
        in_specs=[x_spec, mem_spec, mem_spec] + [_const_spec(c.shape) for c in consts],
        out_specs=x_spec,
        out_shape=jax.ShapeDtypeStruct(x.shape, _F32),
        scratch_shapes=[pltpu.VMEM((XATTN_TILE, D_MODEL), _BF)],
        compiler_params=_params(),
        name=f"mem_xattn_l{layer}",
    )(x, k, v, *consts)


def _ffn_call(x, layer, gpre, wgu, wdown, gpost):
    bn, ln, _ = x.shape
    nt = ln // SEQ_TILE
    consts = (gpre, wgu, wdown, gpost)
    return pl.pallas_call(
        _ffn_kernel,
        grid=(bn, nt),
        in_specs=[_tile_spec(D_MODEL)] + [_const_spec(c.shape) for c in consts],
        out_specs=_tile_spec(D_MODEL),
        out_shape=jax.ShapeDtypeStruct(x.shape, _F32),
        compiler_params=_params(),
        name=f"swiglu_ffn_l{layer}",
    )(x, *consts)


def _layer_weights(l, p):
    w_in = p['w_in'][l].astype(_BF)
    cols = lambda a, b: w_in[:, a:b]
    row = lambda a: a[l][None, :]
    return dict(
        w_mix=jnp.concatenate([cols(0, 1024), cols(1536, 2048), cols(3072, 3584)], axis=1),
        w_bwd=jnp.concatenate([cols(1024, 1536), cols(2048, 3072)], axis=1),
        pre_mix_g=row(p['pre_mix_g']), v_ln_g=row(p['v_ln_g']), v_ln_b=row(p['v_ln_b']),
        w_spatial=p['w_spatial'][l].astype(_BF),
        b_spatial=jnp.broadcast_to(p['b_spatial'][l][:, :, None], (A_GROUPS, A_CHUNK, A_GROUP_DIM)),
        lb_fwd=p['lb_raw'][0], lb_bwd=p['lb_raw'][1],
        onorm_g=row(p['onorm_g']), w_out=p['w_out'][l].astype(_BF), post_mix_g=row(p['post_mix_g']),
        pre_x_g=row(p['pre_x_g']), mem_norm_g=row(p['mem_norm_g']),
        w_xq=p['w_xq'][l].astype(_BF), w_xkv=p['w_xkv'][l].astype(_BF), w_xo=p['w_xo'][l].astype(_BF),
        post_x_g=row(p['post_x_g']), pre_ffn_g=row(p['pre_ffn_g']),
        w_gu=p['w_gu'][l].astype(_BF), w_down=p['w_down'][l].astype(_BF), post_ffn_g=row(p['post_ffn_g']),
    )


def _trunk(x, mem, layers):
    assert x.shape[1] % SEQ_TILE == 0 and x.shape[1] % XATTN_TILE == 0 and x.shape[2] == D_MODEL
    assert SEQ_TILE % ROW_BLOCK == 0 and ROW_BLOCK % SCAN_CHUNK == 0 and ROW_BLOCK % A_CHUNK == 0
    assert SEQ_TILE % FFN_ROW_BLOCK == 0
    for l, w in enumerate(layers):
        obwd, q, v = _bwd_call(x, l, w['pre_mix_g'], w['w_bwd'], w['lb_bwd'])
        x = _mix_call(x, obwd, q, v, l, w['pre_mix_g'], w['w_mix'], w['v_ln_g'], w['v_ln_b'], w['w_spatial'],
                      w['b_spatial'], w['lb_fwd'], w['onorm_g'], w['w_out'], w['post_mix_g'])
        k, v = _kv_call(mem, l, w['mem_norm_g'], w['w_xkv'])
        x = _xattn_call(x, k, v, l, w['pre_x_g'], w['w_xq'], w['w_xo'], w['post_x_g'])
        x = _ffn_call(x, l, w['pre_ffn_g'], w['w_gu'], w['w_down'], w['post_ffn_g'])
    return x


def kernel(x_prompt, x_sample, mem_prompt, mem_sample, pre_mix_g, w_in, v_ln_g, v_ln_b, w_spatial, b_spatial, lb_raw, onorm_g, w_out, post_mix_g, pre_x_g, mem_norm_g, w_xq, w_xkv, w_xo, post_x_g, pre_ffn_g, w_gu, w_down, post_ffn_g):
    p = dict(pre_mix_g=pre_mix_g, w_in=w_in, v_ln_g=v_ln_g, v_ln_b=v_ln_b,
             w_spatial=w_spatial, b_spatial=b_spatial, lb_raw=lb_raw, onorm_g=onorm_g,
             w_out=w_out, post_mix_g=post_mix_g, pre_x_g=pre_x_g, mem_norm_g=mem_norm_g,
             w_xq=w_xq, w_xkv=w_xkv, w_xo=w_xo, post_x_g=post_x_g, pre_ffn_g=pre_ffn_g,
             w_gu=w_gu, w_down=w_down, post_ffn_g=post_ffn_g)
    layers = [_layer_weights(l, p) for l in range(w_in.shape[0])]
    return (_trunk(x_prompt, mem_prompt, layers), _trunk(x_sample, mem_sample, layers))
```

```python
import functools

import jax
import jax.numpy as jnp
from jax import lax
from jax.experimental import pallas as pl
from jax.experimental.pallas import tpu as pltpu

D_MODEL = 1024
A_WIDTH = 512
A_GROUPS = 4
A_GROUP_DIM = 128
A_CHUNK = 128
B_WIDTH = 512
B_HEADS = 4
B_DK = 128
MEM_TOKENS = 256
X_HEADS = 4
X_HEAD_DIM = 256
D_FF = 2816
EPS = 1e-6

SEQ_TILE = 1024
XATTN_TILE = 2048
ROW_BLOCK = 512
FFN_ROW_BLOCK = 256
SCAN_CHUNK = 64
SUB_BLOCK = 16
FACTORED_DECAY_LIMIT = 55.0
VMEM_LIMIT_BYTES = 60 * 1024 * 1024

_BF = jnp.bfloat16
_F32 = jnp.float32


def _dot(a, b):
    return jnp.dot(a.astype(_BF), b.astype(_BF), preferred_element_type=_F32)


def _dot_tb(a, b):
    return lax.dot_general(a.astype(_BF), b.astype(_BF), (((1,), (1,)), ((), ())),
                           preferred_element_type=_F32)


def _dot_ta(a, b):
    return lax.dot_general(a.astype(_BF), b.astype(_BF), (((0,), (0,)), ((), ())),
                           preferred_element_type=_F32)


def _rms(x, g):
    return x * lax.rsqrt(jnp.mean(x * x, axis=-1, keepdims=True) + EPS) * g


def _gelu(x):
    return 0.5 * x * (1.0 + lax.erf(x * (2.0 ** -0.5)))


def _silu(x):
    return x * (0.5 * jnp.tanh(0.5 * x) + 0.5)


def _lower_bound(lbraw_ref, layer):
    depth = lbraw_ref.shape[0]
    rows = [lbraw_ref[j:j + 1, :] for j in range(depth)]
    m = functools.reduce(jnp.maximum, rows)
    e = [jnp.exp(r - m) for r in rows]
    denom = functools.reduce(lambda a, b: a + b, e)
    return functools.reduce(lambda a, b: a + b, e[1:layer + 1]) / denom


def _skewed(order, *stages):
    order = list(order)
    vals = {}
    for i in range(len(order) + len(stages) - 1):
        for k, stage in enumerate(stages):
            if 0 <= i - k < len(order):
                r = order[i - k]
                vals[(k, r)] = stage(r, vals.pop((k - 1, r), None))


def _row_block(r, size=ROW_BLOCK):
    return slice(r * size, (r + 1) * size)


def _split3(x):
    hi = x.astype(_BF)
    r1 = x - hi.astype(_F32)
    mid = r1.astype(_BF)
    lo = (r1 - mid.astype(_F32)).astype(_BF)
    return hi, mid, lo


def _cumulative(g, fwd):
    n = g.shape[0]
    row = lax.broadcasted_iota(jnp.int32, (n, n), 0)
    col = lax.broadcasted_iota(jnp.int32, (n, n), 1)
    tri = jnp.where(col <= row if fwd else col >= row, 1.0, 0.0).astype(_BF)
    g_hi, g_mid, g_lo = _split3(g)
    return (jnp.dot(tri, g_hi, preferred_element_type=_F32)
            + jnp.dot(tri, g_mid, preferred_element_type=_F32)
            + jnp.dot(tri, g_lo, preferred_element_type=_F32))


def _hgrn_gates(f, rows, lb, scan_scratch):
    k_s, g_s = scan_scratch[1], scan_scratch[3]
    if lb is None:
        e = jnp.exp(-jnp.abs(f))
        g = jnp.minimum(f, 0.0) - jnp.log(1.0 + e)
        k_s[rows, :] = jnp.where(f >= 0.0, e, 1.0) / (1.0 + e)
    else:
        forget = lb + (1.0 - lb) * (0.5 * jnp.tanh(0.5 * f) + 0.5)
        g = jnp.log(forget)
        k_s[rows, :] = 1.0 - forget
    g_s[rows, :] = g
    return g


def _hgrn_cumulative(g, rows, scan_scratch, fwd):
    b_s = scan_scratch[4]
    for c in range(ROW_BLOCK // SCAN_CHUNK):
        lo = c * SCAN_CHUNK
        b_s[rows.start + lo:rows.start + lo + SCAN_CHUNK, :] = _cumulative(g[lo:lo + SCAN_CHUNK], fwd)


def _scan_exact_loop(scan_scratch, st_s, fwd, emit):
    q_s, k_s, v_s, g_s, b_s, kc_s, bc_s = scan_scratch[:7]
    C = SCAN_CHUNK
    n_chunks = q_s.shape[0] // C
    n_sub = C // SUB_BLOCK
    sub_row = lax.broadcasted_iota(jnp.int32, (SUB_BLOCK, C), 0)
    sub_col = lax.broadcasted_iota(jnp.int32, (SUB_BLOCK, C), 1)
    last = C - 1 if fwd else 0

    def chunk_body(c, carry):
        idx = c if fwd else n_chunks - 1 - c
        base = pl.multiple_of(idx * C, C)
        rows = pl.ds(base, C)
        bc_s[...] = b_s[rows, :]
        kc_s[...] = k_s[rows, :]
        for hd in range(B_HEADS):
            lanes = slice(hd * B_DK, (hd + 1) * B_DK)
            q = q_s[rows, lanes]
            k = kc_s[:, lanes]
            v = v_s[rows, lanes]
            b = bc_s[:, lanes]
            st = st_s[hd]
            b_tot = b[last:last + 1]
            o = _dot_tb(q * jnp.exp(b), st)
            kd = k * jnp.exp(b_tot - b)
            st_s[hd] = st * jnp.exp(b_tot) + _dot_ta(v, kd)

            a_rows = []
            for i in range(n_sub):
                lo, hi = i * SUB_BLOCK, (i + 1) * SUB_BLOCK
                qi, bi = q[lo:hi], b[lo:hi]
                if fwd and i > 0:
                    r = b[lo - 1:lo]
                    ke = k[:lo] * jnp.exp(r - b[:lo])
                    ke = jnp.concatenate([ke, jnp.zeros((C - lo, B_DK), _F32)], axis=0)
                    a_i = _dot_tb(qi * jnp.exp(bi - r), ke)
                elif (not fwd) and i < n_sub - 1:
                    r = b[hi:hi + 1]
                    ke = k[hi:] * jnp.exp(r - b[hi:])
                    ke = jnp.concatenate([jnp.zeros((hi, B_DK), _F32), ke], axis=0)
                    a_i = _dot_tb(qi * jnp.exp(bi - r), ke)
                else:
                    a_i = jnp.zeros((SUB_BLOCK, C), _F32)
                for s in range(SUB_BLOCK):
                    j = lo + s
                    k_row = kc_s[j:j + 1, lanes]
                    b_row = bc_s[j:j + 1, lanes]
                    p = qi * k_row * jnp.exp(jnp.minimum(bi - b_row, 0.0))
                    a_i = jnp.where(sub_col == j, jnp.sum(p, axis=-1, keepdims=True), a_i)
                local = sub_col - lo
                if fwd:
                    keep = (sub_col < lo) | ((sub_col < hi) & (sub_row >= local))
                else:
                    keep = (sub_col >= hi) | ((sub_col >= lo) & (sub_row <= local))
                a_rows.append(jnp.where(keep, a_i, 0.0))
            a = jnp.concatenate(a_rows, axis=0)
            emit(rows, lanes, o + _dot(a, v))
        return carry

    lax.fori_loop(0, n_chunks, chunk_body, 0)


def _scan_reference_rows(fwd):
    return (SCAN_CHUNK // 2 - 1, SCAN_CHUNK - 1) if fwd else (SCAN_CHUNK // 2, 0)


def _scan_factored_stages(scan_scratch, st_s, fwd, emit):
    q_s, k_s, v_s, g_s, b_s, kc_s, bc_s, x_s, y_s, a_s, u_s = scan_scratch
    C = SCAN_CHUNK
    mid, last = _scan_reference_rows(fwd)
    row = lax.broadcasted_iota(jnp.int32, (C, C), 0)
    col = lax.broadcasted_iota(jnp.int32, (C, C), 1)
    causal = (col <= row) if fwd else (col >= row)
    chunk_rows = lambda c: slice(c * C, (c + 1) * C)
    head_lanes = lambda hd: slice(hd * B_DK, (hd + 1) * B_DK)

    def chunks(r):
        per_block = ROW_BLOCK // C
        ids = list(range(r * per_block, (r + 1) * per_block))
        return ids if fwd else ids[::-1]

    def scale(r, _):
        for c in chunks(r):
            rows = chunk_rows(c)
            b = b_s[rows, :]
            e = jnp.exp(b - b[mid:mid + 1])
            x_s[rows, :] = (q_s[rows, :] * e).astype(_BF)
            y_s[rows, :] = (k_s[rows, :] / e).astype(_BF)

    def products(r, _):
        for c in chunks(r):
            rows = chunk_rows(c)
            for hd in range(B_HEADS):
                lanes = head_lanes(hd)
                yh = y_s[rows, lanes]
                a_s[hd, rows, :] = jnp.where(causal, _dot_tb(x_s[rows, lanes], yh), 0.0).astype(_BF)
                u_s[c, hd] = _dot_ta(v_s[rows, lanes], yh)

    def states(r, _):
        for hd in range(B_HEADS):
            lanes = head_lanes(hd)
            st = st_s[hd]
            for c in chunks(r):
                rho = b_s[c * C + mid:c * C + mid + 1, lanes]
                b_tot = b_s[c * C + last:c * C + last + 1, lanes]
                u = u_s[c, hd]
                u_s[c, hd] = st * jnp.exp(rho)
                st = st * jnp.exp(b_tot) + u * jnp.exp(b_tot - rho)
            st_s[hd] = st

    def outputs(r, _):
        for c in chunks(r):
            rows = chunk_rows(c)
            for hd in range(B_HEADS):
                lanes = head_lanes(hd)
                emit(rows, lanes, _dot_tb(x_s[rows, lanes], u_s[c, hd])
                     + _dot(a_s[hd, rows, :], v_s[rows, lanes]))

    return scale, products, states, outputs


def _scan_factored_ok(scan_scratch, fwd):
    b_s = scan_scratch[4]
    C = SCAN_CHUNK
    mid, last = _scan_reference_rows(fwd)
    worst = jnp.zeros((1, B_WIDTH), _F32)
    for c in range(b_s.shape[0] // C):
        b_mid = b_s[c * C + mid:c * C + mid + 1, :]
        b_tot = b_s[c * C + last:c * C + last + 1, :]
        worst = jnp.maximum(worst, jnp.maximum(-b_mid, b_mid - b_tot))
    return jnp.max(worst) < FACTORED_DECAY_LIMIT


def _begin_tile(st_s, st0_s):
    @pl.when(pl.program_id(1) == 0)
    def _():
        st_s[...] = jnp.zeros_like(st_s)
    st0_s[...] = st_s[...]


def _bwd_kernel(layer, x_ref, gpre_ref, w_ref, lbraw_ref, o_ref, q_ref, v_ref, *scratch):
    (k_s, g_s), rest, (st_s, st0_s) = scratch[:2], scratch[2:-2], scratch[-2:]
    scan_scratch = (q_ref.at[0], k_s, v_ref.at[0], g_s) + tuple(rest)
    _begin_tile(st_s, st0_s)
    lb = _lower_bound(lbraw_ref, layer) if layer > 0 else None

    def project(r, _):
        h = _rms(x_ref[0, _row_block(r), :], gpre_ref[...]).astype(_BF)
        return jnp.dot(h, w_ref[...], preferred_element_type=_F32)

    def gates(r, p):
        rows = _row_block(r)
        q_ref[0, rows, :] = _silu(p[:, 0:512]) * (B_DK ** -0.5)
        v_ref[0, rows, :] = p[:, 1024:1536].astype(_BF)
        return _hgrn_gates(p[:, 512:1024], rows, lb, scan_scratch)

    def cumulate(r, g):
        _hgrn_cumulative(g, _row_block(r), scan_scratch, fwd=False)

    def emit(rows, lanes, o):
        o_ref[0, rows, lanes] = o

    blocks = reversed(range(x_ref.shape[1] // ROW_BLOCK))
    _skewed(blocks, project, gates, cumulate, *_scan_factored_stages(scan_scratch, st_s, False, emit))

    @pl.when(jnp.logical_not(_scan_factored_ok(scan_scratch, False)))
    def _():
        st_s[...] = st0_s[...]
        _scan_exact_loop(scan_scratch, st_s, False, emit)


def _mix_kernel(layer, x_ref, obwd_ref, q_ref, v_ref, gpre_ref, w_ref, vlng_ref, vlnb_ref, wsp_ref,
                bsp_ref, lbraw_ref, onorm_ref, wout_ref, gpost_ref, out_ref, *scratch):
    (k_s, g_s), rest, (gate_s, st_s, st0_s, cat_s) = scratch[:2], scratch[2:-4], scratch[-4:]
    scan_scratch = (q_ref.at[0], k_s, v_ref.at[0], g_s) + tuple(rest)
    _begin_tile(st_s, st0_s)
    lb = _lower_bound(lbraw_ref, layer) if layer > 0 else None
    blocks = range(x_ref.shape[1] // ROW_BLOCK)

    def project(r, _):
        h = _rms(x_ref[0, _row_block(r), :], gpre_ref[...]).astype(_BF)
        return jnp.dot(h, w_ref[...], preferred_element_type=_F32)

    def elementwise(r, p):
        rows = _row_block(r)
        u = _gelu(p[:, 0:512])
        v = _gelu(p[:, 512:1024])
        vc = v - jnp.mean(v, axis=-1, keepdims=True)
        v = vc * lax.rsqrt(jnp.mean(vc * vc, axis=-1, keepdims=True) + EPS) * vlng_ref[...] + vlnb_ref[...]
        g = _hgrn_gates(p[:, 1024:1536], rows, lb, scan_scratch)
        gate_s[rows, :] = _silu(p[:, 1536:2048])
        return u, v.astype(_BF), g

    def dependent_matmuls(r, uvg):
        rows = _row_block(r)
        u, v, g = uvg
        chunks = [slice(n * A_CHUNK, (n + 1) * A_CHUNK) for n in range(ROW_BLOCK // A_CHUNK)]
        for gi in range(A_GROUPS):
            lanes = slice(gi * A_GROUP_DIM, (gi + 1) * A_GROUP_DIM)
            sv = jnp.dot(wsp_ref[gi], jnp.concatenate([v[c, lanes] for c in chunks], axis=1),
                         preferred_element_type=_F32)
            for n, c in enumerate(chunks):
                cat_s[rows.start + c.start:rows.start + c.stop, lanes] = (
                    u[c, lanes] * (sv[:, n * A_GROUP_DIM:(n + 1) * A_GROUP_DIM] + bsp_ref[gi])).astype(_BF)
        _hgrn_cumulative(g, rows, scan_scratch, fwd=True)

    def emit(rows, lanes, o):
        o_sum = o + obwd_ref[0, rows, lanes]
        cat_s[rows, A_WIDTH + lanes.start:A_WIDTH + lanes.stop] = (
            _rms(o_sum, onorm_ref[...]) * gate_s[rows, lanes]).astype(_BF)

    def out_proj(r, _):
        return jnp.dot(cat_s[_row_block(r), :], wout_ref[...], preferred_element_type=_F32)

    def residual(r, y):
        rows = _row_block(r)
        out_ref[0, rows, :] = x_ref[0, rows, :] + _rms(y, gpost_ref[...])

    _skewed(blocks, project, elementwise, dependent_matmuls,
            *_scan_factored_stages(scan_scratch, st_s, True, emit), out_proj, residual)

    @pl.when(jnp.logical_not(_scan_factored_ok(scan_scratch, True)))
    def _():
        st_s[...] = st0_s[...]
        _scan_exact_loop(scan_scratch, st_s, True, emit)
        _skewed(blocks, out_proj, residual)


def _kv_kernel(mem_ref, g_ref, w_ref, k_ref, v_ref):
    m = _rms(mem_ref[0], g_ref[...])
    kv = _dot(m, w_ref[...])
    k_ref[0] = kv[:, :D_MODEL].astype(_BF)
    v_ref[0] = kv[:, D_MODEL:].astype(_BF)


def _xattn_kernel(x_ref, k_ref, v_ref, gpre_ref, wq_ref, wo_ref, gpost_ref, out_ref, cat_s):
    head_lanes = lambda hd: slice(hd * X_HEAD_DIM, (hd + 1) * X_HEAD_DIM)

    def query(r, _):
        h = _rms(x_ref[0, _row_block(r), :], gpre_ref[...])
        return _dot(h, wq_ref[...]).astype(_BF)

    def scores(r, q):
        return [_dot_tb(q[:, head_lanes(hd)], k_ref[0, :, head_lanes(hd)]) for hd in range(X_HEADS)]

    def softmax(r, s_heads):
        p_heads = []
        for s in s_heads:
            s = s * (X_HEAD_DIM ** -0.5)
            e = jnp.exp(s - jnp.max(s, axis=-1, keepdims=True))
            p_heads.append((e / jnp.sum(e, axis=-1, keepdims=True)).astype(_BF))
        return p_heads

    def values(r, p_heads):
        for hd, p in enumerate(p_heads):
            cat_s[_row_block(r), head_lanes(hd)] = _dot(p, v_ref[0, :, head_lanes(hd)]).astype(_BF)

    def out_proj(r, _):
        return jnp.dot(cat_s[_row_block(r), :], wo_ref[...], preferred_element_type=_F32)

    def residual(r, y):
        rows = _row_block(r)
        out_ref[0, rows, :] = x_ref[0, rows, :] + _rms(y, gpost_ref[...])

    _skewed(range(x_ref.shape[1] // ROW_BLOCK), query, scores, softmax, values, out_proj, residual)


def _ffn_kernel(x_ref, gpre_ref, wgu_ref, wdown_ref, gpost_ref, out_ref):
    def gate_up(r, _):
        h = _rms(x_ref[0, _row_block(r, FFN_ROW_BLOCK), :], gpre_ref[...]).astype(_BF)
        return jnp.dot(h, wgu_ref[...], preferred_element_type=_F32)

    def activate(r, p):
        return (_silu(p[:, :D_FF]) * p[:, D_FF:]).astype(_BF)

    def down(r, act):
        return jnp.dot(act, wdown_ref[...], preferred_element_type=_F32)

    def residual(r, y):
        rows = _row_block(r, FFN_ROW_BLOCK)
        out_ref[0, rows, :] = x_ref[0, rows, :] + _rms(y, gpost_ref[...])

    _skewed(range(x_ref.shape[1] // FFN_ROW_BLOCK), gate_up, activate, down, residual)


def _params():
    return pltpu.CompilerParams(dimension_semantics=("arbitrary", "arbitrary"),
                                vmem_limit_bytes=VMEM_LIMIT_BYTES)


def _param_spec(stacked, index):
    return pl.BlockSpec((None,) + stacked.shape[1:], lambda b, t: (index,) + (0,) * (stacked.ndim - 1),
                        pipeline_mode=pl.Buffered(1))


def _tile_spec(width, rev_tiles=None, tile=SEQ_TILE):
    if rev_tiles is None:
        return pl.BlockSpec((1, tile, width), lambda b, t: (b, t, 0))
    return pl.BlockSpec((1, tile, width), lambda b, t: (b, rev_tiles - 1 - t, 0))


def _scan_scratch():
    tile = lambda dt: pltpu.VMEM((SEQ_TILE, B_WIDTH), dt)
    chunk = lambda: pltpu.VMEM((SCAN_CHUNK, B_WIDTH), _F32)
    return [tile(_F32), tile(_F32), tile(_F32), chunk(), chunk(), tile(_BF), tile(_BF),
            pltpu.VMEM((B_HEADS, SEQ_TILE, SCAN_CHUNK), _BF),
            pltpu.VMEM((SEQ_TILE // SCAN_CHUNK, B_HEADS, B_DK, B_DK), _F32)]


def _state_scratch():
    return pltpu.VMEM((B_HEADS, B_DK, B_DK), _F32)


def _bwd_call(x, layer, gpre, w_bwd, lbraw):
    bn, ln, _ = x.shape
    nt = ln // SEQ_TILE
    return pl.pallas_call(
        functools.partial(_bwd_kernel, layer),
        grid=(bn, nt),
        in_specs=[_tile_spec(D_MODEL, nt), _param_spec(gpre, layer), _param_spec(w_bwd, layer),
                  _param_spec(lbraw, 1)],
        out_specs=[_tile_spec(B_WIDTH, nt)] * 3,
        out_shape=[jax.ShapeDtypeStruct((bn, ln, B_WIDTH), _F32),
                   jax.ShapeDtypeStruct((bn, ln, B_WIDTH), _F32),
                   jax.ShapeDtypeStruct((bn, ln, B_WIDTH), _BF)],
        scratch_shapes=_scan_scratch() + [_state_scratch(), _state_scratch()],
        compiler_params=_params(),
        name=f"hgrn_bwd_l{layer}",
    )(x, gpre, w_bwd, lbraw)


def _mix_call(x, obwd, q, v, layer, gpre, w_mix, vlng, vlnb, wsp, bsp, lbraw, onorm, wout, gpost):
    bn, ln, _ = x.shape
    nt = ln // SEQ_TILE
    consts = (gpre, w_mix, vlng, vlnb, wsp, bsp, lbraw, onorm, wout, gpost)
    return pl.pallas_call(
        functools.partial(_mix_kernel, layer),
        grid=(bn, nt),
        in_specs=[_tile_spec(D_MODEL)] + [_tile_spec(B_WIDTH)] * 3
                 + [_param_spec(c, 0 if c is lbraw else layer) for c in consts],
        out_specs=_tile_spec(D_MODEL),
        out_shape=jax.ShapeDtypeStruct(x.shape, _F32),
        scratch_shapes=_scan_scratch() + [pltpu.VMEM((SEQ_TILE, B_WIDTH), _F32), _state_scratch(),
                                          _state_scratch(),
                                          pltpu.VMEM((SEQ_TILE, D_MODEL), _BF)],
        compiler_params=_params(),
        name=f"token_mix_l{layer}",
    )(x, obwd, q, v, *consts)


def _kv_call(mem, layer, g, w):
    bn = mem.shape[0]
    spec = pl.BlockSpec((1, MEM_TOKENS, D_MODEL), lambda b, t: (b, 0, 0))
    return pl.pallas_call(
        _kv_kernel,
        grid=(bn, 1),
        in_specs=[spec, _param_spec(g, layer), _param_spec(w, layer)],
        out_specs=[spec, spec],
        out_shape=[jax.ShapeDtypeStruct(mem.shape, _BF)] * 2,
        compiler_params=_params(),
        name=f"mem_kv_l{layer}",
    )(mem, g, w)


def _xattn_call(x, k, v, layer, gpre, wq, wo, gpost):
    bn, ln, _ = x.shape
    nt = ln // XATTN_TILE
    mem_spec = pl.BlockSpec((1, MEM_TOKENS, D_MODEL), lambda b, t: (b, 0, 0))
    consts = (gpre, wq, wo, gpost)
    x_spec = _tile_spec(D_MODEL, tile=XATTN_TILE)
    return pl.pallas_call(
        _xattn_kernel,
        grid=(bn, nt),
        in_specs=[x_spec, mem_spec, mem_spec] + [_param_spec(c, layer) for c in consts],
        out_specs=x_spec,
        out_shape=jax.ShapeDtypeStruct(x.shape, _F32),
        scratch_shapes=[pltpu.VMEM((XATTN_TILE, D_MODEL), _BF)],
        compiler_params=_params(),
        name=f"mem_xattn_l{layer}",
    )(x, k, v, *consts)


def _ffn_call(x, layer, gpre, wgu, wdown, gpost):
    bn, ln, _ = x.shape
    nt = ln // SEQ_TILE
    consts = (gpre, wgu, wdown, gpost)
    return pl.pallas_call(
        _ffn_kernel,
        grid=(bn, nt),
        in_specs=[_tile_spec(D_MODEL)] + [_param_spec(c, layer) for c in consts],
        out_specs=_tile_spec(D_MODEL),
        out_shape=jax.ShapeDtypeStruct(x.shape, _F32),
        compiler_params=_params(),
        name=f"swiglu_ffn_l{layer}",
    )(x, *consts)


def _kernel_params(p):
    w_in = p['w_in']
    cols = lambda *ranges: jnp.concatenate([w_in[:, :, a:b] for a, b in ranges], axis=2).astype(_BF)
    row = lambda a: a[:, None, :]
    bf = lambda a: a.astype(_BF)
    depth = w_in.shape[0]
    return dict(
        w_mix=cols((0, 1024), (1536, 2048), (3072, 3584)),
        w_bwd=cols((1024, 1536), (2048, 3072)),
        pre_mix_g=row(p['pre_mix_g']), v_ln_g=row(p['v_ln_g']), v_ln_b=row(p['v_ln_b']),
        w_spatial=bf(p['w_spatial']),
        b_spatial=jnp.broadcast_to(p['b_spatial'][:, :, :, None], (depth, A_GROUPS, A_CHUNK, A_GROUP_DIM)),
        lb_raw=p['lb_raw'],
        onorm_g=row(p['onorm_g']), w_out=bf(p['w_out']), post_mix_g=row(p['post_mix_g']),
        pre_x_g=row(p['pre_x_g']), mem_norm_g=row(p['mem_norm_g']),
        w_xq=bf(p['w_xq']), w_xkv=bf(p['w_xkv']), w_xo=bf(p['w_xo']),
        post_x_g=row(p['post_x_g']), pre_ffn_g=row(p['pre_ffn_g']),
        w_gu=bf(p['w_gu']), w_down=bf(p['w_down']), post_ffn_g=row(p['post_ffn_g']),
    )


def _trunk(x, mem, w, depth):
    assert x.shape[1] % SEQ_TILE == 0 and x.shape[1] % XATTN_TILE == 0 and x.shape[2] == D_MODEL
    assert SEQ_TILE % ROW_BLOCK == 0 and ROW_BLOCK % SCAN_CHUNK == 0 and ROW_BLOCK % A_CHUNK == 0
    assert SEQ_TILE % FFN_ROW_BLOCK == 0
    for l in range(depth):
        obwd, q, v = _bwd_call(x, l, w['pre_mix_g'], w['w_bwd'], w['lb_raw'])
        x = _mix_call(x, obwd, q, v, l, w['pre_mix_g'], w['w_mix'], w['v_ln_g'], w['v_ln_b'], w['w_spatial'],
                      w['b_spatial'], w['lb_raw'], w['onorm_g'], w['w_out'], w['post_mix_g'])
        k, v = _kv_call(mem, l, w['mem_norm_g'], w['w_xkv'])
        x = _xattn_call(x, k, v, l, w['pre_x_g'], w['w_xq'], w['w_xo'], w['post_x_g'])
        x = _ffn_call(x, l, w['pre_ffn_g'], w['w_gu'], w['w_down'], w['post_ffn_g'])
    return x


def kernel(x_prompt, x_sample, mem_prompt, mem_sample, pre_mix_g, w_in, v_ln_g, v_ln_b, w_spatial, b_spatial, lb_raw, onorm_g, w_out, post_mix_g, pre_x_g, mem_norm_g, w_xq, w_xkv, w_xo, post_x_g, pre_ffn_g, w_gu, w_down, post_ffn_g):
    p = dict(pre_mix_g=pre_mix_g, w_in=w_in, v_ln_g=v_ln_g, v_ln_b=v_ln_b,
             w_spatial=w_spatial, b_spatial=b_spatial, lb_raw=lb_raw, onorm_g=onorm_g,
             w_out=w_out, post_mix_g=post_mix_g, pre_x_g=pre_x_g, mem_norm_g=mem_norm_g,
             w_xq=w_xq, w_xkv=w_xkv, w_xo=w_xo, post_x_g=post_x_g, pre_ffn_g=pre_ffn_g,
             w_gu=w_gu, w_down=w_down, post_ffn_g=post_ffn_g)
    w = _kernel_params(p)
    depth = w_in.shape[0]
    return (_trunk(x_prompt, mem_prompt, w, depth), _trunk(x_sample, mem_sample, w, depth))
```

```python
import functools

import jax
import jax.numpy as jnp
from jax import lax
from jax.experimental import pallas as pl
from jax.experimental.pallas import tpu as pltpu

D_MODEL = 1024
A_WIDTH = 512
A_GROUPS = 4
A_GROUP_DIM = 128
A_CHUNK = 128
B_WIDTH = 512
B_HEADS = 4
B_DK = 128
MEM_TOKENS = 256
X_HEADS = 4
X_HEAD_DIM = 256
D_FF = 2816
EPS = 1e-6

SEQ_TILE = 1024
XATTN_TILE = 2048
ROW_BLOCK = 512
FFN_ROW_BLOCK = 256
SCAN_CHUNK = 64
SUB_BLOCK = 16
FACTORED_DECAY_LIMIT = 55.0
VMEM_LIMIT_BYTES = 60 * 1024 * 1024

_BF = jnp.bfloat16
_F32 = jnp.float32


def _dot(a, b):
    return jnp.dot(a.astype(_BF), b.astype(_BF), preferred_element_type=_F32)


def _dot_tb(a, b):
    return lax.dot_general(a.astype(_BF), b.astype(_BF), (((1,), (1,)), ((), ())),
                           preferred_element_type=_F32)


def _dot_ta(a, b):
    return lax.dot_general(a.astype(_BF), b.astype(_BF), (((0,), (0,)), ((), ())),
                           preferred_element_type=_F32)


def _rms(x, g):
    return x * lax.rsqrt(jnp.mean(x * x, axis=-1, keepdims=True) + EPS) * g


def _gelu(x):
    return 0.5 * x * (1.0 + lax.erf(x * (2.0 ** -0.5)))


def _silu(x):
    return x * (0.5 * jnp.tanh(0.5 * x) + 0.5)


def _lower_bound(lbraw_ref, layer):
    depth = lbraw_ref.shape[0]
    rows = [lbraw_ref[j:j + 1, :] for j in range(depth)]
    m = functools.reduce(jnp.maximum, rows)
    e = [jnp.exp(r - m) for r in rows]
    denom = functools.reduce(lambda a, b: a + b, e)
    return functools.reduce(lambda a, b: a + b, e[1:layer + 1]) / denom


def _skewed(order, *stages):
    order = list(order)
    vals = {}
    for i in range(len(order) + len(stages) - 1):
        for k, stage in enumerate(stages):
            if 0 <= i - k < len(order):
                r = order[i - k]
                vals[(k, r)] = stage(r, vals.pop((k - 1, r), None))


def _row_block(r, size=ROW_BLOCK):
    return slice(r * size, (r + 1) * size)


def _split3(x):
    hi = x.astype(_BF)
    r1 = x - hi.astype(_F32)
    mid = r1.astype(_BF)
    lo = (r1 - mid.astype(_F32)).astype(_BF)
    return hi, mid, lo


def _cumulative(g, fwd):
    n = g.shape[0]
    row = lax.broadcasted_iota(jnp.int32, (n, n), 0)
    col = lax.broadcasted_iota(jnp.int32, (n, n), 1)
    tri = jnp.where(col <= row if fwd else col >= row, 1.0, 0.0).astype(_BF)
    g_hi, g_mid, g_lo = _split3(g)
    return (jnp.dot(tri, g_hi, preferred_element_type=_F32)
            + jnp.dot(tri, g_mid, preferred_element_type=_F32)
            + jnp.dot(tri, g_lo, preferred_element_type=_F32))


def _hgrn_gates(f, rows, lb, scan_scratch):
    k_s, g_s = scan_scratch[1], scan_scratch[3]
    if lb is None:
        e = jnp.exp(-jnp.abs(f))
        g = jnp.minimum(f, 0.0) - jnp.log(1.0 + e)
        k_s[rows, :] = jnp.where(f >= 0.0, e, 1.0) / (1.0 + e)
    else:
        forget = lb + (1.0 - lb) * (0.5 * jnp.tanh(0.5 * f) + 0.5)
        g = jnp.log(forget)
        k_s[rows, :] = 1.0 - forget
    g_s[rows, :] = g
    return g


def _hgrn_cumulative(g, rows, scan_scratch, fwd):
    b_s = scan_scratch[4]
    for c in range(ROW_BLOCK // SCAN_CHUNK):
        lo = c * SCAN_CHUNK
        b_s[rows.start + lo:rows.start + lo + SCAN_CHUNK, :] = _cumulative(g[lo:lo + SCAN_CHUNK], fwd)


def _scan_exact_loop(scan_scratch, st_s, fwd, emit):
    q_s, k_s, v_s, g_s, b_s, kc_s, bc_s = scan_scratch[:7]
    C = SCAN_CHUNK
    n_chunks = q_s.shape[0] // C
    n_sub = C // SUB_BLOCK
    sub_row = lax.broadcasted_iota(jnp.int32, (SUB_BLOCK, C), 0)
    sub_col = lax.broadcasted_iota(jnp.int32, (SUB_BLOCK, C), 1)
    last = C - 1 if fwd else 0

    def chunk_body(c, carry):
        idx = c if fwd else n_chunks - 1 - c
        base = pl.multiple_of(idx * C, C)
        rows = pl.ds(base, C)
        bc_s[...] = b_s[rows, :]
        kc_s[...] = k_s[rows, :]
        for hd in range(B_HEADS):
            lanes = slice(hd * B_DK, (hd + 1) * B_DK)
            q = q_s[rows, lanes]
            k = kc_s[:, lanes]
            v = v_s[rows, lanes]
            b = bc_s[:, lanes]
            st = st_s[hd]
            b_tot = b[last:last + 1]
            o = _dot_tb(q * jnp.exp(b), st)
            kd = k * jnp.exp(b_tot - b)
            st_s[hd] = st * jnp.exp(b_tot) + _dot_ta(v, kd)

            a_rows = []
            for i in range(n_sub):
                lo, hi = i * SUB_BLOCK, (i + 1) * SUB_BLOCK
                qi, bi = q[lo:hi], b[lo:hi]
                if fwd and i > 0:
                    r = b[lo - 1:lo]
                    ke = k[:lo] * jnp.exp(r - b[:lo])
                    ke = jnp.concatenate([ke, jnp.zeros((C - lo, B_DK), _F32)], axis=0)
                    a_i = _dot_tb(qi * jnp.exp(bi - r), ke)
                elif (not fwd) and i < n_sub - 1:
                    r = b[hi:hi + 1]
                    ke = k[hi:] * jnp.exp(r - b[hi:])
                    ke = jnp.concatenate([jnp.zeros((hi, B_DK), _F32), ke], axis=0)
                    a_i = _dot_tb(qi * jnp.exp(bi - r), ke)
                else:
                    a_i = jnp.zeros((SUB_BLOCK, C), _F32)
                for s in range(SUB_BLOCK):
                    j = lo + s
                    k_row = kc_s[j:j + 1, lanes]
                    b_row = bc_s[j:j + 1, lanes]
                    p = qi * k_row * jnp.exp(jnp.minimum(bi - b_row, 0.0))
                    a_i = jnp.where(sub_col == j, jnp.sum(p, axis=-1, keepdims=True), a_i)
                local = sub_col - lo
                if fwd:
                    keep = (sub_col < lo) | ((sub_col < hi) & (sub_row >= local))
                else:
                    keep = (sub_col >= hi) | ((sub_col >= lo) & (sub_row <= local))
                a_rows.append(jnp.where(keep, a_i, 0.0))
            a = jnp.concatenate(a_rows, axis=0)
            emit(rows, lanes, o + _dot(a, v))
        return carry

    lax.fori_loop(0, n_chunks, chunk_body, 0)


def _scan_reference_rows(fwd):
    return (SCAN_CHUNK // 2 - 1, SCAN_CHUNK - 1) if fwd else (SCAN_CHUNK // 2, 0)


def _scan_factored_stages(scan_scratch, st_s, fwd, emit):
    q_s, k_s, v_s, g_s, b_s, kc_s, bc_s, x_s, y_s, a_s, u_s = scan_scratch
    C = SCAN_CHUNK
    mid, last = _scan_reference_rows(fwd)
    row = lax.broadcasted_iota(jnp.int32, (C, C), 0)
    col = lax.broadcasted_iota(jnp.int32, (C, C), 1)
    causal = (col <= row) if fwd else (col >= row)
    chunk_rows = lambda c: slice(c * C, (c + 1) * C)
    head_lanes = lambda hd: slice(hd * B_DK, (hd + 1) * B_DK)

    def chunks(r):
        per_block = ROW_BLOCK // C
        ids = list(range(r * per_block, (r + 1) * per_block))
        return ids if fwd else ids[::-1]

    def scale(r, _):
        for c in chunks(r):
            rows = chunk_rows(c)
            b = b_s[rows, :]
            e = jnp.exp(b - b[mid:mid + 1])
            x_s[rows, :] = (q_s[rows, :] * e).astype(_BF)
            y_s[rows, :] = (k_s[rows, :] / e).astype(_BF)

    def products(r, _):
        for c in chunks(r):
            rows = chunk_rows(c)
            for hd in range(B_HEADS):
                lanes = head_lanes(hd)
                yh = y_s[rows, lanes]
                a_s[hd, rows, :] = jnp.where(causal, _dot_tb(x_s[rows, lanes], yh), 0.0).astype(_BF)
                u_s[c, hd] = _dot_ta(v_s[rows, lanes], yh)

    def states(r, _):
        for hd in range(B_HEADS):
            lanes = head_lanes(hd)
            st = st_s[hd]
            for c in chunks(r):
                rho = b_s[c * C + mid:c * C + mid + 1, lanes]
                b_tot = b_s[c * C + last:c * C + last + 1, lanes]
                u = u_s[c, hd]
                u_s[c, hd] = st * jnp.exp(rho)
                st = st * jnp.exp(b_tot) + u * jnp.exp(b_tot - rho)
            st_s[hd] = st

    def outputs(r, _):
        for c in chunks(r):
            rows = chunk_rows(c)
            for hd in range(B_HEADS):
                lanes = head_lanes(hd)
                emit(rows, lanes, _dot_tb(x_s[rows, lanes], u_s[c, hd])
                     + _dot(a_s[hd, rows, :], v_s[rows, lanes]))

    return scale, products, states, outputs


def _scan_factored_ok(scan_scratch, fwd):
    b_s = scan_scratch[4]
    C = SCAN_CHUNK
    mid, last = _scan_reference_rows(fwd)
    worst = jnp.zeros((1, B_WIDTH), _F32)
    for c in range(b_s.shape[0] // C):
        b_mid = b_s[c * C + mid:c * C + mid + 1, :]
        b_tot = b_s[c * C + last:c * C + last + 1, :]
        worst = jnp.maximum(worst, jnp.maximum(-b_mid, b_mid - b_tot))
    return jnp.max(worst) < FACTORED_DECAY_LIMIT


def _begin_tile(st_s, st0_s):
    @pl.when(pl.program_id(1) == 0)
    def _():
        st_s[...] = jnp.zeros_like(st_s)
    st0_s[...] = st_s[...]


def _bwd_kernel(layer, x_ref, gpre_ref, w_ref, lbraw_ref, o_ref, q_ref, v_ref, *scratch):
    (k_s, g_s), rest, (st_s, st0_s) = scratch[:2], scratch[2:-2], scratch[-2:]
    scan_scratch = (q_ref.at[0], k_s, v_ref.at[0], g_s) + tuple(rest)
    _begin_tile(st_s, st0_s)
    lb = _lower_bound(lbraw_ref, layer) if layer > 0 else None

    def project(r, _):
        h = _rms(x_ref[0, _row_block(r), :], gpre_ref[...]).astype(_BF)
        return jnp.dot(h, w_ref[...], preferred_element_type=_F32)

    def gates(r, p):
        rows = _row_block(r)
        q_ref[0, rows, :] = _silu(p[:, :B_WIDTH]) * (B_DK ** -0.5)
        v_ref[0, rows, :] = p[:, 2 * B_WIDTH:].astype(_BF)
        return _hgrn_gates(p[:, B_WIDTH:2 * B_WIDTH], rows, lb, scan_scratch)

    def cumulate(r, g):
        _hgrn_cumulative(g, _row_block(r), scan_scratch, fwd=False)

    def emit(rows, lanes, o):
        o_ref[0, rows, lanes] = o

    blocks = reversed(range(x_ref.shape[1] // ROW_BLOCK))
    _skewed(blocks, project, gates, cumulate, *_scan_factored_stages(scan_scratch, st_s, False, emit))

    @pl.when(jnp.logical_not(_scan_factored_ok(scan_scratch, False)))
    def _():
        st_s[...] = st0_s[...]
        _scan_exact_loop(scan_scratch, st_s, False, emit)


def _mix_kernel(layer, x_ref, obwd_ref, q_ref, v_ref, gpre_ref, w_ref, vlng_ref, vlnb_ref, wsp_ref,
                bsp_ref, lbraw_ref, onorm_ref, wout_ref, gpost_ref, out_ref, *scratch):
    (k_s, g_s), rest, (gate_s, st_s, st0_s, cat_s) = scratch[:2], scratch[2:-4], scratch[-4:]
    scan_scratch = (q_ref.at[0], k_s, v_ref.at[0], g_s) + tuple(rest)
    _begin_tile(st_s, st0_s)
    lb = _lower_bound(lbraw_ref, layer) if layer > 0 else None
    blocks = range(x_ref.shape[1] // ROW_BLOCK)

    def project(r, _):
        h = _rms(x_ref[0, _row_block(r), :], gpre_ref[...]).astype(_BF)
        return jnp.dot(h, w_ref[...], preferred_element_type=_F32)

    def elementwise(r, p):
        rows = _row_block(r)
        u = _gelu(p[:, :A_WIDTH])
        v = _gelu(p[:, A_WIDTH:2 * A_WIDTH])
        vc = v - jnp.mean(v, axis=-1, keepdims=True)
        v = vc * lax.rsqrt(jnp.mean(vc * vc, axis=-1, keepdims=True) + EPS) * vlng_ref[...] + vlnb_ref[...]
        g = _hgrn_gates(p[:, 2 * A_WIDTH:2 * A_WIDTH + B_WIDTH], rows, lb, scan_scratch)
        gate_s[rows, :] = _silu(p[:, 2 * A_WIDTH + B_WIDTH:])
        return u, v.astype(_BF), g

    def dependent_matmuls(r, uvg):
        rows = _row_block(r)
        u, v, g = uvg
        chunks = [slice(n * A_CHUNK, (n + 1) * A_CHUNK) for n in range(ROW_BLOCK // A_CHUNK)]
        for gi in range(A_GROUPS):
            lanes = slice(gi * A_GROUP_DIM, (gi + 1) * A_GROUP_DIM)
            sv = jnp.dot(wsp_ref[gi], jnp.concatenate([v[c, lanes] for c in chunks], axis=1),
                         preferred_element_type=_F32)
            for n, c in enumerate(chunks):
                cat_s[rows.start + c.start:rows.start + c.stop, lanes] = (
                    u[c, lanes] * (sv[:, n * A_GROUP_DIM:(n + 1) * A_GROUP_DIM] + bsp_ref[gi])).astype(_BF)
        _hgrn_cumulative(g, rows, scan_scratch, fwd=True)

    def emit(rows, lanes, o):
        o_sum = o + obwd_ref[0, rows, lanes]
        cat_s[rows, A_WIDTH + lanes.start:A_WIDTH + lanes.stop] = (
            _rms(o_sum, onorm_ref[...]) * gate_s[rows, lanes]).astype(_BF)

    def out_proj(r, _):
        return jnp.dot(cat_s[_row_block(r), :], wout_ref[...], preferred_element_type=_F32)

    def residual(r, y):
        rows = _row_block(r)
        out_ref[0, rows, :] = x_ref[0, rows, :] + _rms(y, gpost_ref[...])

    _skewed(blocks, project, elementwise, dependent_matmuls,
            *_scan_factored_stages(scan_scratch, st_s, True, emit), out_proj, residual)

    @pl.when(jnp.logical_not(_scan_factored_ok(scan_scratch, True)))
    def _():
        st_s[...] = st0_s[...]
        _scan_exact_loop(scan_scratch, st_s, True, emit)
        _skewed(blocks, out_proj, residual)


def _kv_kernel(mem_ref, g_ref, w_ref, k_ref, v_ref):
    m = _rms(mem_ref[0], g_ref[...])
    kv = _dot(m, w_ref[...])
    k_ref[0] = kv[:, :D_MODEL].astype(_BF)
    v_ref[0] = kv[:, D_MODEL:].astype(_BF)


def _xattn_kernel(x_ref, k_ref, v_ref, gpre_ref, wq_ref, wo_ref, gpost_ref, out_ref, cat_s):
    head_lanes = lambda hd: slice(hd * X_HEAD_DIM, (hd + 1) * X_HEAD_DIM)

    def query(r, _):
        h = _rms(x_ref[0, _row_block(r), :], gpre_ref[...])
        return _dot(h, wq_ref[...]).astype(_BF)

    def scores(r, q):
        return [_dot_tb(q[:, head_lanes(hd)], k_ref[0, :, head_lanes(hd)]) for hd in range(X_HEADS)]

    def softmax(r, s_heads):
        p_heads = []
        for s in s_heads:
            s = s * (X_HEAD_DIM ** -0.5)
            e = jnp.exp(s - jnp.max(s, axis=-1, keepdims=True))
            p_heads.append((e / jnp.sum(e, axis=-1, keepdims=True)).astype(_BF))
        return p_heads

    def values(r, p_heads):
        for hd, p in enumerate(p_heads):
            cat_s[_row_block(r), head_lanes(hd)] = _dot(p, v_ref[0, :, head_lanes(hd)]).astype(_BF)

    def out_proj(r, _):
        return jnp.dot(cat_s[_row_block(r), :], wo_ref[...], preferred_element_type=_F32)

    def residual(r, y):
        rows = _row_block(r)
        out_ref[0, rows, :] = x_ref[0, rows, :] + _rms(y, gpost_ref[...])

    _skewed(range(x_ref.shape[1] // ROW_BLOCK), query, scores, softmax, values, out_proj, residual)


def _ffn_kernel(x_ref, gpre_ref, wgu_ref, wdown_ref, gpost_ref, out_ref):
    def gate_up(r, _):
        h = _rms(x_ref[0, _row_block(r, FFN_ROW_BLOCK), :], gpre_ref[...]).astype(_BF)
        return jnp.dot(h, wgu_ref[...], preferred_element_type=_F32)

    def activate(r, p):
        return (_silu(p[:, :D_FF]) * p[:, D_FF:]).astype(_BF)

    def down(r, act):
        return jnp.dot(act, wdown_ref[...], preferred_element_type=_F32)

    def residual(r, y):
        rows = _row_block(r, FFN_ROW_BLOCK)
        out_ref[0, rows, :] = x_ref[0, rows, :] + _rms(y, gpost_ref[...])

    _skewed(range(x_ref.shape[1] // FFN_ROW_BLOCK), gate_up, activate, down, residual)


def _params():
    return pltpu.CompilerParams(dimension_semantics=("arbitrary", "arbitrary"),
                                vmem_limit_bytes=VMEM_LIMIT_BYTES)


def _param_spec(stacked, index):
    return pl.BlockSpec((None,) + stacked.shape[1:], lambda b, t: (index,) + (0,) * (stacked.ndim - 1),
                        pipeline_mode=pl.Buffered(1))


def _tile_spec(width, rev_tiles=None, tile=SEQ_TILE):
    if rev_tiles is None:
        return pl.BlockSpec((1, tile, width), lambda b, t: (b, t, 0))
    return pl.BlockSpec((1, tile, width), lambda b, t: (b, rev_tiles - 1 - t, 0))


def _scan_scratch():
    tile = lambda dt: pltpu.VMEM((SEQ_TILE, B_WIDTH), dt)
    chunk = lambda: pltpu.VMEM((SCAN_CHUNK, B_WIDTH), _F32)
    return [tile(_F32), tile(_F32), tile(_F32), chunk(), chunk(), tile(_BF), tile(_BF),
            pltpu.VMEM((B_HEADS, SEQ_TILE, SCAN_CHUNK), _BF),
            pltpu.VMEM((SEQ_TILE // SCAN_CHUNK, B_HEADS, B_DK, B_DK), _F32)]


def _state_scratch():
    return pltpu.VMEM((B_HEADS, B_DK, B_DK), _F32)


def _bwd_call(x, layer, gpre, w_bwd, lbraw):
    bn, ln, _ = x.shape
    nt = ln // SEQ_TILE
    return pl.pallas_call(
        functools.partial(_bwd_kernel, layer),
        grid=(bn, nt),
        in_specs=[_tile_spec(D_MODEL, nt), _param_spec(gpre, layer), _param_spec(w_bwd, layer),
                  _param_spec(lbraw, 1)],
        out_specs=[_tile_spec(B_WIDTH, nt)] * 3,
        out_shape=[jax.ShapeDtypeStruct((bn, ln, B_WIDTH), _F32),
                   jax.ShapeDtypeStruct((bn, ln, B_WIDTH), _F32),
                   jax.ShapeDtypeStruct((bn, ln, B_WIDTH), _BF)],
        scratch_shapes=_scan_scratch() + [_state_scratch(), _state_scratch()],
        compiler_params=_params(),
        name=f"hgrn_bwd_l{layer}",
    )(x, gpre, w_bwd, lbraw)


def _mix_call(x, obwd, q, v, layer, gpre, w_mix, vlng, vlnb, wsp, bsp, lbraw, onorm, wout, gpost):
    bn, ln, _ = x.shape
    nt = ln // SEQ_TILE
    consts = (gpre, w_mix, vlng, vlnb, wsp, bsp, lbraw, onorm, wout, gpost)
    return pl.pallas_call(
        functools.partial(_mix_kernel, layer),
        grid=(bn, nt),
        in_specs=[_tile_spec(D_MODEL)] + [_tile_spec(B_WIDTH)] * 3
                 + [_param_spec(c, 0 if c is lbraw else layer) for c in consts],
        out_specs=_tile_spec(D_MODEL),
        out_shape=jax.ShapeDtypeStruct(x.shape, _F32),
        scratch_shapes=_scan_scratch() + [pltpu.VMEM((SEQ_TILE, B_WIDTH), _F32), _state_scratch(),
                                          _state_scratch(),
                                          pltpu.VMEM((SEQ_TILE, D_MODEL), _BF)],
        compiler_params=_params(),
        name=f"token_mix_l{layer}",
    )(x, obwd, q, v, *consts)


def _kv_call(mem, layer, g, w):
    bn = mem.shape[0]
    spec = pl.BlockSpec((1, MEM_TOKENS, D_MODEL), lambda b, t: (b, 0, 0))
    return pl.pallas_call(
        _kv_kernel,
        grid=(bn, 1),
        in_specs=[spec, _param_spec(g, layer), _param_spec(w, layer)],
        out_specs=[spec, spec],
        out_shape=[jax.ShapeDtypeStruct(mem.shape, _BF)] * 2,
        compiler_params=_params(),
        name=f"mem_kv_l{layer}",
    )(mem, g, w)


def _xattn_call(x, k, v, layer, gpre, wq, wo, gpost):
    bn, ln, _ = x.shape
    nt = ln // XATTN_TILE
    mem_spec = pl.BlockSpec((1, MEM_TOKENS, D_MODEL), lambda b, t: (b, 0, 0))
    consts = (gpre, wq, wo, gpost)
    x_spec = _tile_spec(D_MODEL, tile=XATTN_TILE)
    return pl.pallas_call(
        _xattn_kernel,
        grid=(bn, nt),
        in_specs=[x_spec, mem_spec, mem_spec] + [_param_spec(c, layer) for c in consts],
        out_specs=x_spec,
        out_shape=jax.ShapeDtypeStruct(x.shape, _F32),
        scratch_shapes=[pltpu.VMEM((XATTN_TILE, D_MODEL), _BF)],
        compiler_params=_params(),
        name=f"mem_xattn_l{layer}",
    )(x, k, v, *consts)


def _ffn_call(x, layer, gpre, wgu, wdown, gpost):
    bn, ln, _ = x.shape
    nt = ln // SEQ_TILE
    consts = (gpre, wgu, wdown, gpost)
    return pl.pallas_call(
        _ffn_kernel,
        grid=(bn, nt),
        in_specs=[_tile_spec(D_MODEL)] + [_param_spec(c, layer) for c in consts],
        out_specs=_tile_spec(D_MODEL),
        out_shape=jax.ShapeDtypeStruct(x.shape, _F32),
        compiler_params=_params(),
        name=f"swiglu_ffn_l{layer}",
    )(x, *consts)


def _kernel_params(p):
    w_in = p['w_in']
    q0, ff0, fb0, g0 = 2 * A_WIDTH, 2 * A_WIDTH + B_WIDTH, 2 * A_WIDTH + 2 * B_WIDTH, 2 * A_WIDTH + 4 * B_WIDTH
    cols = lambda *ranges: jnp.concatenate([w_in[:, :, a:b] for a, b in ranges], axis=2).astype(_BF)
    row = lambda a: a[:, None, :]
    bf = lambda a: a.astype(_BF)
    depth = w_in.shape[0]
    return dict(
        w_mix=cols((0, q0), (ff0, fb0), (g0, g0 + B_WIDTH)),
        w_bwd=cols((q0, ff0), (fb0, g0)),
        pre_mix_g=row(p['pre_mix_g']), v_ln_g=row(p['v_ln_g']), v_ln_b=row(p['v_ln_b']),
        w_spatial=bf(p['w_spatial']),
        b_spatial=jnp.broadcast_to(p['b_spatial'][:, :, :, None], (depth, A_GROUPS, A_CHUNK, A_GROUP_DIM)),
        lb_raw=p['lb_raw'],
        onorm_g=row(p['onorm_g']), w_out=bf(p['w_out']), post_mix_g=row(p['post_mix_g']),
        pre_x_g=row(p['pre_x_g']), mem_norm_g=row(p['mem_norm_g']),
        w_xq=bf(p['w_xq']), w_xkv=bf(p['w_xkv']), w_xo=bf(p['w_xo']),
        post_x_g=row(p['post_x_g']), pre_ffn_g=row(p['pre_ffn_g']),
        w_gu=bf(p['w_gu']), w_down=bf(p['w_down']), post_ffn_g=row(p['post_ffn_g']),
    )


def _trunk(x, mem, w, depth):
    assert x.shape[1] % SEQ_TILE == 0 and x.shape[1] % XATTN_TILE == 0 and x.shape[2] == D_MODEL
    assert SEQ_TILE % ROW_BLOCK == 0 and ROW_BLOCK % SCAN_CHUNK == 0 and ROW_BLOCK % A_CHUNK == 0
    assert SEQ_TILE % FFN_ROW_BLOCK == 0
    for l in range(depth):
        obwd, q, v = _bwd_call(x, l, w['pre_mix_g'], w['w_bwd'], w['lb_raw'])
        x = _mix_call(x, obwd, q, v, l, w['pre_mix_g'], w['w_mix'], w['v_ln_g'], w['v_ln_b'], w['w_spatial'],
                      w['b_spatial'], w['lb_raw'], w['onorm_g'], w['w_out'], w['post_mix_g'])
        k, v = _kv_call(mem, l, w['mem_norm_g'], w['w_xkv'])
        x = _xattn_call(x, k, v, l, w['pre_x_g'], w['w_xq'], w['w_xo'], w['post_x_g'])
        x = _ffn_call(x, l, w['pre_ffn_g'], w['w_gu'], w['w_down'], w['post_ffn_g'])
    return x


def kernel(x_prompt, x_sample, mem_prompt, mem_sample, pre_mix_g, w_in, v_ln_g, v_ln_b, w_spatial, b_spatial, lb_raw, onorm_g, w_out, post_mix_g, pre_x_g, mem_norm_g, w_xq, w_xkv, w_xo, post_x_g, pre_ffn_g, w_gu, w_down, post_ffn_g):
    p = dict(pre_mix_g=pre_mix_g, w_in=w_in, v_ln_g=v_ln_g, v_ln_b=v_ln_b,
             w_spatial=w_spatial, b_spatial=b_spatial, lb_raw=lb_raw, onorm_g=onorm_g,
             w_out=w_out, post_mix_g=post_mix_g, pre_x_g=pre_x_g, mem_norm_g=mem_norm_g,
             w_xq=w_xq, w_xkv=w_xkv, w_xo=w_xo, post_x_g=post_x_g, pre_ffn_g=pre_ffn_g,
             w_gu=w_gu, w_down=w_down, post_ffn_g=post_ffn_g)
    w = _kernel_params(p)
    depth = w_in.shape[0]
    return (_trunk(x_prompt, mem_prompt, w, depth), _trunk(x_sample, mem_sample, w, depth))
```

```python
import functools

import jax
import jax.numpy as jnp
from jax import lax
from jax.experimental import pallas as pl
from jax.experimental.pallas import tpu as pltpu

D_MODEL = 1024
A_WIDTH = 512
A_GROUPS = 4
A_GROUP_DIM = 128
A_CHUNK = 128
B_WIDTH = 512
B_HEADS = 4
B_DK = 128
MEM_TOKENS = 256
X_HEADS = 4
X_HEAD_DIM = 256
D_FF = 2816
EPS = 1e-6

SEQ_TILE = 1024
XATTN_TILE = 2048
ROW_BLOCK = 512
FFN_ROW_BLOCK = 256
SCAN_CHUNK = 64
SUB_BLOCK = 16
FACTORED_DECAY_LIMIT = 55.0
VMEM_LIMIT_BYTES = 60 * 1024 * 1024

_BF = jnp.bfloat16
_F32 = jnp.float32


def _dot(a, b):
    return jnp.dot(a.astype(_BF), b.astype(_BF), preferred_element_type=_F32)


def _dot_tb(a, b):
    return lax.dot_general(a.astype(_BF), b.astype(_BF), (((1,), (1,)), ((), ())),
                           preferred_element_type=_F32)


def _dot_ta(a, b):
    return lax.dot_general(a.astype(_BF), b.astype(_BF), (((0,), (0,)), ((), ())),
                           preferred_element_type=_F32)


def _rms(x, g):
    return x * lax.rsqrt(jnp.mean(x * x, axis=-1, keepdims=True) + EPS) * g


def _gelu(x):
    return 0.5 * x * (1.0 + lax.erf(x * (2.0 ** -0.5)))


def _silu(x):
    return x * (0.5 * jnp.tanh(0.5 * x) + 0.5)


def _lower_bound(lbraw_ref, layer):
    depth = lbraw_ref.shape[0]
    rows = [lbraw_ref[j:j + 1, :] for j in range(depth)]
    m = functools.reduce(jnp.maximum, rows)
    e = [jnp.exp(r - m) for r in rows]
    denom = functools.reduce(lambda a, b: a + b, e)
    return functools.reduce(lambda a, b: a + b, e[1:layer + 1]) / denom


def _skewed(order, *stages):
    order = list(order)
    vals = {}
    for i in range(len(order) + len(stages) - 1):
        for k, stage in enumerate(stages):
            if 0 <= i - k < len(order):
                r = order[i - k]
                vals[(k, r)] = stage(r, vals.pop((k - 1, r), None))


def _row_block(r, size=ROW_BLOCK):
    return slice(r * size, (r + 1) * size)


def _split3(x):
    hi = x.astype(_BF)
    r1 = x - hi.astype(_F32)
    mid = r1.astype(_BF)
    lo = (r1 - mid.astype(_F32)).astype(_BF)
    return hi, mid, lo


def _cumulative(g, fwd):
    n = g.shape[0]
    row = lax.broadcasted_iota(jnp.int32, (n, n), 0)
    col = lax.broadcasted_iota(jnp.int32, (n, n), 1)
    tri = jnp.where(col <= row if fwd else col >= row, 1.0, 0.0).astype(_BF)
    g_hi, g_mid, g_lo = _split3(g)
    return (jnp.dot(tri, g_hi, preferred_element_type=_F32)
            + jnp.dot(tri, g_mid, preferred_element_type=_F32)
            + jnp.dot(tri, g_lo, preferred_element_type=_F32))


def _hgrn_gates(f, rows, lb, scan_scratch):
    k_s, g_s = scan_scratch[1], scan_scratch[3]
    if lb is None:
        e = jnp.exp(-jnp.abs(f))
        g = jnp.minimum(f, 0.0) - jnp.log(1.0 + e)
        k_s[rows, :] = jnp.where(f >= 0.0, e, 1.0) / (1.0 + e)
    else:
        forget = lb + (1.0 - lb) * (0.5 * jnp.tanh(0.5 * f) + 0.5)
        g = jnp.log(forget)
        k_s[rows, :] = 1.0 - forget
    g_s[rows, :] = g
    return g


def _hgrn_cumulative(g, rows, scan_scratch, fwd):
    b_s = scan_scratch[4]
    for c in range(ROW_BLOCK // SCAN_CHUNK):
        lo = c * SCAN_CHUNK
        b_s[rows.start + lo:rows.start + lo + SCAN_CHUNK, :] = _cumulative(g[lo:lo + SCAN_CHUNK], fwd)


def _scan_exact_loop(scan_scratch, st_s, fwd, emit):
    q_s, k_s, v_s, g_s, b_s, kc_s, bc_s = scan_scratch[:7]
    C = SCAN_CHUNK
    n_chunks = q_s.shape[0] // C
    n_sub = C // SUB_BLOCK
    sub_row = lax.broadcasted_iota(jnp.int32, (SUB_BLOCK, C), 0)
    sub_col = lax.broadcasted_iota(jnp.int32, (SUB_BLOCK, C), 1)
    last = C - 1 if fwd else 0

    def chunk_body(c, carry):
        idx = c if fwd else n_chunks - 1 - c
        base = pl.multiple_of(idx * C, C)
        rows = pl.ds(base, C)
        bc_s[...] = b_s[rows, :]
        kc_s[...] = k_s[rows, :]
        for hd in range(B_HEADS):
            lanes = slice(hd * B_DK, (hd + 1) * B_DK)
            q = q_s[rows, lanes]
            k = kc_s[:, lanes]
            v = v_s[rows, lanes]
            b = bc_s[:, lanes]
            st = st_s[hd]
            b_tot = b[last:last + 1]
            o = _dot_tb(q * jnp.exp(b), st)
            kd = k * jnp.exp(b_tot - b)
            st_s[hd] = st * jnp.exp(b_tot) + _dot_ta(v, kd)

            a_rows = []
            for i in range(n_sub):
                lo, hi = i * SUB_BLOCK, (i + 1) * SUB_BLOCK
                qi, bi = q[lo:hi], b[lo:hi]
                if fwd and i > 0:
                    r = b[lo - 1:lo]
                    ke = k[:lo] * jnp.exp(r - b[:lo])
                    ke = jnp.concatenate([ke, jnp.zeros((C - lo, B_DK), _F32)], axis=0)
                    a_i = _dot_tb(qi * jnp.exp(bi - r), ke)
                elif (not fwd) and i < n_sub - 1:
                    r = b[hi:hi + 1]
                    ke = k[hi:] * jnp.exp(r - b[hi:])
                    ke = jnp.concatenate([jnp.zeros((hi, B_DK), _F32), ke], axis=0)
                    a_i = _dot_tb(qi * jnp.exp(bi - r), ke)
                else:
                    a_i = jnp.zeros((SUB_BLOCK, C), _F32)
                for s in range(SUB_BLOCK):
                    j = lo + s
                    k_row = kc_s[j:j + 1, lanes]
                    b_row = bc_s[j:j + 1, lanes]
                    p = qi * k_row * jnp.exp(jnp.minimum(bi - b_row, 0.0))
                    a_i = jnp.where(sub_col == j, jnp.sum(p, axis=-1, keepdims=True), a_i)
                local = sub_col - lo
                if fwd:
                    keep = (sub_col < lo) | ((sub_col < hi) & (sub_row >= local))
                else:
                    keep = (sub_col >= hi) | ((sub_col >= lo) & (sub_row <= local))
                a_rows.append(jnp.where(keep, a_i, 0.0))
            a = jnp.concatenate(a_rows, axis=0)
            emit(rows, lanes, o + _dot(a, v))
        return carry

    lax.fori_loop(0, n_chunks, chunk_body, 0)


def _scan_reference_rows(fwd):
    return (SCAN_CHUNK // 2 - 1, SCAN_CHUNK - 1) if fwd else (SCAN_CHUNK // 2, 0)


def _scan_factored_stages(scan_scratch, st_s, fwd, emit):
    q_s, k_s, v_s, g_s, b_s, kc_s, bc_s, x_s, y_s, a_s, u_s = scan_scratch
    C = SCAN_CHUNK
    mid, last = _scan_reference_rows(fwd)
    row = lax.broadcasted_iota(jnp.int32, (C, C), 0)
    col = lax.broadcasted_iota(jnp.int32, (C, C), 1)
    causal = (col <= row) if fwd else (col >= row)
    chunk_rows = lambda c: slice(c * C, (c + 1) * C)
    head_lanes = lambda hd: slice(hd * B_DK, (hd + 1) * B_DK)

    def chunks(r):
        per_block = ROW_BLOCK // C
        ids = list(range(r * per_block, (r + 1) * per_block))
        return ids if fwd else ids[::-1]

    def scale(r, _):
        for c in chunks(r):
            rows = chunk_rows(c)
            b = b_s[rows, :]
            e = jnp.exp(b - b[mid:mid + 1])
            x_s[rows, :] = (q_s[rows, :] * e).astype(_BF)
            y_s[rows, :] = (k_s[rows, :] / e).astype(_BF)

    def products(r, _):
        for c in chunks(r):
            rows = chunk_rows(c)
            for hd in range(B_HEADS):
                lanes = head_lanes(hd)
                yh = y_s[rows, lanes]
                a_s[hd, rows, :] = jnp.where(causal, _dot_tb(x_s[rows, lanes], yh), 0.0).astype(_BF)
                u_s[c, hd] = _dot_ta(v_s[rows, lanes], yh)

    def states(r, _):
        for hd in range(B_HEADS):
            lanes = head_lanes(hd)
            st = st_s[hd]
            for c in chunks(r):
                rho = b_s[c * C + mid:c * C + mid + 1, lanes]
                b_tot = b_s[c * C + last:c * C + last + 1, lanes]
                u = u_s[c, hd]
                u_s[c, hd] = st * jnp.exp(rho)
                st = st * jnp.exp(b_tot) + u * jnp.exp(b_tot - rho)
            st_s[hd] = st

    def outputs(r, _):
        for c in chunks(r):
            rows = chunk_rows(c)
            for hd in range(B_HEADS):
                lanes = head_lanes(hd)
                emit(rows, lanes, _dot_tb(x_s[rows, lanes], u_s[c, hd])
                     + _dot(a_s[hd, rows, :], v_s[rows, lanes]))

    return scale, products, states, outputs


def _scan_factored_ok(scan_scratch, fwd):
    b_s = scan_scratch[4]
    C = SCAN_CHUNK
    mid, last = _scan_reference_rows(fwd)
    worst = jnp.zeros((1, B_WIDTH), _F32)
    for c in range(b_s.shape[0] // C):
        b_mid = b_s[c * C + mid:c * C + mid + 1, :]
        b_tot = b_s[c * C + last:c * C + last + 1, :]
        worst = jnp.maximum(worst, jnp.maximum(-b_mid, b_mid - b_tot))
    return jnp.max(worst) < FACTORED_DECAY_LIMIT


def _begin_tile(st_s, st0_s):
    @pl.when(pl.program_id(1) == 0)
    def _():
        st_s[...] = jnp.zeros_like(st_s)
    st0_s[...] = st_s[...]


def _bwd_kernel(layer, x_ref, gpre_ref, w_ref, lbraw_ref, o_ref, q_ref, v_ref, *scratch):
    (k_s, g_s), rest, (st_s, st0_s) = scratch[:2], scratch[2:-2], scratch[-2:]
    scan_scratch = (q_ref.at[0], k_s, v_ref.at[0], g_s) + tuple(rest)
    _begin_tile(st_s, st0_s)
    lb = _lower_bound(lbraw_ref, layer) if layer > 0 else None

    def project(r, _):
        h = _rms(x_ref[0, _row_block(r), :], gpre_ref[...]).astype(_BF)
        return jnp.dot(h, w_ref[...], preferred_element_type=_F32)

    def gates(r, p):
        rows = _row_block(r)
        q_ref[0, rows, :] = _silu(p[:, :B_WIDTH]) * (B_DK ** -0.5)
        v_ref[0, rows, :] = p[:, 2 * B_WIDTH:].astype(_BF)
        return _hgrn_gates(p[:, B_WIDTH:2 * B_WIDTH], rows, lb, scan_scratch)

    def cumulate(r, g):
        _hgrn_cumulative(g, _row_block(r), scan_scratch, fwd=False)

    def emit(rows, lanes, o):
        o_ref[0, rows, lanes] = o

    blocks = reversed(range(x_ref.shape[1] // ROW_BLOCK))
    _skewed(blocks, project, gates, cumulate, *_scan_factored_stages(scan_scratch, st_s, False, emit))

    @pl.when(jnp.logical_not(_scan_factored_ok(scan_scratch, False)))
    def _():
        st_s[...] = st0_s[...]
        _scan_exact_loop(scan_scratch, st_s, False, emit)


def _mix_kernel(layer, x_ref, obwd_ref, q_ref, v_ref, gpre_ref, w_ref, vlng_ref, vlnb_ref, wsp_ref,
                bsp_ref, lbraw_ref, onorm_ref, wout_ref, gpost_ref, out_ref, *scratch):
    (k_s, g_s), rest, (gate_s, st_s, st0_s, cat_s) = scratch[:2], scratch[2:-4], scratch[-4:]
    scan_scratch = (q_ref.at[0], k_s, v_ref.at[0], g_s) + tuple(rest)
    _begin_tile(st_s, st0_s)
    lb = _lower_bound(lbraw_ref, layer) if layer > 0 else None
    blocks = range(x_ref.shape[1] // ROW_BLOCK)

    def project(r, _):
        h = _rms(x_ref[0, _row_block(r), :], gpre_ref[...]).astype(_BF)
        return jnp.dot(h, w_ref[...], preferred_element_type=_F32)

    def elementwise(r, p):
        rows = _row_block(r)
        u = _gelu(p[:, :A_WIDTH])
        v = _gelu(p[:, A_WIDTH:2 * A_WIDTH])
        vc = v - jnp.mean(v, axis=-1, keepdims=True)
        v = vc * lax.rsqrt(jnp.mean(vc * vc, axis=-1, keepdims=True) + EPS) * vlng_ref[...] + vlnb_ref[...]
        g = _hgrn_gates(p[:, 2 * A_WIDTH:2 * A_WIDTH + B_WIDTH], rows, lb, scan_scratch)
        gate_s[rows, :] = _silu(p[:, 2 * A_WIDTH + B_WIDTH:])
        return u, v.astype(_BF), g

    def dependent_matmuls(r, uvg):
        rows = _row_block(r)
        u, v, g = uvg
        chunks = [slice(n * A_CHUNK, (n + 1) * A_CHUNK) for n in range(ROW_BLOCK // A_CHUNK)]
        for gi in range(A_GROUPS):
            lanes = slice(gi * A_GROUP_DIM, (gi + 1) * A_GROUP_DIM)
            sv = jnp.dot(wsp_ref[gi], jnp.concatenate([v[c, lanes] for c in chunks], axis=1),
                         preferred_element_type=_F32)
            for n, c in enumerate(chunks):
                cat_s[rows.start + c.start:rows.start + c.stop, lanes] = (
                    u[c, lanes] * (sv[:, n * A_GROUP_DIM:(n + 1) * A_GROUP_DIM] + bsp_ref[gi])).astype(_BF)
        _hgrn_cumulative(g, rows, scan_scratch, fwd=True)

    def emit(rows, lanes, o):
        o_sum = o + obwd_ref[0, rows, lanes]
        cat_s[rows, A_WIDTH + lanes.start:A_WIDTH + lanes.stop] = (
            _rms(o_sum, onorm_ref[...]) * gate_s[rows, lanes]).astype(_BF)

    def out_proj(r, _):
        return jnp.dot(cat_s[_row_block(r), :], wout_ref[...], preferred_element_type=_F32)

    def residual(r, y):
        rows = _row_block(r)
        out_ref[0, rows, :] = x_ref[0, rows, :] + _rms(y, gpost_ref[...])

    _skewed(blocks, project, elementwise, dependent_matmuls,
            *_scan_factored_stages(scan_scratch, st_s, True, emit), out_proj, residual)

    @pl.when(jnp.logical_not(_scan_factored_ok(scan_scratch, True)))
    def _():
        st_s[...] = st0_s[...]
        _scan_exact_loop(scan_scratch, st_s, True, emit)
        _skewed(blocks, out_proj, residual)


def _xattn_kernel(x_ref, mem_ref, gmem_ref, wkv_ref, gpre_ref, wq_ref, wo_ref, gpost_ref, out_ref,
                  cat_s, k_ref, v_ref):
    head_lanes = lambda hd: slice(hd * X_HEAD_DIM, (hd + 1) * X_HEAD_DIM)

    @pl.when(pl.program_id(1) == 0)
    def _():
        kv = _dot(_rms(mem_ref[0], gmem_ref[...]), wkv_ref[...])
        k_ref[0] = kv[:, :D_MODEL].astype(_BF)
        v_ref[0] = kv[:, D_MODEL:].astype(_BF)

    def query(r, _):
        h = _rms(x_ref[0, _row_block(r), :], gpre_ref[...])
        return _dot(h, wq_ref[...]).astype(_BF)

    def scores(r, q):
        return [_dot_tb(q[:, head_lanes(hd)], k_ref[0, :, head_lanes(hd)]) for hd in range(X_HEADS)]

    def softmax(r, s_heads):
        p_heads = []
        for s in s_heads:
            s = s * (X_HEAD_DIM ** -0.5)
            e = jnp.exp(s - jnp.max(s, axis=-1, keepdims=True))
            p_heads.append((e / jnp.sum(e, axis=-1, keepdims=True)).astype(_BF))
        return p_heads

    def values(r, p_heads):
        for hd, p in enumerate(p_heads):
            cat_s[_row_block(r), head_lanes(hd)] = _dot(p, v_ref[0, :, head_lanes(hd)]).astype(_BF)

    def out_proj(r, _):
        return jnp.dot(cat_s[_row_block(r), :], wo_ref[...], preferred_element_type=_F32)

    def residual(r, y):
        rows = _row_block(r)
        out_ref[0, rows, :] = x_ref[0, rows, :] + _rms(y, gpost_ref[...])

    _skewed(range(x_ref.shape[1] // ROW_BLOCK), query, scores, softmax, values, out_proj, residual)


def _ffn_kernel(x_ref, gpre_ref, wgu_ref, wdown_ref, gpost_ref, out_ref):
    def gate_up(r, _):
        h = _rms(x_ref[0, _row_block(r, FFN_ROW_BLOCK), :], gpre_ref[...]).astype(_BF)
        return jnp.dot(h, wgu_ref[...], preferred_element_type=_F32)

    def activate(r, p):
        return (_silu(p[:, :D_FF]) * p[:, D_FF:]).astype(_BF)

    def down(r, act):
        return jnp.dot(act, wdown_ref[...], preferred_element_type=_F32)

    def residual(r, y):
        rows = _row_block(r, FFN_ROW_BLOCK)
        out_ref[0, rows, :] = x_ref[0, rows, :] + _rms(y, gpost_ref[...])

    _skewed(range(x_ref.shape[1] // FFN_ROW_BLOCK), gate_up, activate, down, residual)


def _params():
    return pltpu.CompilerParams(dimension_semantics=("arbitrary", "arbitrary"),
                                vmem_limit_bytes=VMEM_LIMIT_BYTES)


def _param_spec(stacked, index):
    return pl.BlockSpec((None,) + stacked.shape[1:], lambda b, t: (index,) + (0,) * (stacked.ndim - 1),
                        pipeline_mode=pl.Buffered(1))


def _tile_spec(width, rev_tiles=None, tile=SEQ_TILE):
    if rev_tiles is None:
        return pl.BlockSpec((1, tile, width), lambda b, t: (b, t, 0))
    return pl.BlockSpec((1, tile, width), lambda b, t: (b, rev_tiles - 1 - t, 0))


def _scan_scratch():
    tile = lambda dt: pltpu.VMEM((SEQ_TILE, B_WIDTH), dt)
    chunk = lambda: pltpu.VMEM((SCAN_CHUNK, B_WIDTH), _F32)
    return [tile(_F32), tile(_F32), tile(_F32), chunk(), chunk(), tile(_BF), tile(_BF),
            pltpu.VMEM((B_HEADS, SEQ_TILE, SCAN_CHUNK), _BF),
            pltpu.VMEM((SEQ_TILE // SCAN_CHUNK, B_HEADS, B_DK, B_DK), _F32)]


def _state_scratch():
    return pltpu.VMEM((B_HEADS, B_DK, B_DK), _F32)


def _bwd_call(x, layer, gpre, w_bwd, lbraw):
    bn, ln, _ = x.shape
    nt = ln // SEQ_TILE
    return pl.pallas_call(
        functools.partial(_bwd_kernel, layer),
        grid=(bn, nt),
        in_specs=[_tile_spec(D_MODEL, nt), _param_spec(gpre, layer), _param_spec(w_bwd, layer),
                  _param_spec(lbraw, 1)],
        out_specs=[_tile_spec(B_WIDTH, nt)] * 3,
        out_shape=[jax.ShapeDtypeStruct((bn, ln, B_WIDTH), _F32),
                   jax.ShapeDtypeStruct((bn, ln, B_WIDTH), _F32),
                   jax.ShapeDtypeStruct((bn, ln, B_WIDTH), _BF)],
        scratch_shapes=_scan_scratch() + [_state_scratch(), _state_scratch()],
        compiler_params=_params(),
        name=f"hgrn_bwd_l{layer}",
    )(x, gpre, w_bwd, lbraw)


def _mix_call(x, obwd, q, v, layer, gpre, w_mix, vlng, vlnb, wsp, bsp, lbraw, onorm, wout, gpost):
    bn, ln, _ = x.shape
    nt = ln // SEQ_TILE
    consts = (gpre, w_mix, vlng, vlnb, wsp, bsp, lbraw, onorm, wout, gpost)
    return pl.pallas_call(
        functools.partial(_mix_kernel, layer),
        grid=(bn, nt),
        in_specs=[_tile_spec(D_MODEL)] + [_tile_spec(B_WIDTH)] * 3
                 + [_param_spec(c, 0 if c is lbraw else layer) for c in consts],
        out_specs=_tile_spec(D_MODEL),
        out_shape=jax.ShapeDtypeStruct(x.shape, _F32),
        scratch_shapes=_scan_scratch() + [pltpu.VMEM((SEQ_TILE, B_WIDTH), _F32), _state_scratch(),
                                          _state_scratch(),
                                          pltpu.VMEM((SEQ_TILE, D_MODEL), _BF)],
        compiler_params=_params(),
        name=f"token_mix_l{layer}",
    )(x, obwd, q, v, *consts)


def _xattn_call(x, mem, layer, gmem, wkv, gpre, wq, wo, gpost):
    bn, ln, _ = x.shape
    nt = ln // XATTN_TILE
    mem_spec = pl.BlockSpec((1, MEM_TOKENS, D_MODEL), lambda b, t: (b, 0, 0))
    consts = (gmem, wkv, gpre, wq, wo, gpost)
    x_spec = _tile_spec(D_MODEL, tile=XATTN_TILE)
    kv_scratch = pltpu.VMEM((1, MEM_TOKENS, D_MODEL), _BF)
    return pl.pallas_call(
        _xattn_kernel,
        grid=(bn, nt),
        in_specs=[x_spec, mem_spec] + [_param_spec(c, layer) for c in consts],
        out_specs=x_spec,
        out_shape=jax.ShapeDtypeStruct(x.shape, _F32),
        scratch_shapes=[pltpu.VMEM((XATTN_TILE, D_MODEL), _BF), kv_scratch, kv_scratch],
        compiler_params=_params(),
        name=f"mem_xattn_l{layer}",
    )(x, mem, *consts)


def _ffn_call(x, layer, gpre, wgu, wdown, gpost):
    bn, ln, _ = x.shape
    nt = ln // SEQ_TILE
    consts = (gpre, wgu, wdown, gpost)
    return pl.pallas_call(
        _ffn_kernel,
        grid=(bn, nt),
        in_specs=[_tile_spec(D_MODEL)] + [_param_spec(c, layer) for c in consts],
        out_specs=_tile_spec(D_MODEL),
        out_shape=jax.ShapeDtypeStruct(x.shape, _F32),
        compiler_params=_params(),
        name=f"swiglu_ffn_l{layer}",
    )(x, *consts)


def _kernel_params(p):
    w_in = p['w_in']
    q0, ff0, fb0, g0 = 2 * A_WIDTH, 2 * A_WIDTH + B_WIDTH, 2 * A_WIDTH + 2 * B_WIDTH, 2 * A_WIDTH + 4 * B_WIDTH
    cols = lambda *ranges: jnp.concatenate([w_in[:, :, a:b] for a, b in ranges], axis=2).astype(_BF)
    row = lambda a: a[:, None, :]
    bf = lambda a: a.astype(_BF)
    depth = w_in.shape[0]
    return dict(
        w_mix=cols((0, q0), (ff0, fb0), (g0, g0 + B_WIDTH)),
        w_bwd=cols((q0, ff0), (fb0, g0)),
        pre_mix_g=row(p['pre_mix_g']), v_ln_g=row(p['v_ln_g']), v_ln_b=row(p['v_ln_b']),
        w_spatial=bf(p['w_spatial']),
        b_spatial=jnp.broadcast_to(p['b_spatial'][:, :, :, None], (depth, A_GROUPS, A_CHUNK, A_GROUP_DIM)),
        lb_raw=p['lb_raw'],
        onorm_g=row(p['onorm_g']), w_out=bf(p['w_out']), post_mix_g=row(p['post_mix_g']),
        pre_x_g=row(p['pre_x_g']), mem_norm_g=row(p['mem_norm_g']),
        w_xq=bf(p['w_xq']), w_xkv=bf(p['w_xkv']), w_xo=bf(p['w_xo']),
        post_x_g=row(p['post_x_g']), pre_ffn_g=row(p['pre_ffn_g']),
        w_gu=bf(p['w_gu']), w_down=bf(p['w_down']), post_ffn_g=row(p['post_ffn_g']),
    )


def _trunk(x, mem, w, depth):
    assert x.shape[1] % SEQ_TILE == 0 and x.shape[1] % XATTN_TILE == 0 and x.shape[2] == D_MODEL
    assert SEQ_TILE % ROW_BLOCK == 0 and ROW_BLOCK % SCAN_CHUNK == 0 and ROW_BLOCK % A_CHUNK == 0
    assert SEQ_TILE % FFN_ROW_BLOCK == 0
    for l in range(depth):
        obwd, q, v = _bwd_call(x, l, w['pre_mix_g'], w['w_bwd'], w['lb_raw'])
        x = _mix_call(x, obwd, q, v, l, w['pre_mix_g'], w['w_mix'], w['v_ln_g'], w['v_ln_b'], w['w_spatial'],
                      w['b_spatial'], w['lb_raw'], w['onorm_g'], w['w_out'], w['post_mix_g'])
        x = _xattn_call(x, mem, l, w['mem_norm_g'], w['w_xkv'], w['pre_x_g'], w['w_xq'], w['w_xo'],
                        w['post_x_g'])
        x = _ffn_call(x, l, w['pre_ffn_g'], w['w_gu'], w['w_down'], w['post_ffn_g'])
    return x


def kernel(x_prompt, x_sample, mem_prompt, mem_sample, pre_mix_g, w_in, v_ln_g, v_ln_b, w_spatial, b_spatial, lb_raw, onorm_g, w_out, post_mix_g, pre_x_g, mem_norm_g, w_xq, w_xkv, w_xo, post_x_g, pre_ffn_g, w_gu, w_down, post_ffn_g):
    p = dict(pre_mix_g=pre_mix_g, w_in=w_in, v_ln_g=v_ln_g, v_ln_b=v_ln_b,
             w_spatial=w_spatial, b_spatial=b_spatial, lb_raw=lb_raw, onorm_g=onorm_g,
             w_out=w_out, post_mix_g=post_mix_g, pre_x_g=pre_x_g, mem_norm_g=mem_norm_g,
             w_xq=w_xq, w_xkv=w_xkv, w_xo=w_xo, post_x_g=post_x_g, pre_ffn_g=pre_ffn_g,
             w_gu=w_gu, w_down=w_down, post_ffn_g=post_ffn_g)
    w = _kernel_params(p)
    depth = w_in.shape[0]
    return (_trunk(x_prompt, mem_prompt, w, depth), _trunk(x_sample, mem_sample, w, depth))
```
